```python
import jax, jax.numpy as jnp
from jax import lax
import numpy as np

D_MODEL = 1024
BATCH = 16
SEQ = 256
DEPTH = 2
DEC_BATCH = 4
DEC_SEQ = 4096
PAST_LEN = 256

GRID_W = 64
D_MIX = D_MODEL
HEAD_DIM = 64
A_HEADS = D_MIX // 128
A_KV_HEADS = A_HEADS // 4
A_GROUP = A_HEADS // A_KV_HEADS
A_WIDTH = A_HEADS * HEAD_DIM
A_KV_WIDTH = A_KV_HEADS * HEAD_DIM
WINDOW = 128
BLOCK = 128
B_WIDTH = D_MIX // 4
B_GROUP_DIM = 64
B_GROUPS = B_WIDTH // B_GROUP_DIM
C_HEADS = D_MIX // 256
C_NOPE = 64
C_ROPE = 32
C_V = 64
C_Q_LORA = 192
C_KV_LORA = 128
C_WIDTH = C_HEADS * C_V
IN_COLS = A_WIDTH + 2 * A_KV_WIDTH + B_WIDTH + C_Q_LORA + C_KV_LORA + C_ROPE
D_FF = 2816
CONV_W = 3
ROPE_BASE = 10000.0
EPS = 1e-6
NEG = -1e30

kernel_name = "hybrid_diffusion_parallel_heads_step"


def _rms_norm(x, g):
    xf = x.astype(jnp.float32)
    y = xf * lax.rsqrt(jnp.mean(xf * xf, axis=-1, keepdims=True) + EPS)
    return (y * g.astype(jnp.float32)).astype(x.dtype)


def _modulation(cvec, w_ada, b_ada):
    m = jax.nn.silu(cvec) @ w_ada + b_ada
    return jnp.split(m[:, None, :], 6, axis=-1)


def _axial_tables(n_tok, dim):
    rows = n_tok // GRID_W
    r = jnp.repeat(jnp.arange(rows), GRID_W).astype(jnp.float32)
    col = jnp.tile(jnp.arange(GRID_W), rows).astype(jnp.float32)
    quarter = dim // 4
    inv = ROPE_BASE ** (-jnp.arange(quarter, dtype=jnp.float32) / quarter)
    ang_r = r[:, None] * inv
    ang_c = col[:, None] * inv
    return (jnp.cos(ang_r), jnp.sin(ang_r), jnp.cos(ang_c), jnp.sin(ang_c))


def _rot(x, cos, sin):
    x1, x2 = jnp.split(x, 2, axis=-1)
    return jnp.concatenate([x1 * cos - x2 * sin, x2 * cos + x1 * sin], axis=-1)


def _apply_axial_rope(x, tables):
    cr, sr, cc, sc = [t.astype(x.dtype)[:, None, :] for t in tables]
    half = x.shape[-1] // 2
    return jnp.concatenate([_rot(x[..., :half], cr, sr), _rot(x[..., half:], cc, sc)], axis=-1)


def _split_proj(p):
    sizes = [A_WIDTH, A_KV_WIDTH, A_KV_WIDTH, B_WIDTH, C_Q_LORA, C_KV_LORA, C_ROPE]
    idx = np.cumsum(sizes)[:-1].tolist()
    return jnp.split(p, idx, axis=-1)


def _sink_softmax(logits, sink_b):
    p = jax.nn.softmax(jnp.concatenate([logits, sink_b], axis=-1), axis=-1)
    return p[..., :-1]


def _ctx_attn_a(q, k, v, sink):
    B, n = q.shape[:2]
    qg = q.reshape(B, n, A_KV_HEADS, A_GROUP, HEAD_DIM)
    s = jnp.einsum('bqhgd,bkhd->bhgqk', qg, k, preferred_element_type=jnp.float32) * (HEAD_DIM ** -0.5)
    sink_b = jnp.broadcast_to(sink.astype(jnp.float32).reshape(1, A_KV_HEADS, A_GROUP, 1, 1), s.shape[:-1] + (1,))
    p = _sink_softmax(s, sink_b)
    out = jnp.einsum('bhgqk,bkhd->bqhgd', p.astype(v.dtype), v)
    return out.reshape(B, n, A_WIDTH)


def _latent_attn_a(q, k, v, k_ctx, v_ctx, sink):
    B, n = q.shape[:2]
    nb = n // BLOCK
    qb = q.reshape(B, nb, BLOCK, A_KV_HEADS, A_GROUP, HEAD_DIM)
    pad = ((0, 0), (BLOCK, BLOCK), (0, 0), (0, 0))
    kp = jnp.pad(k, pad).reshape(B, nb + 2, BLOCK, A_KV_HEADS, HEAD_DIM)
    vp = jnp.pad(v, pad).reshape(B, nb + 2, BLOCK, A_KV_HEADS, HEAD_DIM)
    kband = jnp.concatenate([kp[:, :-2], kp[:, 1:-1], kp[:, 2:]], axis=2)
    vband = jnp.concatenate([vp[:, :-2], vp[:, 1:-1], vp[:, 2:]], axis=2)
    scale = HEAD_DIM ** -0.5
    s_band = jnp.einsum('bnqhgd,bnkhd->bnhgqk', qb, kband, preferred_element_type=jnp.float32) * scale
    qi = jnp.arange(BLOCK)
    kj = jnp.arange(3 * BLOCK)
    rel = qi[:, None] - kj[None, :] + BLOCK
    key_pos = (jnp.arange(nb)[:, None] - 1) * BLOCK + kj[None, :]
    in_range = (key_pos >= 0) & (key_pos < n)
    mask = (jnp.abs(rel) <= WINDOW)[None] & in_range[:, None, :]
    s_band = jnp.where(mask[None, :, None, None], s_band, NEG)
    s_ctx = jnp.einsum('bnqhgd,bchd->bnhgqc', qb, k_ctx, preferred_element_type=jnp.float32) * scale
    sink_b = jnp.broadcast_to(sink.astype(jnp.float32).reshape(1, 1, A_KV_HEADS, A_GROUP, 1, 1), s_band.shape[:-1] + (1,))
    p = _sink_softmax(jnp.concatenate([s_band, s_ctx], axis=-1), sink_b)
    p_band = p[..., :3 * BLOCK].astype(v.dtype)
    p_ctx = p[..., 3 * BLOCK:].astype(v.dtype)
    out = (jnp.einsum('bnhgqk,bnkhd->bnqhgd', p_band, vband)
           + jnp.einsum('bnhgqc,bchd->bnqhgd', p_ctx, v_ctx))
    return out.reshape(B, n, A_WIDTH)


def _fourier_mix(f):
    B, n, _ = f.shape
    fg = f.reshape(B, n, B_GROUPS, B_GROUP_DIM).astype(jnp.float32)
    y = jnp.fft.fft2(fg, axes=(1, 3), norm='ortho').real
    return y.reshape(B, n, B_WIDTH).astype(f.dtype)


def _mla_q(cq, g_cq, w_uq):
    B, n = cq.shape[:2]
    q = (_rms_norm(cq, g_cq) @ w_uq).reshape(B, n, C_HEADS, C_NOPE + C_ROPE)
    return q[..., :C_NOPE], q[..., C_NOPE:]


def _mla_kv(ckv_n, kr, w_ukv):
    B, n = ckv_n.shape[:2]
    kv = (ckv_n @ w_ukv).reshape(B, n, C_HEADS, C_NOPE + C_V)
    k = jnp.concatenate([kv[..., :C_NOPE], jnp.broadcast_to(kr[:, :, None, :], (B, n, C_HEADS, C_ROPE))], axis=-1)
    return k, kv[..., C_NOPE:]


def _ctx_mla(q, k, v):
    B, n = q.shape[:2]
    s = jnp.einsum('bqhd,bkhd->bhqk', q, k, preferred_element_type=jnp.float32) * ((C_NOPE + C_ROPE) ** -0.5)
    p = jax.nn.softmax(s, axis=-1).astype(v.dtype)
    return jnp.einsum('bhqk,bkhd->bqhd', p, v).reshape(B, n, C_WIDTH)


def _latent_mla(q, k, v, k_ctx, v_ctx):
    B, n = q.shape[:2]
    nb = n // BLOCK
    scale = (C_NOPE + C_ROPE) ** -0.5
    qblocks = q.reshape(B, nb, BLOCK, C_HEADS, C_NOPE + C_ROPE).transpose(1, 0, 2, 3, 4)

    def one_block(qb):
        s_lat = jnp.einsum('bqhd,bkhd->bhqk', qb, k, preferred_element_type=jnp.float32)
        s_ctx = jnp.einsum('bqhd,bkhd->bhqk', qb, k_ctx, preferred_element_type=jnp.float32)
        p = jax.nn.softmax(jnp.concatenate([s_lat, s_ctx], axis=-1) * scale, axis=-1).astype(v.dtype)
        return (jnp.einsum('bhqk,bkhd->bqhd', p[..., :n], v)
                + jnp.einsum('bhqk,bkhd->bqhd', p[..., n:], v_ctx))

    out = lax.map(one_block, qblocks)
    return out.transpose(1, 0, 2, 3, 4).reshape(B, n, C_WIDTH)


def _mixer_context(h, w_in, sink, g_cq, w_uq, g_ckv, w_ukv):
    B, n, _ = h.shape
    qa, ka, va, fb, cq, ckv, kr = _split_proj(h @ w_in)
    qa = qa.reshape(B, n, A_HEADS, HEAD_DIM)
    ka = ka.reshape(B, n, A_KV_HEADS, HEAD_DIM)
    va = va.reshape(B, n, A_KV_HEADS, HEAD_DIM)
    out_a = _ctx_attn_a(qa, ka, va, sink)
    out_b = _fourier_mix(fb)
    ckv_n = _rms_norm(ckv, g_ckv)
    q_nope, q_rope = _mla_q(cq, g_cq, w_uq)
    kc, vc = _mla_kv(ckv_n, kr, w_ukv)
    out_c = _ctx_mla(jnp.concatenate([q_nope, q_rope], axis=-1), kc, vc)
    return jnp.concatenate([out_a, out_b, out_c], axis=-1), ka, va, ckv_n, kr


def _mixer_latent(h, k_ctx_a, v_ctx_a, ckv_ctx, kr_ctx, w_in, sink, g_cq, w_uq, g_ckv, w_ukv):
    B, n, _ = h.shape
    qa, ka, va, fb, cq, ckv, kr = _split_proj(h @ w_in)
    tab_a = _axial_tables(n, HEAD_DIM)
    tab_c = _axial_tables(n, C_ROPE)
    qa = _apply_axial_rope(qa.reshape(B, n, A_HEADS, HEAD_DIM), tab_a)
    ka = _apply_axial_rope(ka.reshape(B, n, A_KV_HEADS, HEAD_DIM), tab_a)
    va = va.reshape(B, n, A_KV_HEADS, HEAD_DIM)
    out_a = _latent_attn_a(qa, ka, va, k_ctx_a, v_ctx_a, sink)
    out_b = _fourier_mix(fb)
    ckv_n = _rms_norm(ckv, g_ckv)
    q_nope, q_rope = _mla_q(cq, g_cq, w_uq)
    q_rope = _apply_axial_rope(q_rope, tab_c)
    kr = _apply_axial_rope(kr[:, :, None, :], tab_c)[:, :, 0]
    kc, vc = _mla_kv(ckv_n, kr, w_ukv)
    kc_ctx, vc_ctx = _mla_kv(ckv_ctx, kr_ctx, w_ukv)
    out_c = _latent_mla(jnp.concatenate([q_nope, q_rope], axis=-1), kc, vc, kc_ctx, vc_ctx)
    return jnp.concatenate([out_a, out_b, out_c], axis=-1)


def _conv_ffn(h, w_ug, conv_w, conv_b, w_down):
    n = h.shape[1]
    u = h @ w_ug
    half = CONV_W // 2
    up = jnp.pad(u, ((0, 0), (half, half), (0, 0)))
    uc = sum(up[:, i:i + n] * conv_w[i] for i in range(CONV_W)) + conv_b
    a, g = jnp.split(uc, 2, axis=-1)
    return (jax.nn.silu(g) * a) @ w_down


def setup_inputs(seed: int = 0) -> dict:
    key = jax.random.key(seed)
    ks = jax.random.split(key, 26)
    f32 = jnp.float32

    def nrm(k, shape, scale):
        return jax.random.normal(k, shape, f32) * scale

    return {
        "x_prompt": nrm(ks[0], (BATCH, SEQ, D_MODEL), 1.0),
        "x_sample": nrm(ks[1], (DEC_BATCH, DEC_SEQ, D_MODEL), 1.0),
        "cache_win_k": nrm(ks[2], (DEC_BATCH, DEPTH, PAST_LEN, A_KV_HEADS, HEAD_DIM), 1.0),
        "cache_win_v": nrm(ks[3], (DEC_BATCH, DEPTH, PAST_LEN, A_KV_HEADS, HEAD_DIM), 1.0),
        "cache_mla_ckv": nrm(ks[4], (DEC_BATCH, DEPTH, PAST_LEN, C_KV_LORA), 1.0),
        "cache_mla_krope": nrm(ks[5], (DEC_BATCH, DEPTH, PAST_LEN, C_ROPE), 1.0),
        "c": nrm(ks[6], (DEC_BATCH, D_MODEL), 1.0),
        "c_ctx": nrm(ks[7], (D_MODEL,), 1.0),
        "w_ada": nrm(ks[8], (DEPTH, D_MODEL, 6 * D_MODEL), 0.5 * D_MODEL ** -0.5),
        "b_ada": nrm(ks[9], (DEPTH, 6 * D_MODEL), 0.02),
        "g_mix": 1.0 + nrm(ks[10], (DEPTH, D_MODEL), 0.05),
        "w_in": nrm(ks[11], (DEPTH, D_MODEL, IN_COLS), D_MODEL ** -0.5),
        "sink": nrm(ks[12], (DEPTH, A_HEADS), 0.5),
        "g_cq": 1.0 + nrm(ks[13], (DEPTH, C_Q_LORA), 0.05),
        "w_uq": nrm(ks[14], (DEPTH, C_Q_LORA, C_HEADS * (C_NOPE + C_ROPE)), C_Q_LORA ** -0.5),
        "g_ckv": 1.0 + nrm(ks[15], (DEPTH, C_KV_LORA), 0.05),
        "w_ukv": nrm(ks[16], (DEPTH, C_KV_LORA, C_HEADS * (C_NOPE + C_V)), C_KV_LORA ** -0.5),
        "w_out": nrm(ks[17], (DEPTH, D_MIX, D_MODEL), D_MIX ** -0.5),
        "g_ffn": 1.0 + nrm(ks[18], (DEPTH, D_MODEL), 0.05),
        "w_ug": nrm(ks[19], (DEPTH, D_MODEL, 2 * D_FF), D_MODEL ** -0.5),
        "conv_w": nrm(ks[20], (DEPTH, CONV_W, 2 * D_FF), CONV_W ** -0.5),
        "conv_b": nrm(ks[21], (DEPTH, 2 * D_FF), 0.02),
        "w_down": nrm(ks[22], (DEPTH, D_FF, D_MODEL), D_FF ** -0.5),
        "g_final": 1.0 + nrm(ks[23], (D_MODEL,), 0.05),
    }


def reference(x_prompt, x_sample, cache_win_k, cache_win_v, cache_mla_ckv, cache_mla_krope,
              c, c_ctx, w_ada, b_ada, g_mix, w_in, sink, g_cq, w_uq, g_ckv, w_ukv, w_out,
              g_ffn, w_ug, conv_w, conv_b, w_down, g_final):
    xp = x_prompt
    xs = x_sample
    new_k, new_v, new_ckv, new_kr = [], [], [], []
    for l in range(DEPTH):
        sh1, sc1, gt1, sh2, sc2, gt2 = _modulation(c_ctx[None, :], w_ada[l], b_ada[l])
        h = _rms_norm(xp, g_mix[l]) * (1.0 + sc1) + sh1
        mix, ka, va, ckv_n, kr = _mixer_context(h, w_in[l], sink[l], g_cq[l], w_uq[l], g_ckv[l], w_ukv[l])
        xp = xp + gt1 * (mix @ w_out[l])
        h = _rms_norm(xp, g_ffn[l]) * (1.0 + sc2) + sh2
        xp = xp + gt2 * _conv_ffn(h, w_ug[l], conv_w[l], conv_b[l], w_down[l])
        new_k.append(ka)
        new_v.append(va)
        new_ckv.append(ckv_n)
        new_kr.append(kr)
        sh1, sc1, gt1, sh2, sc2, gt2 = _modulation(c, w_ada[l], b_ada[l])
        h = _rms_norm(xs, g_mix[l]) * (1.0 + sc1) + sh1
        mix = _mixer_latent(h, cache_win_k[:, l], cache_win_v[:, l], cache_mla_ckv[:, l], cache_mla_krope[:, l],
                            w_in[l], sink[l], g_cq[l], w_uq[l], g_ckv[l], w_ukv[l])
        xs = xs + gt1 * (mix @ w_out[l])
        h = _rms_norm(xs, g_ffn[l]) * (1.0 + sc2) + sh2
        xs = xs + gt2 * _conv_ffn(h, w_ug[l], conv_w[l], conv_b[l], w_down[l])
    y_prompt = _rms_norm(xp, g_final)
    y_sample = _rms_norm(xs, g_final)
    state_win_k = jnp.stack(new_k, axis=1)
    state_win_v = jnp.stack(new_v, axis=1)
    state_mla_ckv = jnp.stack(new_ckv, axis=1)
    state_mla_krope = jnp.stack(new_kr, axis=1)
    return (y_prompt, y_sample, state_win_k, state_win_v, state_mla_ckv, state_mla_krope)
```

```python
import functools

import numpy as np
import jax
import jax.numpy as jnp
from jax import lax
from jax.experimental import pallas as pl
from jax.experimental.pallas import tpu as pltpu

F32 = jnp.float32
BF16 = jnp.bfloat16

D_MODEL = 1024
BATCH = 16
SEQ = 256
DEPTH = 2
DEC_BATCH = 4
DEC_SEQ = 4096
PAST_LEN = 256
GRID_W = 64
HEAD_DIM = 64
A_HEADS = 8
A_KV_HEADS = 2
A_GROUP = 4
A_WIDTH = 512
A_KV_WIDTH = 128
WINDOW = 128
BLOCK = 128
B_WIDTH = 256
B_GROUP_DIM = 64
B_GROUPS = 4
C_HEADS = 4
C_NOPE = 64
C_ROPE = 32
C_V = 64
C_Q_LORA = 192
C_KV_LORA = 128
C_WIDTH = 256
D_FF = 2816
ROPE_BASE = 10000.0
EPS = 1e-6
NEG = -1e30

LANES = 128
BF16_ROWS = 16
C_HEAD_PAD = 128
CQ_PAD = 256
VMEM_LIMIT = 56 * 1024 * 1024

_QA0, _KA0, _VA0, _FB0, _CQ0, _CKV0, _KR0, _IN_PAD = 0, 512, 640, 768, 1024, 1280, 1408, 1536
KR_LANE0 = C_NOPE

T_PROJ = 512
T_OUT = 512
T_FFN = 1024
F_CHUNK = 256
TQ_MLA = 512
TK_MLA = 256
FOURIER_COLS = 8


def _cparams(sem):
    return pltpu.CompilerParams(dimension_semantics=sem, vmem_limit_bytes=VMEM_LIMIT)


def _dot(a, b):
    return jnp.dot(a, b, preferred_element_type=F32)


def _dot_nt(a, b):
    return lax.dot_general(a, b, (((1,), (1,)), ((), ())), preferred_element_type=F32)


def _rms(x, g, n):
    ms = jnp.sum(x * x, axis=-1, keepdims=True) * (1.0 / n)
    return x * lax.rsqrt(ms + EPS) * g


def _mod_kernel(c_ref, w_ref, b_ref, o_ref):
    cv = c_ref[...]
    s = cv * jax.nn.sigmoid(cv)
    o_ref[...] = jnp.dot(s, w_ref[...], preferred_element_type=F32,
                         precision=lax.Precision.HIGHEST) + b_ref[...]


def _modulation(cvecs, w_ada, b_ada):
    nj = 6
    return pl.pallas_call(
        _mod_kernel,
        grid=(DEPTH, nj),
        in_specs=[
            pl.BlockSpec((8, D_MODEL), lambda l, j: (0, 0)),
            pl.BlockSpec((None, D_MODEL, D_MODEL), lambda l, j: (l, 0, j)),
            pl.BlockSpec((None, 1, D_MODEL), lambda l, j: (l, 0, j)),
        ],
        out_specs=pl.BlockSpec((None, 8, D_MODEL), lambda l, j: (l, 0, j)),
        out_shape=jax.ShapeDtypeStruct((DEPTH, 8, 6 * D_MODEL), F32),
        compiler_params=_cparams(("arbitrary", "arbitrary")),
        name="modulation",
    )(cvecs, w_ada, b_ada.reshape(DEPTH, 1, 6 * D_MODEL))


def _rope_block(x, cos, sin, half):
    lane = lax.broadcasted_iota(jnp.int32, x.shape, 1)
    first = (lane % (2 * half)) < half
    partner = jnp.where(first, pltpu.roll(x, LANES - half, 1), pltpu.roll(x, half, 1))
    return x * cos + partner * sin


def _rope(x, cos, sin, half):
    blocks = [_rope_block(x[:, j:j + LANES], cos, sin, half) for j in range(0, x.shape[1], LANES)]
    return blocks[0] if len(blocks) == 1 else jnp.concatenate(blocks, axis=1)


def _proj_kernel(latent, x_ref, mod_ref, g_ref, win_ref, gcq_ref, wuq_ref, gckv_ref, wukv_ref, *rest):
    if latent:
        cosa_ref, sina_ref, cosc_ref, sinc_ref = rest[:4]
        qa_ref, ka_ref, va_ref, fb_ref, qc_ref, kc_ref, vc_ref = rest[4:]
    else:
        qa_ref, ka_ref, va_ref, fb_ref, qc_ref, kc_ref, vc_ref, ska_ref, sva_ref, sckv_ref, skr_ref = rest
    x = x_ref[...]
    y = _rms(x, g_ref[...], D_MODEL)
    sh1 = mod_ref[:, 0:D_MODEL]
    sc1 = mod_ref[:, D_MODEL:2 * D_MODEL]
    h = (y * (1.0 + sc1) + sh1).astype(BF16)

    qa = _dot(h, win_ref[:, _QA0:_KA0])
    ka = _dot(h, win_ref[:, _KA0:_VA0])
    va = _dot(h, win_ref[:, _VA0:_FB0])
    fb = _dot(h, win_ref[:, _FB0:_CQ0])
    cq = _dot(h, win_ref[:, _CQ0:_CKV0])
    ckv = _dot(h, win_ref[:, _CKV0:_KR0])
    kr = _dot(h, win_ref[:, _KR0:_IN_PAD])

    if not latent:
        ska_ref[...] = ka
        sva_ref[...] = va
        skr_ref[...] = kr[:, KR_LANE0:KR_LANE0 + C_ROPE]
    else:
        qa = _rope(qa, cosa_ref[...], sina_ref[...], HEAD_DIM // 4)
        ka = _rope(ka, cosa_ref[...], sina_ref[...], HEAD_DIM // 4)
        kr = _rope(kr, cosc_ref[...], sinc_ref[...], C_ROPE // 4)
    qa_ref[...] = (qa * (HEAD_DIM ** -0.5)).astype(BF16)
    ka_ref[...] = ka.astype(BF16)
    va_ref[...] = va.astype(BF16)
    fb_ref[...] = fb.astype(BF16)

    cqn = _rms(cq, gcq_ref[...], C_Q_LORA).astype(BF16)
    qc = _dot(cqn, wuq_ref[...])
    if latent:
        qc = _rope(qc, cosc_ref[...], sinc_ref[...], C_ROPE // 4)
    qc_ref[...] = (qc * ((C_NOPE + C_ROPE) ** -0.5)).astype(BF16)

    ckvn = _rms(ckv, gckv_ref[...], C_KV_LORA)
    if not latent:
        sckv_ref[...] = ckvn
    kv = _dot(ckvn.astype(BF16), wukv_ref[...])
    kw = C_HEADS * C_HEAD_PAD
    kc_ref[...] = (kv[:, :kw] + jnp.concatenate([kr] * C_HEADS, axis=1)).astype(BF16)
    vc_ref[...] = kv[:, kw:].astype(BF16)


def _projection(latent, x, mod_l, g_mix, win_p, gcq_p, wuq_p, gckv, wukv_p, tables):
    n = x.shape[0]
    T = T_PROJ
    nt = n // T
    tpb = (DEC_SEQ // T) if latent else nt
    base = 1 if latent else 0
    tok = lambda w: pl.BlockSpec((T, w), lambda i: (i, 0))
    full = lambda a: pl.BlockSpec(a.shape, lambda i: (0,) * a.ndim)
    in_specs = [
        tok(D_MODEL),
        pl.BlockSpec((None, 1, 6 * D_MODEL), lambda i: (base + i // tpb, 0, 0)),
        full(g_mix), full(win_p), full(gcq_p), full(wuq_p), full(gckv), full(wukv_p),
    ]
    args = [x, mod_l, g_mix, win_p, gcq_p, wuq_p, gckv, wukv_p]
    if latent:
        in_specs += [pl.BlockSpec((T, LANES), lambda i: (i % tpb, 0))] * 4
        args += list(tables)
    widths = [A_WIDTH, A_KV_WIDTH, A_KV_WIDTH, B_WIDTH, C_HEADS * C_HEAD_PAD, C_HEADS * C_HEAD_PAD, C_WIDTH]
    out_specs = [tok(w) for w in widths]
    out_shape = [jax.ShapeDtypeStruct((n, w), BF16) for w in widths]
    if not latent:
        sw = [A_KV_WIDTH, A_KV_WIDTH, C_KV_LORA, C_ROPE]
        out_specs += [tok(w) for w in sw]
        out_shape += [jax.ShapeDtypeStruct((n, w), F32) for w in sw]
    return pl.pallas_call(
        functools.partial(_proj_kernel, latent),
        grid=(nt,),
        in_specs=in_specs,
        out_specs=out_specs,
        out_shape=out_shape,
        compiler_params=_cparams(("arbitrary",)),
        name="proj_lat" if latent else "proj_ctx",
    )(*args)


def _kvcache_kernel(ckv_ref, kr_ref, wukv_ref, kc_ref, vc_ref):
    kv = _dot(ckv_ref[...].astype(BF16), wukv_ref[...])
    kw = C_HEADS * C_HEAD_PAD
    kc_ref[...] = (kv[:, :kw] + jnp.concatenate([kr_ref[...]] * C_HEADS, axis=1)).astype(BF16)
    vc_ref[...] = kv[:, kw:].astype(BF16)


def _kvcache(ckv, kr_pad, wukv_p):
    B, P, _ = ckv.shape
    kw = C_HEADS * C_HEAD_PAD
    return pl.pallas_call(
        _kvcache_kernel,
        grid=(B,),
        in_specs=[
            pl.BlockSpec((None, P, C_KV_LORA), lambda b: (b, 0, 0)),
            pl.BlockSpec((None, P, LANES), lambda b: (b, 0, 0)),
            pl.BlockSpec(wukv_p.shape, lambda b: (0, 0)),
        ],
        out_specs=[pl.BlockSpec((None, P, kw), lambda b: (b, 0, 0)),
                   pl.BlockSpec((None, P, C_WIDTH), lambda b: (b, 0, 0))],
        out_shape=[jax.ShapeDtypeStruct((B, P, kw), BF16), jax.ShapeDtypeStruct((B, P, C_WIDTH), BF16)],
        compiler_params=_cparams(("arbitrary",)),
        name="kvcache",
    )(ckv, kr_pad, wukv_p)


def _attn_a_ctx_kernel(sink_ref, q_ref, k_ref, v_ref, o_ref):
    outs = []
    for h in range(A_HEADS):
        g = h // A_GROUP
        q = q_ref[:, h * HEAD_DIM:(h + 1) * HEAD_DIM]
        k = k_ref[:, g * HEAD_DIM:(g + 1) * HEAD_DIM]
        v = v_ref[:, g * HEAD_DIM:(g + 1) * HEAD_DIM]
        s = _dot_nt(q, k)
        sk = sink_ref[0, h]
        m = jnp.maximum(jnp.max(s, axis=-1, keepdims=True), sk)
        p = jnp.exp(s - m)
        l = jnp.sum(p, axis=-1, keepdims=True) + jnp.exp(sk - m)
        outs.append(_dot(p.astype(BF16), v) / l)
    o_ref[...] = jnp.concatenate(outs, axis=1).astype(BF16)


def _attn_a_ctx(sink, q, k, v):
    B, n, _ = q.shape
    blk = lambda w: pl.BlockSpec((None, n, w), lambda b: (b, 0, 0))
    return pl.pallas_call(
        _attn_a_ctx_kernel,
        grid=(B,),
        in_specs=[pl.BlockSpec(memory_space=pltpu.SMEM), blk(A_WIDTH), blk(A_KV_WIDTH), blk(A_KV_WIDTH)],
        out_specs=blk(A_WIDTH),
        out_shape=jax.ShapeDtypeStruct((B, n, A_WIDTH), BF16),
        compiler_params=_cparams(("arbitrary",)),
        name="attn_a_ctx",
    )(sink, q, k, v)


def _attn_a_lat_kernel(sink_ref, q_ref, k_ref, v_ref, kc_ref, vc_ref, o_ref):
    i = pl.program_id(1)
    n = k_ref.shape[0]
    band = 3 * BLOCK
    start = pl.multiple_of(jnp.clip((i - 1) * BLOCK, 0, n - band), BLOCK)
    kb = k_ref[pl.ds(start, band), :]
    vb = v_ref[pl.ds(start, band), :]
    kcx = kc_ref[...]
    vcx = vc_ref[...]
    qpos = i * BLOCK + lax.broadcasted_iota(jnp.int32, (BLOCK, band), 0)
    kpos = start + lax.broadcasted_iota(jnp.int32, (BLOCK, band), 1)
    mask = jnp.abs(qpos - kpos) <= WINDOW
    outs = []
    for h in range(A_HEADS):
        g = h // A_GROUP
        gs = slice(g * HEAD_DIM, (g + 1) * HEAD_DIM)
        q = q_ref[:, h * HEAD_DIM:(h + 1) * HEAD_DIM]
        s_b = jnp.where(mask, _dot_nt(q, kb[:, gs]), NEG)
        s_c = _dot_nt(q, kcx[:, gs])
        sk = sink_ref[0, h]
        m = jnp.maximum(jnp.maximum(jnp.max(s_b, axis=-1, keepdims=True),
                                    jnp.max(s_c, axis=-1, keepdims=True)), sk)
        p_b = jnp.exp(s_b - m)
        p_c = jnp.exp(s_c - m)
        l = (jnp.sum(p_b, axis=-1, keepdims=True) + jnp.sum(p_c, axis=-1, keepdims=True)
             + jnp.exp(sk - m))
        o = _dot(p_b.astype(BF16), vb[:, gs]) + _dot(p_c.astype(BF16), vcx[:, gs])
        outs.append(o / l)
    o_ref[...] = jnp.concatenate(outs, axis=1).astype(BF16)


def _attn_a_lat(sink, q, k, v, k_ctx, v_ctx):
    B, n, _ = q.shape
    P = k_ctx.shape[1]
    seq = lambda w, m: pl.BlockSpec((None, m, w), lambda b, i: (b, 0, 0))
    return pl.pallas_call(
        _attn_a_lat_kernel,
        grid=(B, n // BLOCK),
        in_specs=[pl.BlockSpec(memory_space=pltpu.SMEM),
                  pl.BlockSpec((None, BLOCK, A_WIDTH), lambda b, i: (b, i, 0)),
                  seq(A_KV_WIDTH, n), seq(A_KV_WIDTH, n), seq(A_KV_WIDTH, P), seq(A_KV_WIDTH, P)],
        out_specs=pl.BlockSpec((None, BLOCK, A_WIDTH), lambda b, i: (b, i, 0)),
        out_shape=jax.ShapeDtypeStruct((B, n, A_WIDTH), BF16),
        compiler_params=_cparams(("arbitrary", "arbitrary")),
        name="attn_a_lat",
    )(sink, q, k, v, k_ctx, v_ctx)


def _mla_kernel(n_chunks, tk, q_ref, k_ref, v_ref, o_ref, m_scr, l_scr, acc_scr):
    tq = q_ref.shape[0]
    m_scr[...] = jnp.full(m_scr.shape, -jnp.inf, F32)
    l_scr[...] = jnp.zeros(l_scr.shape, F32)
    acc_scr[...] = jnp.zeros(acc_scr.shape, F32)
    low = lax.broadcasted_iota(jnp.int32, (tq, LANES), 1) < C_V

    def body(c, carry):
        ks = pl.multiple_of(c * tk, tk)
        for j in range(C_HEADS // 2):
            vt = v_ref[pl.ds(ks, tk), j * LANES:(j + 1) * LANES]
            alphas, pvs = [], []
            for h in (2 * j, 2 * j + 1):
                hs = slice(h * C_HEAD_PAD, (h + 1) * C_HEAD_PAD)
                s = _dot_nt(q_ref[:, hs], k_ref[pl.ds(ks, tk), hs])
                m_old = m_scr[h]
                m_new = jnp.maximum(m_old, jnp.max(s, axis=-1, keepdims=True))
                alpha = jnp.exp(m_old - m_new)
                p = jnp.exp(s - m_new)
                l_scr[h] = alpha * l_scr[h] + jnp.sum(p, axis=-1, keepdims=True)
                m_scr[h] = m_new
                alphas.append(alpha)
                pvs.append(_dot(p.astype(BF16), vt))
            acc_scr[j] = (acc_scr[j] * jnp.where(low, alphas[0], alphas[1])
                          + jnp.where(low, pvs[0], pvs[1]))
        return carry

    lax.fori_loop(0, n_chunks, body, 0)
    for j in range(C_HEADS // 2):
        l = jnp.where(low, l_scr[2 * j], l_scr[2 * j + 1])
        o_ref[:, j * LANES:(j + 1) * LANES] = (acc_scr[j] / l).astype(BF16)


def _mla(q, k, v, tq, tk):
    B, n, qw = q.shape
    nk = k.shape[1]
    return pl.pallas_call(
        functools.partial(_mla_kernel, nk // tk, tk),
        grid=(B, n // tq),
        in_specs=[pl.BlockSpec((None, tq, qw), lambda b, i: (b, i, 0)),
                  pl.BlockSpec((None, nk, qw), lambda b, i: (b, 0, 0)),
                  pl.BlockSpec((None, nk, C_WIDTH), lambda b, i: (b, 0, 0))],
        out_specs=pl.BlockSpec((None, tq, C_WIDTH), lambda b, i: (b, i, 0)),
        out_shape=jax.ShapeDtypeStruct((B, n, C_WIDTH), BF16),
        scratch_shapes=[pltpu.VMEM((C_HEADS, tq, 1), F32), pltpu.VMEM((C_HEADS, tq, 1), F32),
                        pltpu.VMEM((C_HEADS // 2, tq, LANES), F32)],
        compiler_params=_cparams(("arbitrary", "arbitrary")),
        name="mla",
    )(q, k, v)


def _dft_tables(n):
    j = jnp.arange(n, dtype=jnp.int32)
    ang = ((j[:, None] * j[None, :]) % n).astype(F32) * (2.0 * np.pi / n)
    return jnp.cos(ang), jnp.sin(ang)


def _channel_dft():
    c, s = _dft_tables(B_GROUP_DIM)
    eye = jnp.eye(B_GROUPS, dtype=F32)
    return jnp.concatenate([jnp.kron(eye, c), -jnp.kron(eye, s)], axis=1).astype(BF16)


def _fourier_ctx_kernel(scale, x_ref, fc_ref, fn_ref, o_ref):
    u = _dot(x_ref[...], fc_ref[...])
    ucat = jnp.concatenate([u[:, :B_WIDTH], u[:, B_WIDTH:]], axis=0).astype(BF16)
    o_ref[...] = (_dot(fn_ref[...], ucat) * scale).astype(BF16)


def _fourier_ctx(fb, fc):
    B, n, _ = fb.shape
    c, s = _dft_tables(n)
    fn = jnp.concatenate([c, s], axis=1).astype(BF16)
    scale = float((n * B_GROUP_DIM) ** -0.5)
    return pl.pallas_call(
        functools.partial(_fourier_ctx_kernel, scale),
        grid=(B,),
        in_specs=[pl.BlockSpec((None, n, B_WIDTH), lambda b: (b, 0, 0)),
                  pl.BlockSpec(fc.shape, lambda b: (0, 0)),
                  pl.BlockSpec(fn.shape, lambda b: (0, 0))],
        out_specs=pl.BlockSpec((None, n, B_WIDTH), lambda b: (b, 0, 0)),
        out_shape=jax.ShapeDtypeStruct((B, n, B_WIDTH), BF16),
        compiler_params=_cparams(("arbitrary",)),
        name="fourier_ctx",
    )(fb, fc, fn)


def _fourier_lat1_kernel(x_ref, fc_ref, g_ref, zr_ref, zi_ref):
    for cc in range(FOURIER_COLS):
        xc = x_ref[:, cc * B_WIDTH:(cc + 1) * B_WIDTH]
        u = _dot(xc, fc_ref[...]).astype(BF16)
        p = _dot(g_ref[cc], u)
        R = p.shape[0] // 2
        zr_ref[cc] = (p[:R, :B_WIDTH] - p[R:, B_WIDTH:]).astype(BF16)
        zi_ref[cc] = (p[:R, B_WIDTH:] + p[R:, :B_WIDTH]).astype(BF16)


def _fourier_lat2_kernel(scale, zr_ref, zi_ref, f_ref, o_ref):
    z = jnp.concatenate([zr_ref[...], zi_ref[...]], axis=0)
    o_ref[...] = (_dot(f_ref[...], z) * scale).astype(BF16)


def _fourier_lat(fb, fc):
    B, n, _ = fb.shape
    R = n // GRID_W
    W = GRID_W * B_WIDTH
    kr = jnp.arange(R, dtype=jnp.int32)
    pos = (GRID_W * jnp.arange(R, dtype=jnp.int32)[None, None, :]
           + jnp.arange(GRID_W, dtype=jnp.int32)[:, None, None])
    ang = ((kr[None, :, None] * pos) % n).astype(F32) * (2.0 * np.pi / n)
    g = jnp.concatenate([jnp.cos(ang), -jnp.sin(ang)], axis=1).astype(BF16)
    c64, s64 = _dft_tables(GRID_W)
    f2 = jnp.concatenate([c64, s64], axis=1).astype(BF16)
    nc = GRID_W // FOURIER_COLS
    wc = FOURIER_COLS * B_WIDTH
    zshape = jax.ShapeDtypeStruct((B, GRID_W, R, B_WIDTH), BF16)
    zr, zi = pl.pallas_call(
        _fourier_lat1_kernel,
        grid=(B, nc),
        in_specs=[pl.BlockSpec((None, R, wc), lambda b, j: (b, 0, j)),
                  pl.BlockSpec(fc.shape, lambda b, j: (0, 0)),
                  pl.BlockSpec((FOURIER_COLS, 2 * R, R), lambda b, j: (j, 0, 0))],
        out_specs=[pl.BlockSpec((None, FOURIER_COLS, R, B_WIDTH), lambda b, j: (b, j, 0, 0))] * 2,
        out_shape=[zshape, zshape],
        compiler_params=_cparams(("arbitrary", "arbitrary")),
        name="fourier_lat1",
    )(fb.reshape(B, R, W), fc, g)
    scale = float((n * B_GROUP_DIM) ** -0.5)
    w2 = R * B_WIDTH
    out = pl.pallas_call(
        functools.partial(_fourier_lat2_kernel, scale),
        grid=(B, nc),
        in_specs=[pl.BlockSpec((None, GRID_W, w2 // nc), lambda b, j: (b, 0, j))] * 2
        + [pl.BlockSpec(f2.shape, lambda b, j: (0, 0))],
        out_specs=pl.BlockSpec((None, GRID_W, w2 // nc), lambda b, j: (b, 0, j)),
        out_shape=jax.ShapeDtypeStruct((B, GRID_W, w2), BF16),
        compiler_params=_cparams(("arbitrary", "arbitrary")),
        name="fourier_lat2",
    )(zr.reshape(B, GRID_W, w2), zi.reshape(B, GRID_W, w2), f2)
    return out.reshape(B, n, B_WIDTH)


def _outproj_kernel(x_ref, a_ref, b_ref, c_ref, mod_ref, g_ref, wout_ref, x1_ref, h2_ref):
    mix = (_dot(a_ref[...], wout_ref[0:A_WIDTH, :])
           + _dot(b_ref[...], wout_ref[A_WIDTH:A_WIDTH + B_WIDTH, :])
           + _dot(c_ref[...], wout_ref[A_WIDTH + B_WIDTH:, :]))
    gt1 = mod_ref[:, 2 * D_MODEL:3 * D_MODEL]
    sh2 = mod_ref[:, 3 * D_MODEL:4 * D_MODEL]
    sc2 = mod_ref[:, 4 * D_MODEL:5 * D_MODEL]
    x1 = x_ref[...] + gt1 * mix
    x1_ref[...] = x1
    h2_ref[...] = (_rms(x1, g_ref[...], D_MODEL) * (1.0 + sc2) + sh2).astype(BF16)


def _outproj(latent, x, oa, ob, oc, mod_l, g_ffn, wout):
    n = x.shape[0]
    T = T_OUT
    nt = n // T
    tpb = (DEC_SEQ // T) if latent else nt
    base = 1 if latent else 0
    tok = lambda w: pl.BlockSpec((T, w), lambda i: (i, 0))
    return pl.pallas_call(
        _outproj_kernel,
        grid=(nt,),
        in_specs=[tok(D_MODEL), tok(A_WIDTH), tok(B_WIDTH), tok(C_WIDTH),
                  pl.BlockSpec((None, 1, 6 * D_MODEL), lambda i: (base + i // tpb, 0, 0)),
                  pl.BlockSpec(g_ffn.shape, lambda i: (0, 0)),
                  pl.BlockSpec(wout.shape, lambda i: (0, 0))],
        out_specs=[tok(D_MODEL), tok(D_MODEL)],
        out_shape=[jax.ShapeDtypeStruct((n, D_MODEL), F32), jax.ShapeDtypeStruct((n, D_MODEL), BF16)],
        compiler_params=_cparams(("arbitrary",)),
        name="outproj_lat" if latent else "outproj_ctx",
    )(x, oa, ob, oc, mod_l, g_ffn, wout)


def _ffn_kernel(T, S, final, x1_ref, hm_ref, hp_ref, hn_ref, mod_ref, wa_ref, wg_ref, cwa_ref, cwg_ref,
                cba_ref, cbg_ref, wd_ref, gfin_ref, o_ref, hs_scr, acc_scr):
    i = pl.program_id(0)
    j = pl.program_id(1)
    H = BF16_ROWS

    @pl.when(j == 0)
    def _():
        hs_scr[0:H] = hp_ref[...]
        hs_scr[H:T + H] = hm_ref[...]
        hs_scr[T + H:T + 2 * H] = hn_ref[...]
        acc_scr[...] = jnp.zeros(acc_scr.shape, F32)

    hs = hs_scr[...]
    pos = (i * T + lax.broadcasted_iota(jnp.int32, (T, 1), 0)) % S
    has_prev = pos != 0
    has_next = pos != S - 1

    def branch(w_ref, cw_ref, cb_ref):
        u = _dot(hs, w_ref[...])
        up = jnp.where(has_prev, u[H - 1:T + H - 1], 0.0)
        un = jnp.where(has_next, u[H + 1:T + H + 1], 0.0)
        return up * cw_ref[0:1, :] + u[H:T + H] * cw_ref[1:2, :] + un * cw_ref[2:3, :] + cb_ref[...]

    a = branch(wa_ref, cwa_ref, cba_ref)
    g = branch(wg_ref, cwg_ref, cbg_ref)
    act = (g * jax.nn.sigmoid(g) * a).astype(BF16)
    acc_scr[...] += _dot(act, wd_ref[...])

    @pl.when(j == pl.num_programs(1) - 1)
    def _():
        gt2 = mod_ref[:, 5 * D_MODEL:6 * D_MODEL]
        x2 = x1_ref[...] + gt2 * acc_scr[...]
        if final:
            x2 = _rms(x2, gfin_ref[...], D_MODEL)
        o_ref[...] = x2


def _ffn(latent, final, x1, h2, mod_l, wug, conv_w, conv_b, wdown, g_final):
    n = x1.shape[0]
    T = T_FFN
    S = DEC_SEQ if latent else SEQ
    nt = n // T
    nj = D_FF // F_CHUNK
    tpb = (DEC_SEQ // T) if latent else nt
    base = 1 if latent else 0
    hb = T // BF16_ROWS
    nhb = n // BF16_ROWS
    conv_b2 = conv_b.reshape(1, 2 * D_FF)
    return pl.pallas_call(
        functools.partial(_ffn_kernel, T, S, final),
        grid=(nt, nj),
        in_specs=[
            pl.BlockSpec((T, D_MODEL), lambda i, j: (i, 0)),
            pl.BlockSpec((T, D_MODEL), lambda i, j: (i, 0)),
            pl.BlockSpec((BF16_ROWS, D_MODEL), lambda i, j: (jnp.maximum(i * hb - 1, 0), 0)),
            pl.BlockSpec((BF16_ROWS, D_MODEL), lambda i, j: (jnp.minimum((i + 1) * hb, nhb - 1), 0)),
            pl.BlockSpec((None, 1, 6 * D_MODEL), lambda i, j: (base + i // tpb, 0, 0)),
            pl.BlockSpec((D_MODEL, F_CHUNK), lambda i, j: (0, j)),
            pl.BlockSpec((D_MODEL, F_CHUNK), lambda i, j: (0, nj + j)),
            pl.BlockSpec((3, F_CHUNK), lambda i, j: (0, j)),
            pl.BlockSpec((3, F_CHUNK), lambda i, j: (0, nj + j)),
            pl.BlockSpec((1, F_CHUNK), lambda i, j: (0, j)),
            pl.BlockSpec((1, F_CHUNK), lambda i, j: (0, nj + j)),
            pl.BlockSpec((F_CHUNK, D_MODEL), lambda i, j: (j, 0)),
            pl.BlockSpec((1, D_MODEL), lambda i, j: (0, 0)),
        ],
        out_specs=pl.BlockSpec((T, D_MODEL), lambda i, j: (i, 0)),
        out_shape=jax.ShapeDtypeStruct((n, D_MODEL), F32),
        scratch_shapes=[pltpu.VMEM((T + 2 * BF16_ROWS, D_MODEL), BF16), pltpu.VMEM((T, D_MODEL), F32)],
        compiler_params=_cparams(("arbitrary", "arbitrary")),
        name="ffn_lat" if latent else "ffn_ctx",
    )(x1, h2, h2, h2, mod_l, wug, wug, conv_w, conv_w, conv_b2, conv_b2, wdown, g_final)


def _rope_tables(n_tok, dim, lane0, width):
    rows = n_tok // GRID_W
    r = jnp.repeat(jnp.arange(rows), GRID_W).astype(F32)
    col = jnp.tile(jnp.arange(GRID_W), rows).astype(F32)
    quarter = dim // 4
    inv = ROPE_BASE ** (-jnp.arange(quarter, dtype=F32) / quarter)
    ang_r = r[:, None] * inv
    ang_c = col[:, None] * inv
    cos = jnp.concatenate([jnp.cos(ang_r)] * 2 + [jnp.cos(ang_c)] * 2, axis=1)
    sin = jnp.concatenate([-jnp.sin(ang_r), jnp.sin(ang_r), -jnp.sin(ang_c), jnp.sin(ang_c)], axis=1)
    reps = width // dim
    cos = jnp.concatenate([cos] * reps, axis=1)
    sin = jnp.concatenate([sin] * reps, axis=1)
    pad = ((0, 0), (lane0, LANES - lane0 - width))
    return jnp.pad(cos, pad, constant_values=1.0), jnp.pad(sin, pad)


def _layer_weights(w_in, g_cq, w_uq, w_ukv):
    z = lambda r, c: jnp.zeros((r, c), F32)
    o = np.cumsum([0, A_WIDTH, A_KV_WIDTH, A_KV_WIDTH, B_WIDTH, C_Q_LORA, C_KV_LORA, C_ROPE])
    win_p = jnp.concatenate([
        w_in[:, o[0]:o[4]],
        w_in[:, o[4]:o[5]], z(D_MODEL, CQ_PAD - C_Q_LORA),
        w_in[:, o[5]:o[6]],
        z(D_MODEL, KR_LANE0), w_in[:, o[6]:o[7]], z(D_MODEL, LANES - KR_LANE0 - C_ROPE),
    ], axis=1).astype(BF16)
    gcq_p = jnp.pad(g_cq, (0, CQ_PAD - C_Q_LORA)).reshape(1, CQ_PAD)
    hq = C_NOPE + C_ROPE
    wuq_h = w_uq.reshape(C_Q_LORA, C_HEADS, hq)
    wuq_p = jnp.pad(wuq_h, ((0, CQ_PAD - C_Q_LORA), (0, 0), (0, C_HEAD_PAD - hq)))
    wuq_p = wuq_p.reshape(CQ_PAD, C_HEADS * C_HEAD_PAD).astype(BF16)
    wukv_h = w_ukv.reshape(C_KV_LORA, C_HEADS, C_NOPE + C_V)
    wk = jnp.pad(wukv_h[:, :, :C_NOPE], ((0, 0), (0, 0), (0, C_HEAD_PAD - C_NOPE)))
    wukv_p = jnp.concatenate([wk.reshape(C_KV_LORA, C_HEADS * C_HEAD_PAD),
                              wukv_h[:, :, C_NOPE:].reshape(C_KV_LORA, C_WIDTH)], axis=1).astype(BF16)
    return win_p, gcq_p, wuq_p, wukv_p


def kernel(x_prompt, x_sample, cache_win_k, cache_win_v, cache_mla_ckv, cache_mla_krope, c, c_ctx,
           w_ada, b_ada, g_mix, w_in, sink, g_cq, w_uq, g_ckv, w_ukv, w_out, g_ffn, w_ug, conv_w,
           conv_b, w_down, g_final):
    n_ctx = BATCH * SEQ
    n_lat = DEC_BATCH * DEC_SEQ
    xp = x_prompt.reshape(n_ctx, D_MODEL)
    xs = x_sample.reshape(n_lat, D_MODEL)

    cvecs = jnp.concatenate([c_ctx[None, :], c, jnp.zeros((8 - 1 - DEC_BATCH, D_MODEL), F32)], axis=0)
    mod = _modulation(cvecs, w_ada, b_ada)

    cos_a, sin_a = _rope_tables(DEC_SEQ, HEAD_DIM, 0, LANES)
    cos_c, sin_c = _rope_tables(DEC_SEQ, C_ROPE, KR_LANE0, C_ROPE)
    tables = (cos_a, sin_a, cos_c, sin_c)
    fc = _channel_dft()
    g_final2 = g_final.reshape(1, D_MODEL)

    new_k, new_v, new_ckv, new_kr = [], [], [], []
    for l in range(DEPTH):
        win_p, gcq_p, wuq_p, wukv_p = _layer_weights(w_in[l], g_cq[l], w_uq[l], w_ukv[l])
        mod_l = mod[l].reshape(8, 1, 6 * D_MODEL)
        g_mix_l = g_mix[l].reshape(1, D_MODEL)
        g_ckv_l = g_ckv[l].reshape(1, C_KV_LORA)
        g_ffn_l = g_ffn[l].reshape(1, D_MODEL)
        sink_l = sink[l].reshape(1, A_HEADS)
        wout_l = w_out[l].astype(BF16)
        wug_l = w_ug[l].astype(BF16)
        wdown_l = w_down[l].astype(BF16)
        final = l == DEPTH - 1

        (qa, ka, va, fb, qc, kc, vc, ska, sva, sckv, skr) = _projection(
            False, xp, mod_l, g_mix_l, win_p, gcq_p, wuq_p, g_ckv_l, wukv_p, None)
        r3 = lambda a: a.reshape(BATCH, SEQ, a.shape[-1])
        oa = _attn_a_ctx(sink_l, r3(qa), r3(ka), r3(va))
        ob = _fourier_ctx(r3(fb), fc)
        oc = _mla(r3(qc), r3(kc), r3(vc), SEQ, SEQ)
        f2 = lambda a: a.reshape(n_ctx, a.shape[-1])
        x1, h2 = _outproj(False, xp, f2(oa), f2(ob), f2(oc), mod_l, g_ffn_l, wout_l)
        xp = _ffn(False, final, x1, h2, mod_l, wug_l, conv_w[l], conv_b[l], wdown_l, g_final2)
        new_k.append(ska.reshape(BATCH, SEQ, A_KV_HEADS, HEAD_DIM))
        new_v.append(sva.reshape(BATCH, SEQ, A_KV_HEADS, HEAD_DIM))
        new_ckv.append(sckv.reshape(BATCH, SEQ, C_KV_LORA))
        new_kr.append(skr.reshape(BATCH, SEQ, C_ROPE))

        (qa, ka, va, fb, qc, kc, vc) = _projection(
            True, xs, mod_l, g_mix_l, win_p, gcq_p, wuq_p, g_ckv_l, wukv_p, tables)
        r3 = lambda a: a.reshape(DEC_BATCH, DEC_SEQ, a.shape[-1])
        kwin = cache_win_k[:, l].reshape(DEC_BATCH, PAST_LEN, A_KV_WIDTH).astype(BF16)
        vwin = cache_win_v[:, l].reshape(DEC_BATCH, PAST_LEN, A_KV_WIDTH).astype(BF16)
        oa = _attn_a_lat(sink_l, r3(qa), r3(ka), r3(va), kwin, vwin)
        ob = _fourier_lat(r3(fb), fc)
        kr_pad = jnp.pad(cache_mla_krope[:, l], ((0, 0), (0, 0), (KR_LANE0, LANES - KR_LANE0 - C_ROPE)))
        kc_ctx, vc_ctx = _kvcache(cache_mla_ckv[:, l], kr_pad, wukv_p)
        k_all = jnp.concatenate([r3(kc), kc_ctx], axis=1)
        v_all = jnp.concatenate([r3(vc), vc_ctx], axis=1)
        oc = _mla(r3(qc), k_all, v_all, TQ_MLA, TK_MLA)
        f2 = lambda a: a.reshape(n_lat, a.shape[-1])
        x1, h2 = _outproj(True, xs, f2(oa), f2(ob), f2(oc), mod_l, g_ffn_l, wout_l)
        xs = _ffn(True, final, x1, h2, mod_l, wug_l, conv_w[l], conv_b[l], wdown_l, g_final2)

    y_prompt = xp.reshape(BATCH, SEQ, D_MODEL)
    y_sample = xs.reshape(DEC_BATCH, DEC_SEQ, D_MODEL)
    return (y_prompt, y_sample, jnp.stack(new_k, axis=1), jnp.stack(new_v, axis=1),
            jnp.stack(new_ckv, axis=1), jnp.stack(new_kr, axis=1))
```

```python
import functools

import numpy as np
import jax
import jax.numpy as jnp
from jax import lax
from jax.experimental import pallas as pl
from jax.experimental.pallas import tpu as pltpu

F32 = jnp.float32
BF16 = jnp.bfloat16

D_MODEL = 1024
BATCH = 16
SEQ = 256
DEPTH = 2
DEC_BATCH = 4
DEC_SEQ = 4096
PAST_LEN = 256
GRID_W = 64
HEAD_DIM = 64
A_HEADS = 8
A_KV_HEADS = 2
A_GROUP = 4
A_WIDTH = 512
A_KV_WIDTH = 128
WINDOW = 128
BLOCK = 128
B_WIDTH = 256
B_GROUP_DIM = 64
B_GROUPS = 4
C_HEADS = 4
C_NOPE = 64
C_ROPE = 32
C_V = 64
C_Q_LORA = 192
C_KV_LORA = 128
C_WIDTH = 256
D_FF = 2816
ROPE_BASE = 10000.0
EPS = 1e-6
NEG = -1e30

LANES = 128
BF16_ROWS = 16
C_HEAD_PAD = 128
CW = C_HEADS * C_HEAD_PAD
CQ_PAD = 256
VMEM_LIMIT = 56 * 1024 * 1024

_QA0, _KA0, _VA0, _FB0, _CQ0, _CKV0, _KR0, _IN_PAD = 0, 512, 640, 768, 1024, 1280, 1408, 1536
KR_LANE0 = C_NOPE

T_PROJ = 512
T_OUT = 512
T_FFN = 1024
F_CHUNK = 256
TQ_MLA = 512
TK_MLA = 512
A_SUB = 4
A_HEAD_ORDER = [h for j in range(A_GROUP) for h in (j, A_GROUP + j)]
LOG2_E = float(np.log2(np.e))
FOURIER_COLS = 8


def _cparams(sem):
    return pltpu.CompilerParams(dimension_semantics=sem, vmem_limit_bytes=VMEM_LIMIT)


def _dot(a, b):
    return jnp.dot(a, b, preferred_element_type=F32)


def _dot_nt(a, b):
    return lax.dot_general(a, b, (((1,), (1,)), ((), ())), preferred_element_type=F32)


def _rms(x, g, n):
    ms = jnp.sum(x * x, axis=-1, keepdims=True) * (1.0 / n)
    return x * lax.rsqrt(ms + EPS) * g


def _mod_kernel(c_ref, w_ref, b_ref, o_ref):
    cv = c_ref[...]
    s = cv * jax.nn.sigmoid(cv)
    o_ref[...] = jnp.dot(s, w_ref[...], preferred_element_type=F32,
                         precision=lax.Precision.HIGHEST) + b_ref[...]


def _modulation(cvecs, w_ada, b_ada):
    nj = 6
    return pl.pallas_call(
        _mod_kernel,
        grid=(DEPTH, nj),
        in_specs=[
            pl.BlockSpec((8, D_MODEL), lambda l, j: (0, 0)),
            pl.BlockSpec((None, D_MODEL, D_MODEL), lambda l, j: (l, 0, j)),
            pl.BlockSpec((None, 1, D_MODEL), lambda l, j: (l, 0, j)),
        ],
        out_specs=pl.BlockSpec((None, 8, D_MODEL), lambda l, j: (l, 0, j)),
        out_shape=jax.ShapeDtypeStruct((DEPTH, 8, 6 * D_MODEL), F32),
        compiler_params=_cparams(("arbitrary", "arbitrary")),
        name="modulation",
    )(cvecs, w_ada, b_ada.reshape(DEPTH, 1, 6 * D_MODEL))


def _rope_block(x, cos, sin, half):
    lane = lax.broadcasted_iota(jnp.int32, x.shape, 1)
    first = (lane % (2 * half)) < half
    partner = jnp.where(first, pltpu.roll(x, LANES - half, 1), pltpu.roll(x, half, 1))
    return x * cos + partner * sin


def _rope(x, cos, sin, half):
    blocks = [_rope_block(x[:, j:j + LANES], cos, sin, half) for j in range(0, x.shape[1], LANES)]
    return blocks[0] if len(blocks) == 1 else jnp.concatenate(blocks, axis=1)


def _proj_kernel(latent, x_ref, mod_ref, g_ref, win_ref, gcq_ref, wuq_ref, gckv_ref, wukv_ref, *rest):
    if latent:
        cosa_ref, sina_ref, cosc_ref, sinc_ref = rest[:4]
        qa_ref, ka_ref, va_ref, fb_ref, qc_ref, kc_ref, vc_ref = rest[4:]
    else:
        qa_ref, ka_ref, va_ref, fb_ref, qc_ref, kc_ref, vc_ref, ska_ref, sva_ref, sckv_ref, skr_ref = rest
    x = x_ref[...]
    y = _rms(x, g_ref[...], D_MODEL)
    sh1 = mod_ref[:, 0:D_MODEL]
    sc1 = mod_ref[:, D_MODEL:2 * D_MODEL]
    h = (y * (1.0 + sc1) + sh1).astype(BF16)

    qa = _dot(h, win_ref[:, _QA0:_KA0])
    ka = _dot(h, win_ref[:, _KA0:_VA0])
    va = _dot(h, win_ref[:, _VA0:_FB0])
    fb = _dot(h, win_ref[:, _FB0:_CQ0])
    cq = _dot(h, win_ref[:, _CQ0:_CKV0])
    ckv = _dot(h, win_ref[:, _CKV0:_KR0])
    kr = _dot(h, win_ref[:, _KR0:_IN_PAD])

    if not latent:
        ska_ref[...] = ka
        sva_ref[...] = va
        skr_ref[...] = kr[:, KR_LANE0:KR_LANE0 + C_ROPE]
    else:
        qa = _rope(qa, cosa_ref[...], sina_ref[...], HEAD_DIM // 4)
        ka = _rope(ka, cosa_ref[...], sina_ref[...], HEAD_DIM // 4)
        kr = _rope(kr, cosc_ref[...], sinc_ref[...], C_ROPE // 4)
    qa_ref[...] = (qa * (LOG2_E * HEAD_DIM ** -0.5)).astype(BF16)
    ka_ref[...] = ka.astype(BF16)
    va_ref[...] = va.astype(BF16)
    fb_ref[...] = fb.astype(BF16)

    cqn = _rms(cq, gcq_ref[...], C_Q_LORA).astype(BF16)
    qc = _dot(cqn, wuq_ref[...])
    if latent:
        qc = _rope(qc, cosc_ref[...], sinc_ref[...], C_ROPE // 4)
    qc_ref[...] = (qc * (LOG2_E * (C_NOPE + C_ROPE) ** -0.5)).astype(BF16)

    ckvn = _rms(ckv, gckv_ref[...], C_KV_LORA)
    if not latent:
        sckv_ref[...] = ckvn
    _store_kv(_dot(ckvn.astype(BF16), wukv_ref[...]), kr, kc_ref, vc_ref)


def _store_kv(kv, kr, kc_ref, vc_ref):
    kc_ref[...] = (kv[:, :CW] + jnp.concatenate([kr] * C_HEADS, axis=1)).astype(BF16)
    lane = lax.broadcasted_iota(jnp.int32, (1, CW), 1)
    upper = lane % C_HEAD_PAD >= C_V
    odd = (lane // C_HEAD_PAD) % 2 == 1
    vc_ref[...] = (kv[:, CW:] + jnp.where(upper != odd, 1.0, 0.0).astype(F32)).astype(BF16)


def _projection(latent, x, mod_l, g_mix, win_p, gcq_p, wuq_p, gckv, wukv_p, tables):
    n = x.shape[0]
    T = T_PROJ
    nt = n // T
    tpb = (DEC_SEQ // T) if latent else nt
    base = 1 if latent else 0
    tok = lambda w: pl.BlockSpec((T, w), lambda i: (i, 0))
    full = lambda a: pl.BlockSpec(a.shape, lambda i: (0,) * a.ndim)
    in_specs = [
        tok(D_MODEL),
        pl.BlockSpec((None, 1, 6 * D_MODEL), lambda i: (base + i // tpb, 0, 0)),
        full(g_mix), full(win_p), full(gcq_p), full(wuq_p), full(gckv), full(wukv_p),
    ]
    args = [x, mod_l, g_mix, win_p, gcq_p, wuq_p, gckv, wukv_p]
    if latent:
        in_specs += [pl.BlockSpec((T, LANES), lambda i: (i % tpb, 0))] * 4
        args += list(tables)
    widths = [A_WIDTH, A_KV_WIDTH, A_KV_WIDTH, B_WIDTH, CW, CW, CW]
    out_specs = [tok(w) for w in widths]
    out_shape = [jax.ShapeDtypeStruct((n, w), BF16) for w in widths]
    if not latent:
        sw = [A_KV_WIDTH, A_KV_WIDTH, C_KV_LORA, C_ROPE]
        out_specs += [tok(w) for w in sw]
        out_shape += [jax.ShapeDtypeStruct((n, w), F32) for w in sw]
    return pl.pallas_call(
        functools.partial(_proj_kernel, latent),
        grid=(nt,),
        in_specs=in_specs,
        out_specs=out_specs,
        out_shape=out_shape,
        compiler_params=_cparams(("arbitrary",)),
        name="proj_lat" if latent else "proj_ctx",
    )(*args)


def _kvcache_kernel(ckv_ref, kr_ref, wukv_ref, kc_ref, vc_ref):
    _store_kv(_dot(ckv_ref[...].astype(BF16), wukv_ref[...]), kr_ref[...], kc_ref, vc_ref)


def _kvcache(ckv, kr_pad, wukv_p):
    B, P, _ = ckv.shape
    return pl.pallas_call(
        _kvcache_kernel,
        grid=(B,),
        in_specs=[
            pl.BlockSpec((None, P, C_KV_LORA), lambda b: (b, 0, 0)),
            pl.BlockSpec((None, P, LANES), lambda b: (b, 0, 0)),
            pl.BlockSpec(wukv_p.shape, lambda b: (0, 0)),
        ],
        out_specs=[pl.BlockSpec((None, P, CW), lambda b: (b, 0, 0))] * 2,
        out_shape=[jax.ShapeDtypeStruct((B, P, CW), BF16)] * 2,
        compiler_params=_cparams(("arbitrary",)),
        name="kvcache",
    )(ckv, kr_pad, wukv_p)


def _gqa_group(g, q_blocks, keys, masks, values, sink_ref):
    tq = q_blocks[0].shape[0]
    lane = lax.broadcasted_iota(jnp.int32, (tq, LANES), 1)
    mine = (lane >= HEAD_DIM) if g else (lane < HEAD_DIM)
    qs = jnp.concatenate([jnp.where(mine, qb, jnp.zeros_like(qb)) for qb in q_blocks], axis=0)
    parts = []
    for k, mask in zip(keys, masks):
        s = _dot_nt(qs, k)
        parts.append(s if mask is None else jnp.where(mask, s, NEG))
    s = parts[0] if len(parts) == 1 else jnp.concatenate(parts, axis=1)
    rowblk = lax.broadcasted_iota(jnp.int32, (A_GROUP * tq, 1), 0) // tq
    sk = jnp.full((A_GROUP * tq, 1), sink_ref[0, g * A_GROUP] * LOG2_E, F32)
    for j in range(1, A_GROUP):
        sk = jnp.where(rowblk == j, sink_ref[0, g * A_GROUP + j] * LOG2_E, sk)
    m = jnp.maximum(jnp.max(s, axis=-1, keepdims=True), sk)
    p = jnp.exp2(s - m)
    l = jnp.sum(p, axis=-1, keepdims=True) + jnp.exp2(sk - m)
    v = values[0] if len(values) == 1 else jnp.concatenate(values, axis=0)
    return _dot(p.astype(BF16), v) / l


def _gqa_store(o_ref, rows, tq, o0, o1):
    low = lax.broadcasted_iota(jnp.int32, (tq, LANES), 1) < HEAD_DIM
    for j in range(A_GROUP):
        blk = jnp.where(low, o0[j * tq:(j + 1) * tq], o1[j * tq:(j + 1) * tq])
        o_ref[rows, j * LANES:(j + 1) * LANES] = blk.astype(BF16)


def _attn_a_ctx_kernel(sink_ref, q_ref, k_ref, v_ref, o_ref):
    n = q_ref.shape[0]
    q_blocks = [q_ref[:, j * LANES:(j + 1) * LANES] for j in range(A_GROUP)]
    outs = [_gqa_group(g, q_blocks, [k_ref[...]], [None], [v_ref[...]], sink_ref) for g in range(A_KV_HEADS)]
    _gqa_store(o_ref, slice(None), n, *outs)


def _attn_a_ctx(sink, q, k, v):
    B, n, _ = q.shape
    blk = lambda w: pl.BlockSpec((None, n, w), lambda b: (b, 0, 0))
    return pl.pallas_call(
        _attn_a_ctx_kernel,
        grid=(B,),
        in_specs=[pl.BlockSpec(memory_space=pltpu.SMEM), blk(A_WIDTH), blk(A_KV_WIDTH), blk(A_KV_WIDTH)],
        out_specs=blk(A_WIDTH),
        out_shape=jax.ShapeDtypeStruct((B, n, A_WIDTH), BF16),
        compiler_params=_cparams(("arbitrary",)),
        name="attn_a_ctx",
    )(sink, q, k, v)


def _attn_a_lat_kernel(sink_ref, q_ref, k_ref, v_ref, kc_ref, vc_ref, o_ref):
    n = k_ref.shape[0]
    band = 3 * BLOCK
    kcx = kc_ref[...]
    vcx = vc_ref[...]

    def body(sb, carry):
        blk = pl.program_id(1) * A_SUB + sb
        rows = pl.ds(pl.multiple_of(sb * BLOCK, BLOCK), BLOCK)
        start = pl.multiple_of(jnp.clip((blk - 1) * BLOCK, 0, n - band), BLOCK)
        kb = k_ref[pl.ds(start, band), :]
        vb = v_ref[pl.ds(start, band), :]
        shape = (A_GROUP * BLOCK, band)
        qpos = blk * BLOCK + lax.broadcasted_iota(jnp.int32, shape, 0) % BLOCK
        kpos = start + lax.broadcasted_iota(jnp.int32, shape, 1)
        mask = jnp.abs(qpos - kpos) <= WINDOW
        q_blocks = [q_ref[rows, j * LANES:(j + 1) * LANES] for j in range(A_GROUP)]
        outs = [_gqa_group(g, q_blocks, [kb, kcx], [mask, None], [vb, vcx], sink_ref)
                for g in range(A_KV_HEADS)]
        _gqa_store(o_ref, rows, BLOCK, *outs)
        return carry

    lax.fori_loop(0, A_SUB, body, 0)


def _attn_a_lat(sink, q, k, v, k_ctx, v_ctx):
    B, n, _ = q.shape
    P = k_ctx.shape[1]
    tq = A_SUB * BLOCK
    seq = lambda w, m: pl.BlockSpec((None, m, w), lambda b, i: (b, 0, 0))
    return pl.pallas_call(
        _attn_a_lat_kernel,
        grid=(B, n // tq),
        in_specs=[pl.BlockSpec(memory_space=pltpu.SMEM),
                  pl.BlockSpec((None, tq, A_WIDTH), lambda b, i: (b, i, 0)),
                  seq(A_KV_WIDTH, n), seq(A_KV_WIDTH, n), seq(A_KV_WIDTH, P), seq(A_KV_WIDTH, P)],
        out_specs=pl.BlockSpec((None, tq, A_WIDTH), lambda b, i: (b, i, 0)),
        out_shape=jax.ShapeDtypeStruct((B, n, A_WIDTH), BF16),
        compiler_params=_cparams(("arbitrary", "arbitrary")),
        name="attn_a_lat",
    )(sink, q, k, v, k_ctx, v_ctx)


def _mla_step(q_ref, k, v, m_scr, acc_scr):
    reps = k.shape[0] // LANES
    for h in range(C_HEADS):
        hs = slice(h * C_HEAD_PAD, (h + 1) * C_HEAD_PAD)
        s = _dot_nt(q_ref[:, hs], k[:, hs])
        m_old = m_scr[h]
        m_new = jnp.maximum(m_old, jnp.max(s, axis=-1, keepdims=True))
        alpha = jnp.exp2(m_old - m_new)
        p = jnp.exp2(s - jnp.concatenate([m_new] * reps, axis=1))
        m_scr[h] = m_new
        acc_scr[h] = acc_scr[h] * alpha + _dot(p.astype(BF16), v[:, hs])


def _mla_kernel(n_chunks, tk, has_ctx, q_ref, k_ref, v_ref, *rest):
    if has_ctx:
        kc_ref, vc_ref, o_ref, m_scr, acc_scr = rest
    else:
        o_ref, m_scr, acc_scr = rest
    tq = q_ref.shape[0]
    m_scr[...] = jnp.full(m_scr.shape, -jnp.inf, F32)
    acc_scr[...] = jnp.zeros(acc_scr.shape, F32)

    def body(c, carry):
        ks = pl.multiple_of(c * tk, tk)
        _mla_step(q_ref, k_ref[pl.ds(ks, tk), :], v_ref[pl.ds(ks, tk), :], m_scr, acc_scr)
        return carry

    if has_ctx:
        _mla_step(q_ref, kc_ref[...], vc_ref[...], m_scr, acc_scr)
    lax.fori_loop(0, n_chunks, body, 0)

    low = lax.broadcasted_iota(jnp.int32, (tq, LANES), 1) < C_V
    for j in range(C_HEADS // 2):
        even, odd = acc_scr[2 * j], acc_scr[2 * j + 1]
        sums = pltpu.roll(jnp.where(low, odd, even), C_V, 1)
        o_ref[:, j * LANES:(j + 1) * LANES] = (jnp.where(low, even, odd) / sums).astype(BF16)


def _mla(q, k, v, k_ctx, v_ctx, tq, tk):
    B, n, _ = q.shape
    nk = k.shape[1]
    has_ctx = k_ctx is not None
    seq = lambda m: pl.BlockSpec((None, m, CW), lambda b, i: (b, 0, 0))
    in_specs = [pl.BlockSpec((None, tq, CW), lambda b, i: (b, i, 0)), seq(nk), seq(nk)]
    args = [q, k, v]
    if has_ctx:
        in_specs += [seq(k_ctx.shape[1])] * 2
        args += [k_ctx, v_ctx]
    return pl.pallas_call(
        functools.partial(_mla_kernel, nk // tk, tk, has_ctx),
        grid=(B, n // tq),
        in_specs=in_specs,
        out_specs=pl.BlockSpec((None, tq, C_WIDTH), lambda b, i: (b, i, 0)),
        out_shape=jax.ShapeDtypeStruct((B, n, C_WIDTH), BF16),
        scratch_shapes=[pltpu.VMEM((C_HEADS, tq, LANES), F32), pltpu.VMEM((C_HEADS, tq, LANES), F32)],
        compiler_params=_cparams(("arbitrary", "arbitrary")),
        name="mla_lat" if has_ctx else "mla_ctx",
    )(*args)


def _dft_tables(n):
    j = jnp.arange(n, dtype=jnp.int32)
    ang = ((j[:, None] * j[None, :]) % n).astype(F32) * (2.0 * np.pi / n)
    return jnp.cos(ang), jnp.sin(ang)


def _channel_dft():
    c, s = _dft_tables(B_GROUP_DIM)
    eye = jnp.eye(B_GROUPS, dtype=F32)
    return jnp.concatenate([jnp.kron(eye, c), -jnp.kron(eye, s)], axis=1).astype(BF16)


def _fourier_ctx_kernel(scale, x_ref, fc_ref, fn_ref, o_ref):
    u = _dot(x_ref[...], fc_ref[...])
    ucat = jnp.concatenate([u[:, :B_WIDTH], u[:, B_WIDTH:]], axis=0).astype(BF16)
    o_ref[...] = (_dot(fn_ref[...], ucat) * scale).astype(BF16)


def _fourier_ctx(fb, fc):
    B, n, _ = fb.shape
    c, s = _dft_tables(n)
    fn = jnp.concatenate([c, s], axis=1).astype(BF16)
    scale = float((n * B_GROUP_DIM) ** -0.5)
    return pl.pallas_call(
        functools.partial(_fourier_ctx_kernel, scale),
        grid=(B,),
        in_specs=[pl.BlockSpec((None, n, B_WIDTH), lambda b: (b, 0, 0)),
                  pl.BlockSpec(fc.shape, lambda b: (0, 0)),
                  pl.BlockSpec(fn.shape, lambda b: (0, 0))],
        out_specs=pl.BlockSpec((None, n, B_WIDTH), lambda b: (b, 0, 0)),
        out_shape=jax.ShapeDtypeStruct((B, n, B_WIDTH), BF16),
        compiler_params=_cparams(("arbitrary",)),
        name="fourier_ctx",
    )(fb, fc, fn)


def _fourier_lat1_kernel(x_ref, fc_ref, g_ref, zr_ref, zi_ref):
    for cc in range(FOURIER_COLS):
        xc = x_ref[:, cc * B_WIDTH:(cc + 1) * B_WIDTH]
        u = _dot(xc, fc_ref[...]).astype(BF16)
        p = _dot(g_ref[cc], u)
        R = p.shape[0] // 2
        zr_ref[cc] = (p[:R, :B_WIDTH] - p[R:, B_WIDTH:]).astype(BF16)
        zi_ref[cc] = (p[:R, B_WIDTH:] + p[R:, :B_WIDTH]).astype(BF16)


def _fourier_lat2_kernel(scale, zr_ref, zi_ref, f_ref, o_ref):
    z = jnp.concatenate([zr_ref[...], zi_ref[...]], axis=0)
    o_ref[...] = (_dot(f_ref[...], z) * scale).astype(BF16)


def _fourier_lat(fb, fc):
    B, n, _ = fb.shape
    R = n // GRID_W
    W = GRID_W * B_WIDTH
    kr = jnp.arange(R, dtype=jnp.int32)
    pos = (GRID_W * jnp.arange(R, dtype=jnp.int32)[None, None, :]
           + jnp.arange(GRID_W, dtype=jnp.int32)[:, None, None])
    ang = ((kr[None, :, None] * pos) % n).astype(F32) * (2.0 * np.pi / n)
    g = jnp.concatenate([jnp.cos(ang), -jnp.sin(ang)], axis=1).astype(BF16)
    c64, s64 = _dft_tables(GRID_W)
    f2 = jnp.concatenate([c64, s64], axis=1).astype(BF16)
    nc = GRID_W // FOURIER_COLS
    wc = FOURIER_COLS * B_WIDTH
    zshape = jax.ShapeDtypeStruct((B, GRID_W, R, B_WIDTH), BF16)
    zr, zi = pl.pallas_call(
        _fourier_lat1_kernel,
        grid=(B, nc),
        in_specs=[pl.BlockSpec((None, R, wc), lambda b, j: (b, 0, j)),
                  pl.BlockSpec(fc.shape, lambda b, j: (0, 0)),
                  pl.BlockSpec((FOURIER_COLS, 2 * R, R), lambda b, j: (j, 0, 0))],
        out_specs=[pl.BlockSpec((None, FOURIER_COLS, R, B_WIDTH), lambda b, j: (b, j, 0, 0))] * 2,
        out_shape=[zshape, zshape],
        compiler_params=_cparams(("arbitrary", "arbitrary")),
        name="fourier_lat1",
    )(fb.reshape(B, R, W), fc, g)
    scale = float((n * B_GROUP_DIM) ** -0.5)
    w2 = R * B_WIDTH
    out = pl.pallas_call(
        functools.partial(_fourier_lat2_kernel, scale),
        grid=(B, nc),
        in_specs=[pl.BlockSpec((None, GRID_W, w2 // nc), lambda b, j: (b, 0, j))] * 2
        + [pl.BlockSpec(f2.shape, lambda b, j: (0, 0))],
        out_specs=pl.BlockSpec((None, GRID_W, w2 // nc), lambda b, j: (b, 0, j)),
        out_shape=jax.ShapeDtypeStruct((B, GRID_W, w2), BF16),
        compiler_params=_cparams(("arbitrary", "arbitrary")),
        name="fourier_lat2",
    )(zr.reshape(B, GRID_W, w2), zi.reshape(B, GRID_W, w2), f2)
    return out.reshape(B, n, B_WIDTH)


def _outproj_kernel(x_ref, a_ref, b_ref, c_ref, mod_ref, g_ref, wout_ref, x1_ref, h2_ref):
    mix = (_dot(a_ref[...], wout_ref[0:A_WIDTH, :])
           + _dot(b_ref[...], wout_ref[A_WIDTH:A_WIDTH + B_WIDTH, :])
           + _dot(c_ref[...], wout_ref[A_WIDTH + B_WIDTH:, :]))
    gt1 = mod_ref[:, 2 * D_MODEL:3 * D_MODEL]
    sh2 = mod_ref[:, 3 * D_MODEL:4 * D_MODEL]
    sc2 = mod_ref[:, 4 * D_MODEL:5 * D_MODEL]
    x1 = x_ref[...] + gt1 * mix
    x1_ref[...] = x1
    h2_ref[...] = (_rms(x1, g_ref[...], D_MODEL) * (1.0 + sc2) + sh2).astype(BF16)


def _outproj(latent, x, oa, ob, oc, mod_l, g_ffn, wout):
    n = x.shape[0]
    T = T_OUT
    nt = n // T
    tpb = (DEC_SEQ // T) if latent else nt
    base = 1 if latent else 0
    tok = lambda w: pl.BlockSpec((T, w), lambda i: (i, 0))
    return pl.pallas_call(
        _outproj_kernel,
        grid=(nt,),
        in_specs=[tok(D_MODEL), tok(A_WIDTH), tok(B_WIDTH), tok(C_WIDTH),
                  pl.BlockSpec((None, 1, 6 * D_MODEL), lambda i: (base + i // tpb, 0, 0)),
                  pl.BlockSpec(g_ffn.shape, lambda i: (0, 0)),
                  pl.BlockSpec(wout.shape, lambda i: (0, 0))],
        out_specs=[tok(D_MODEL), tok(D_MODEL)],
        out_shape=[jax.ShapeDtypeStruct((n, D_MODEL), F32), jax.ShapeDtypeStruct((n, D_MODEL), BF16)],
        compiler_params=_cparams(("arbitrary",)),
        name="outproj_lat" if latent else "outproj_ctx",
    )(x, oa, ob, oc, mod_l, g_ffn, wout)


def _ffn_kernel(T, S, final, x1_ref, hm_ref, hp_ref, hn_ref, mod_ref, wa_ref, wg_ref, cwa_ref, cwg_ref,
                cba_ref, cbg_ref, wd_ref, gfin_ref, o_ref, hs_scr, acc_scr):
    i = pl.program_id(0)
    j = pl.program_id(1)
    H = BF16_ROWS

    @pl.when(j == 0)
    def _():
        hs_scr[0:H] = hp_ref[...]
        hs_scr[H:T + H] = hm_ref[...]
        hs_scr[T + H:T + 2 * H] = hn_ref[...]
        acc_scr[...] = jnp.zeros(acc_scr.shape, F32)

    hs = hs_scr[...]
    pos = (i * T + lax.broadcasted_iota(jnp.int32, (T, 1), 0)) % S
    has_prev = pos != 0
    has_next = pos != S - 1

    def branch(w_ref, cw_ref, cb_ref):
        u = _dot(hs, w_ref[...])
        up = jnp.where(has_prev, u[H - 1:T + H - 1], 0.0)
        un = jnp.where(has_next, u[H + 1:T + H + 1], 0.0)
        return up * cw_ref[0:1, :] + u[H:T + H] * cw_ref[1:2, :] + un * cw_ref[2:3, :] + cb_ref[...]

    a = branch(wa_ref, cwa_ref, cba_ref)
    g = branch(wg_ref, cwg_ref, cbg_ref)
    act = (g * jax.nn.sigmoid(g) * a).astype(BF16)
    acc_scr[...] += _dot(act, wd_ref[...])

    @pl.when(j == pl.num_programs(1) - 1)
    def _():
        gt2 = mod_ref[:, 5 * D_MODEL:6 * D_MODEL]
        x2 = x1_ref[...] + gt2 * acc_scr[...]
        if final:
            x2 = _rms(x2, gfin_ref[...], D_MODEL)
        o_ref[...] = x2


def _ffn(latent, final, x1, h2, mod_l, wug, conv_w, conv_b, wdown, g_final):
    n = x1.shape[0]
    T = T_FFN
    S = DEC_SEQ if latent else SEQ
    nt = n // T
    nj = D_FF // F_CHUNK
    tpb = (DEC_SEQ // T) if latent else nt
    base = 1 if latent else 0
    hb = T // BF16_ROWS
    nhb = n // BF16_ROWS
    conv_b2 = conv_b.reshape(1, 2 * D_FF)
    return pl.pallas_call(
        functools.partial(_ffn_kernel, T, S, final),
        grid=(nt, nj),
        in_specs=[
            pl.BlockSpec((T, D_MODEL), lambda i, j: (i, 0)),
            pl.BlockSpec((T, D_MODEL), lambda i, j: (i, 0)),
            pl.BlockSpec((BF16_ROWS, D_MODEL), lambda i, j: (jnp.maximum(i * hb - 1, 0), 0)),
            pl.BlockSpec((BF16_ROWS, D_MODEL), lambda i, j: (jnp.minimum((i + 1) * hb, nhb - 1), 0)),
            pl.BlockSpec((None, 1, 6 * D_MODEL), lambda i, j: (base + i // tpb, 0, 0)),
            pl.BlockSpec((D_MODEL, F_CHUNK), lambda i, j: (0, j)),
            pl.BlockSpec((D_MODEL, F_CHUNK), lambda i, j: (0, nj + j)),
            pl.BlockSpec((3, F_CHUNK), lambda i, j: (0, j)),
            pl.BlockSpec((3, F_CHUNK), lambda i, j: (0, nj + j)),
            pl.BlockSpec((1, F_CHUNK), lambda i, j: (0, j)),
            pl.BlockSpec((1, F_CHUNK), lambda i, j: (0, nj + j)),
            pl.BlockSpec((F_CHUNK, D_MODEL), lambda i, j: (j, 0)),
            pl.BlockSpec((1, D_MODEL), lambda i, j: (0, 0)),
        ],
        out_specs=pl.BlockSpec((T, D_MODEL), lambda i, j: (i, 0)),
        out_shape=jax.ShapeDtypeStruct((n, D_MODEL), F32),
        scratch_shapes=[pltpu.VMEM((T + 2 * BF16_ROWS, D_MODEL), BF16), pltpu.VMEM((T, D_MODEL), F32)],
        compiler_params=_cparams(("arbitrary", "arbitrary")),
        name="ffn_lat" if latent else "ffn_ctx",
    )(x1, h2, h2, h2, mod_l, wug, wug, conv_w, conv_w, conv_b2, conv_b2, wdown, g_final)


def _rope_tables(n_tok, dim, lane0, width):
    rows = n_tok // GRID_W
    r = jnp.repeat(jnp.arange(rows), GRID_W).astype(F32)
    col = jnp.tile(jnp.arange(GRID_W), rows).astype(F32)
    quarter = dim // 4
    inv = ROPE_BASE ** (-jnp.arange(quarter, dtype=F32) / quarter)
    ang_r = r[:, None] * inv
    ang_c = col[:, None] * inv
    cos = jnp.concatenate([jnp.cos(ang_r)] * 2 + [jnp.cos(ang_c)] * 2, axis=1)
    sin = jnp.concatenate([-jnp.sin(ang_r), jnp.sin(ang_r), -jnp.sin(ang_c), jnp.sin(ang_c)], axis=1)
    reps = width // dim
    cos = jnp.concatenate([cos] * reps, axis=1)
    sin = jnp.concatenate([sin] * reps, axis=1)
    pad = ((0, 0), (lane0, LANES - lane0 - width))
    return jnp.pad(cos, pad, constant_values=1.0), jnp.pad(sin, pad)


def _layer_weights(w_in, g_cq, w_uq, w_ukv):
    z = lambda r, c: jnp.zeros((r, c), F32)
    o = np.cumsum([0, A_WIDTH, A_KV_WIDTH, A_KV_WIDTH, B_WIDTH, C_Q_LORA, C_KV_LORA, C_ROPE])
    win_p = jnp.concatenate([
        w_in[:, o[0]:o[1]].reshape(D_MODEL, A_HEADS, HEAD_DIM)[:, A_HEAD_ORDER, :].reshape(D_MODEL, A_WIDTH),
        w_in[:, o[1]:o[4]],
        w_in[:, o[4]:o[5]], z(D_MODEL, CQ_PAD - C_Q_LORA),
        w_in[:, o[5]:o[6]],
        z(D_MODEL, KR_LANE0), w_in[:, o[6]:o[7]], z(D_MODEL, LANES - KR_LANE0 - C_ROPE),
    ], axis=1).astype(BF16)
    gcq_p = jnp.pad(g_cq, (0, CQ_PAD - C_Q_LORA)).reshape(1, CQ_PAD)
    hq = C_NOPE + C_ROPE
    wuq_h = w_uq.reshape(C_Q_LORA, C_HEADS, hq)
    wuq_p = jnp.pad(wuq_h, ((0, CQ_PAD - C_Q_LORA), (0, 0), (0, C_HEAD_PAD - hq)))
    wuq_p = wuq_p.reshape(CQ_PAD, C_HEADS * C_HEAD_PAD).astype(BF16)
    wukv_h = w_ukv.reshape(C_KV_LORA, C_HEADS, C_NOPE + C_V)
    wk = jnp.pad(wukv_h[:, :, :C_NOPE], ((0, 0), (0, 0), (0, C_HEAD_PAD - C_NOPE)))
    zv = jnp.zeros((C_KV_LORA, C_HEAD_PAD - C_V), F32)
    wv = [wukv_h[:, h, C_NOPE:] for h in range(C_HEADS)]
    wv = jnp.concatenate([jnp.concatenate([zv, w] if h % 2 else [w, zv], axis=1) for h, w in enumerate(wv)], axis=1)
    wukv_p = jnp.concatenate([wk.reshape(C_KV_LORA, CW), wv], axis=1).astype(BF16)
    return win_p, gcq_p, wuq_p, wukv_p


def kernel(x_prompt, x_sample, cache_win_k, cache_win_v, cache_mla_ckv, cache_mla_krope, c, c_ctx,
           w_ada, b_ada, g_mix, w_in, sink, g_cq, w_uq, g_ckv, w_ukv, w_out, g_ffn, w_ug, conv_w,
           conv_b, w_down, g_final):
    n_ctx = BATCH * SEQ
    n_lat = DEC_BATCH * DEC_SEQ
    xp = x_prompt.reshape(n_ctx, D_MODEL)
    xs = x_sample.reshape(n_lat, D_MODEL)

    cvecs = jnp.concatenate([c_ctx[None, :], c, jnp.zeros((8 - 1 - DEC_BATCH, D_MODEL), F32)], axis=0)
    mod = _modulation(cvecs, w_ada, b_ada)

    cos_a, sin_a = _rope_tables(DEC_SEQ, HEAD_DIM, 0, LANES)
    cos_c, sin_c = _rope_tables(DEC_SEQ, C_ROPE, KR_LANE0, C_ROPE)
    tables = (cos_a, sin_a, cos_c, sin_c)
    fc = _channel_dft()
    g_final2 = g_final.reshape(1, D_MODEL)

    new_k, new_v, new_ckv, new_kr = [], [], [], []
    for l in range(DEPTH):
        win_p, gcq_p, wuq_p, wukv_p = _layer_weights(w_in[l], g_cq[l], w_uq[l], w_ukv[l])
        mod_l = mod[l].reshape(8, 1, 6 * D_MODEL)
        g_mix_l = g_mix[l].reshape(1, D_MODEL)
        g_ckv_l = g_ckv[l].reshape(1, C_KV_LORA)
        g_ffn_l = g_ffn[l].reshape(1, D_MODEL)
        sink_l = sink[l].reshape(1, A_HEADS)
        wout_a = w_out[l, :A_WIDTH].reshape(A_HEADS, HEAD_DIM, D_MODEL)[jnp.array(A_HEAD_ORDER)]
        wout_l = jnp.concatenate([wout_a.reshape(A_WIDTH, D_MODEL), w_out[l, A_WIDTH:]], axis=0).astype(BF16)
        wug_l = w_ug[l].astype(BF16)
        wdown_l = w_down[l].astype(BF16)
        final = l == DEPTH - 1

        (qa, ka, va, fb, qc, kc, vc, ska, sva, sckv, skr) = _projection(
            False, xp, mod_l, g_mix_l, win_p, gcq_p, wuq_p, g_ckv_l, wukv_p, None)
        r3 = lambda a: a.reshape(BATCH, SEQ, a.shape[-1])
        oa = _attn_a_ctx(sink_l, r3(qa), r3(ka), r3(va))
        ob = _fourier_ctx(r3(fb), fc)
        oc = _mla(r3(qc), r3(kc), r3(vc), None, None, SEQ, SEQ)
        f2 = lambda a: a.reshape(n_ctx, a.shape[-1])
        x1, h2 = _outproj(False, xp, f2(oa), f2(ob), f2(oc), mod_l, g_ffn_l, wout_l)
        xp = _ffn(False, final, x1, h2, mod_l, wug_l, conv_w[l], conv_b[l], wdown_l, g_final2)
        new_k.append(ska.reshape(BATCH, SEQ, A_KV_HEADS, HEAD_DIM))
        new_v.append(sva.reshape(BATCH, SEQ, A_KV_HEADS, HEAD_DIM))
        new_ckv.append(sckv.reshape(BATCH, SEQ, C_KV_LORA))
        new_kr.append(skr.reshape(BATCH, SEQ, C_ROPE))

        (qa, ka, va, fb, qc, kc, vc) = _projection(
            True, xs, mod_l, g_mix_l, win_p, gcq_p, wuq_p, g_ckv_l, wukv_p, tables)
        r3 = lambda a: a.reshape(DEC_BATCH, DEC_SEQ, a.shape[-1])
        kwin = cache_win_k[:, l].reshape(DEC_BATCH, PAST_LEN, A_KV_WIDTH).astype(BF16)
        vwin = cache_win_v[:, l].reshape(DEC_BATCH, PAST_LEN, A_KV_WIDTH).astype(BF16)
        oa = _attn_a_lat(sink_l, r3(qa), r3(ka), r3(va), kwin, vwin)
        ob = _fourier_lat(r3(fb), fc)
        kr_pad = jnp.pad(cache_mla_krope[:, l], ((0, 0), (0, 0), (KR_LANE0, LANES - KR_LANE0 - C_ROPE)))
        kc_ctx, vc_ctx = _kvcache(cache_mla_ckv[:, l], kr_pad, wukv_p)
        oc = _mla(r3(qc), r3(kc), r3(vc), kc_ctx, vc_ctx, TQ_MLA, TK_MLA)
        f2 = lambda a: a.reshape(n_lat, a.shape[-1])
        x1, h2 = _outproj(True, xs, f2(oa), f2(ob), f2(oc), mod_l, g_ffn_l, wout_l)
        xs = _ffn(True, final, x1, h2, mod_l, wug_l, conv_w[l], conv_b[l], wdown_l, g_final2)

    y_prompt = xp.reshape(BATCH, SEQ, D_MODEL)
    y_sample = xs.reshape(DEC_BATCH, DEC_SEQ, D_MODEL)
    return (y_prompt, y_sample, jnp.stack(new_k, axis=1), jnp.stack(new_v, axis=1),
            jnp.stack(new_ckv, axis=1), jnp.stack(new_kr, axis=1))
```

```python
import functools

import numpy as np
import jax
import jax.numpy as jnp
from jax import lax
from jax.experimental import pallas as pl
from jax.experimental.pallas import tpu as pltpu

F32 = jnp.float32
BF16 = jnp.bfloat16

D_MODEL = 1024
BATCH = 16
SEQ = 256
DEPTH = 2
DEC_BATCH = 4
DEC_SEQ = 4096
PAST_LEN = 256
GRID_W = 64
HEAD_DIM = 64
A_HEADS = 8
A_KV_HEADS = 2
A_GROUP = 4
A_WIDTH = 512
A_KV_WIDTH = 128
WINDOW = 128
BLOCK = 128
B_WIDTH = 256
B_GROUP_DIM = 64
B_GROUPS = 4
C_HEADS = 4
C_NOPE = 64
C_ROPE = 32
C_V = 64
C_Q_LORA = 192
C_KV_LORA = 128
C_WIDTH = 256
D_FF = 2816
ROPE_BASE = 10000.0
EPS = 1e-6
NEG = -1e30

LANES = 128
BF16_ROWS = 16
C_HEAD_PAD = 128
CW = C_HEADS * C_HEAD_PAD
CQ_PAD = 256
VMEM_LIMIT = 56 * 1024 * 1024

_QA0, _KA0, _VA0, _FB0, _CQ0, _CKV0, _KR0, _IN_PAD = 0, 512, 640, 768, 1024, 1280, 1408, 1536
KR_LANE0 = C_NOPE

T_PROJ = 512
T_OUT = 512
T_FFN = 1024
F_CHUNK = 256
TQ_MLA = 512
TK_MLA = 512
A_SUB = 4
A_HEAD_ORDER = [h for j in range(A_GROUP) for h in (j, A_GROUP + j)]
LOG2_E = float(np.log2(np.e))
FOURIER_COLS = 8


def _cparams(sem):
    return pltpu.CompilerParams(dimension_semantics=sem, vmem_limit_bytes=VMEM_LIMIT)


def _dot(a, b):
    return jnp.dot(a, b, preferred_element_type=F32)


def _dot_nt(a, b):
    return lax.dot_general(a, b, (((1,), (1,)), ((), ())), preferred_element_type=F32)


def _rms(x, g, n):
    ms = jnp.sum(x * x, axis=-1, keepdims=True) * (1.0 / n)
    return x * lax.rsqrt(ms + EPS) * g


def _mod_kernel(c_ref, w_ref, b_ref, o_ref):
    cv = c_ref[...]
    s = cv * jax.nn.sigmoid(cv)
    o_ref[...] = jnp.dot(s, w_ref[...], preferred_element_type=F32,
                         precision=lax.Precision.HIGHEST) + b_ref[...]


def _modulation(cvecs, w_ada, b_ada):
    nj = 6
    return pl.pallas_call(
        _mod_kernel,
        grid=(DEPTH, nj),
        in_specs=[
            pl.BlockSpec((8, D_MODEL), lambda l, j: (0, 0)),
            pl.BlockSpec((None, D_MODEL, D_MODEL), lambda l, j: (l, 0, j)),
            pl.BlockSpec((None, 1, D_MODEL), lambda l, j: (l, 0, j)),
        ],
        out_specs=pl.BlockSpec((None, 8, D_MODEL), lambda l, j: (l, 0, j)),
        out_shape=jax.ShapeDtypeStruct((DEPTH, 8, 6 * D_MODEL), F32),
        compiler_params=_cparams(("arbitrary", "arbitrary")),
        name="modulation",
    )(cvecs, w_ada, b_ada.reshape(DEPTH, 1, 6 * D_MODEL))


def _rope_block(x, cos, sin, half):
    lane = lax.broadcasted_iota(jnp.int32, x.shape, 1)
    first = (lane % (2 * half)) < half
    partner = jnp.where(first, pltpu.roll(x, LANES - half, 1), pltpu.roll(x, half, 1))
    return x * cos + partner * sin


def _rope(x, cos, sin, half):
    blocks = [_rope_block(x[:, j:j + LANES], cos, sin, half) for j in range(0, x.shape[1], LANES)]
    return blocks[0] if len(blocks) == 1 else jnp.concatenate(blocks, axis=1)


def _proj_kernel(latent, x_ref, mod_ref, g_ref, win_ref, gcq_ref, wuq_ref, gckv_ref, wukv_ref, *rest):
    if latent:
        cosa_ref, sina_ref, cosc_ref, sinc_ref = rest[:4]
        qa_ref, ka_ref, va_ref, fb_ref, qc_ref, kc_ref, vc_ref = rest[4:]
    else:
        qa_ref, ka_ref, va_ref, fb_ref, qc_ref, kc_ref, vc_ref, ska_ref, sva_ref, sckv_ref, skr_ref = rest
    x = x_ref[...]
    y = _rms(x, g_ref[...], D_MODEL)
    sh1 = mod_ref[:, 0:D_MODEL]
    sc1 = mod_ref[:, D_MODEL:2 * D_MODEL]
    h = (y * (1.0 + sc1) + sh1).astype(BF16)

    qa = _dot(h, win_ref[:, _QA0:_KA0])
    ka = _dot(h, win_ref[:, _KA0:_VA0])
    va = _dot(h, win_ref[:, _VA0:_FB0])
    fb = _dot(h, win_ref[:, _FB0:_CQ0])
    cq = _dot(h, win_ref[:, _CQ0:_CKV0])
    ckv = _dot(h, win_ref[:, _CKV0:_KR0])
    kr = _dot(h, win_ref[:, _KR0:_IN_PAD])

    if not latent:
        ska_ref[...] = ka
        sva_ref[...] = va
        skr_ref[...] = kr[:, KR_LANE0:KR_LANE0 + C_ROPE]
    else:
        qa = _rope(qa, cosa_ref[...], sina_ref[...], HEAD_DIM // 4)
        ka = _rope(ka, cosa_ref[...], sina_ref[...], HEAD_DIM // 4)
        kr = _rope(kr, cosc_ref[...], sinc_ref[...], C_ROPE // 4)
    qa_ref[...] = (qa * (LOG2_E * HEAD_DIM ** -0.5)).astype(BF16)
    ka_ref[...] = ka.astype(BF16)
    va_ref[...] = va.astype(BF16)
    fb_ref[...] = fb.astype(BF16)

    cqn = _rms(cq, gcq_ref[...], C_Q_LORA).astype(BF16)
    qc = _dot(cqn, wuq_ref[...])
    if latent:
        qc = _rope(qc, cosc_ref[...], sinc_ref[...], C_ROPE // 4)
    qc_ref[...] = (qc * (LOG2_E * (C_NOPE + C_ROPE) ** -0.5)).astype(BF16)

    ckvn = _rms(ckv, gckv_ref[...], C_KV_LORA)
    if not latent:
        sckv_ref[...] = ckvn
    _store_kv(_dot(ckvn.astype(BF16), wukv_ref[...]), kr, kc_ref, vc_ref)


def _store_kv(kv, kr, kc_ref, vc_ref):
    kc_ref[...] = (kv[:, :CW] + jnp.concatenate([kr] * C_HEADS, axis=1)).astype(BF16)
    lane = lax.broadcasted_iota(jnp.int32, (1, CW), 1)
    upper = lane % C_HEAD_PAD >= C_V
    odd = (lane // C_HEAD_PAD) % 2 == 1
    vc_ref[...] = (kv[:, CW:] + jnp.where(upper != odd, 1.0, 0.0).astype(F32)).astype(BF16)


def _projection(latent, x, mod_l, g_mix, win_p, gcq_p, wuq_p, gckv, wukv_p, tables):
    n = x.shape[0]
    T = T_PROJ
    nt = n // T
    tpb = (DEC_SEQ // T) if latent else nt
    base = 1 if latent else 0
    tok = lambda w: pl.BlockSpec((T, w), lambda i: (i, 0))
    full = lambda a: pl.BlockSpec(a.shape, lambda i: (0,) * a.ndim)
    in_specs = [
        tok(D_MODEL),
        pl.BlockSpec((None, 1, 6 * D_MODEL), lambda i: (base + i // tpb, 0, 0)),
        full(g_mix), full(win_p), full(gcq_p), full(wuq_p), full(gckv), full(wukv_p),
    ]
    args = [x, mod_l, g_mix, win_p, gcq_p, wuq_p, gckv, wukv_p]
    if latent:
        in_specs += [pl.BlockSpec((T, LANES), lambda i: (i % tpb, 0))] * 4
        args += list(tables)
    widths = [A_WIDTH, A_KV_WIDTH, A_KV_WIDTH, B_WIDTH, CW, CW, CW]
    out_specs = [tok(w) for w in widths]
    out_shape = [jax.ShapeDtypeStruct((n, w), BF16) for w in widths]
    if not latent:
        sw = [A_KV_WIDTH, A_KV_WIDTH, C_KV_LORA, C_ROPE]
        out_specs += [tok(w) for w in sw]
        out_shape += [jax.ShapeDtypeStruct((n, w), F32) for w in sw]
    return pl.pallas_call(
        functools.partial(_proj_kernel, latent),
        grid=(nt,),
        in_specs=in_specs,
        out_specs=out_specs,
        out_shape=out_shape,
        compiler_params=_cparams(("arbitrary",)),
        name="proj_lat" if latent else "proj_ctx",
    )(*args)


def _kvcache_kernel(ckv_ref, kr_ref, wukv_ref, kc_ref, vc_ref):
    _store_kv(_dot(ckv_ref[...].astype(BF16), wukv_ref[...]), kr_ref[...], kc_ref, vc_ref)


def _kvcache(ckv, kr_pad, wukv_p):
    B, P, _ = ckv.shape
    return pl.pallas_call(
        _kvcache_kernel,
        grid=(B,),
        in_specs=[
            pl.BlockSpec((None, P, C_KV_LORA), lambda b: (b, 0, 0)),
            pl.BlockSpec((None, P, LANES), lambda b: (b, 0, 0)),
            pl.BlockSpec(wukv_p.shape, lambda b: (0, 0)),
        ],
        out_specs=[pl.BlockSpec((None, P, CW), lambda b: (b, 0, 0))] * 2,
        out_shape=[jax.ShapeDtypeStruct((B, P, CW), BF16)] * 2,
        compiler_params=_cparams(("arbitrary",)),
        name="kvcache",
    )(ckv, kr_pad, wukv_p)


def _gqa_group(g, q_blocks, keys, masks, values, sink_ref):
    tq = q_blocks[0].shape[0]
    lane = lax.broadcasted_iota(jnp.int32, (tq, LANES), 1)
    mine = (lane >= HEAD_DIM) if g else (lane < HEAD_DIM)
    qs = jnp.concatenate([jnp.where(mine, qb, jnp.zeros_like(qb)) for qb in q_blocks], axis=0)
    parts = []
    for k, mask in zip(keys, masks):
        s = _dot_nt(qs, k)
        parts.append(s if mask is None else jnp.where(mask, s, NEG))
    s = parts[0] if len(parts) == 1 else jnp.concatenate(parts, axis=1)
    rowblk = lax.broadcasted_iota(jnp.int32, (A_GROUP * tq, 1), 0) // tq
    sk = jnp.full((A_GROUP * tq, 1), sink_ref[0, g * A_GROUP] * LOG2_E, F32)
    for j in range(1, A_GROUP):
        sk = jnp.where(rowblk == j, sink_ref[0, g * A_GROUP + j] * LOG2_E, sk)
    m = jnp.maximum(jnp.max(s, axis=-1, keepdims=True), sk)
    p = jnp.exp2(s - m)
    l = jnp.sum(p, axis=-1, keepdims=True) + jnp.exp2(sk - m)
    v = values[0] if len(values) == 1 else jnp.concatenate(values, axis=0)
    return _dot(p.astype(BF16), v) / l


def _gqa_store(o_ref, rows, tq, o0, o1):
    low = lax.broadcasted_iota(jnp.int32, (tq, LANES), 1) < HEAD_DIM
    for j in range(A_GROUP):
        blk = jnp.where(low, o0[j * tq:(j + 1) * tq], o1[j * tq:(j + 1) * tq])
        o_ref[rows, j * LANES:(j + 1) * LANES] = blk.astype(BF16)


def _attn_a_ctx_kernel(sink_ref, q_ref, k_ref, v_ref, o_ref):
    n = q_ref.shape[0]
    q_blocks = [q_ref[:, j * LANES:(j + 1) * LANES] for j in range(A_GROUP)]
    outs = [_gqa_group(g, q_blocks, [k_ref[...]], [None], [v_ref[...]], sink_ref) for g in range(A_KV_HEADS)]
    _gqa_store(o_ref, slice(None), n, *outs)


def _attn_a_ctx(sink, q, k, v):
    B, n, _ = q.shape
    blk = lambda w: pl.BlockSpec((None, n, w), lambda b: (b, 0, 0))
    return pl.pallas_call(
        _attn_a_ctx_kernel,
        grid=(B,),
        in_specs=[pl.BlockSpec(memory_space=pltpu.SMEM), blk(A_WIDTH), blk(A_KV_WIDTH), blk(A_KV_WIDTH)],
        out_specs=blk(A_WIDTH),
        out_shape=jax.ShapeDtypeStruct((B, n, A_WIDTH), BF16),
        compiler_params=_cparams(("arbitrary",)),
        name="attn_a_ctx",
    )(sink, q, k, v)


def _attn_a_lat_kernel(sink_ref, q_ref, k_ref, v_ref, kc_ref, vc_ref, o_ref):
    n = k_ref.shape[0]
    band = 3 * BLOCK
    kcx = kc_ref[...]
    vcx = vc_ref[...]

    def body(sb, carry):
        blk = pl.program_id(1) * A_SUB + sb
        rows = pl.ds(pl.multiple_of(sb * BLOCK, BLOCK), BLOCK)
        start = pl.multiple_of(jnp.clip((blk - 1) * BLOCK, 0, n - band), BLOCK)
        kb = k_ref[pl.ds(start, band), :]
        vb = v_ref[pl.ds(start, band), :]
        shape = (A_GROUP * BLOCK, band)
        qpos = blk * BLOCK + lax.broadcasted_iota(jnp.int32, shape, 0) % BLOCK
        kpos = start + lax.broadcasted_iota(jnp.int32, shape, 1)
        mask = jnp.abs(qpos - kpos) <= WINDOW
        q_blocks = [q_ref[rows, j * LANES:(j + 1) * LANES] for j in range(A_GROUP)]
        outs = [_gqa_group(g, q_blocks, [kb, kcx], [mask, None], [vb, vcx], sink_ref)
                for g in range(A_KV_HEADS)]
        _gqa_store(o_ref, rows, BLOCK, *outs)
        return carry

    lax.fori_loop(0, A_SUB, body, 0)


def _attn_a_lat(sink, q, k, v, k_ctx, v_ctx):
    B, n, _ = q.shape
    P = k_ctx.shape[1]
    tq = A_SUB * BLOCK
    seq = lambda w, m: pl.BlockSpec((None, m, w), lambda b, i: (b, 0, 0))
    return pl.pallas_call(
        _attn_a_lat_kernel,
        grid=(B, n // tq),
        in_specs=[pl.BlockSpec(memory_space=pltpu.SMEM),
                  pl.BlockSpec((None, tq, A_WIDTH), lambda b, i: (b, i, 0)),
                  seq(A_KV_WIDTH, n), seq(A_KV_WIDTH, n), seq(A_KV_WIDTH, P), seq(A_KV_WIDTH, P)],
        out_specs=pl.BlockSpec((None, tq, A_WIDTH), lambda b, i: (b, i, 0)),
        out_shape=jax.ShapeDtypeStruct((B, n, A_WIDTH), BF16),
        compiler_params=_cparams(("arbitrary", "arbitrary")),
        name="attn_a_lat",
    )(sink, q, k, v, k_ctx, v_ctx)


def _mla_step(q_ref, k, v, m_scr, acc_scr):
    reps = k.shape[0] // LANES
    for h in range(C_HEADS):
        hs = slice(h * C_HEAD_PAD, (h + 1) * C_HEAD_PAD)
        s = _dot_nt(q_ref[:, hs], k[:, hs])
        m_old = m_scr[h]
        m_new = jnp.maximum(m_old, jnp.max(s, axis=-1, keepdims=True))
        alpha = jnp.exp2(m_old - m_new)
        p = jnp.exp2(s - jnp.concatenate([m_new] * reps, axis=1))
        m_scr[h] = m_new
        acc_scr[h] = acc_scr[h] * alpha + _dot(p.astype(BF16), v[:, hs])


def _mla_kernel(n_chunks, tk, has_ctx, q_ref, k_ref, v_ref, *rest):
    if has_ctx:
        kc_ref, vc_ref, o_ref, m_scr, acc_scr = rest
    else:
        o_ref, m_scr, acc_scr = rest
    tq = q_ref.shape[0]
    m_scr[...] = jnp.full(m_scr.shape, -jnp.inf, F32)
    acc_scr[...] = jnp.zeros(acc_scr.shape, F32)

    def body(c, carry):
        ks = pl.multiple_of(c * tk, tk)
        _mla_step(q_ref, k_ref[pl.ds(ks, tk), :], v_ref[pl.ds(ks, tk), :], m_scr, acc_scr)
        return carry

    if has_ctx:
        _mla_step(q_ref, kc_ref[...], vc_ref[...], m_scr, acc_scr)
    lax.fori_loop(0, n_chunks, body, 0)

    low = lax.broadcasted_iota(jnp.int32, (tq, LANES), 1) < C_V
    for j in range(C_HEADS // 2):
        even, odd = acc_scr[2 * j], acc_scr[2 * j + 1]
        sums = pltpu.roll(jnp.where(low, odd, even), C_V, 1)
        o_ref[:, j * LANES:(j + 1) * LANES] = (jnp.where(low, even, odd) / sums).astype(BF16)


def _mla(q, k, v, k_ctx, v_ctx, tq, tk):
    B, n, _ = q.shape
    nk = k.shape[1]
    has_ctx = k_ctx is not None
    seq = lambda m: pl.BlockSpec((None, m, CW), lambda b, i: (b, 0, 0))
    in_specs = [pl.BlockSpec((None, tq, CW), lambda b, i: (b, i, 0)), seq(nk), seq(nk)]
    args = [q, k, v]
    if has_ctx:
        in_specs += [seq(k_ctx.shape[1])] * 2
        args += [k_ctx, v_ctx]
    return pl.pallas_call(
        functools.partial(_mla_kernel, nk // tk, tk, has_ctx),
        grid=(B, n // tq),
        in_specs=in_specs,
        out_specs=pl.BlockSpec((None, tq, C_WIDTH), lambda b, i: (b, i, 0)),
        out_shape=jax.ShapeDtypeStruct((B, n, C_WIDTH), BF16),
        scratch_shapes=[pltpu.VMEM((C_HEADS, tq, LANES), F32), pltpu.VMEM((C_HEADS, tq, LANES), F32)],
        compiler_params=_cparams(("arbitrary", "arbitrary")),
        name="mla_lat" if has_ctx else "mla_ctx",
    )(*args)


def _dft_tables(n):
    j = jnp.arange(n, dtype=jnp.int32)
    ang = ((j[:, None] * j[None, :]) % n).astype(F32) * (2.0 * np.pi / n)
    return jnp.cos(ang), jnp.sin(ang)


def _channel_dft():
    c, s = _dft_tables(B_GROUP_DIM)
    eye = jnp.eye(B_GROUPS, dtype=F32)
    return jnp.concatenate([jnp.kron(eye, c), -jnp.kron(eye, s)], axis=1).astype(BF16)


def _fourier_ctx_kernel(scale, x_ref, fc_ref, fn_ref, o_ref):
    u = _dot(x_ref[...], fc_ref[...])
    ucat = jnp.concatenate([u[:, :B_WIDTH], u[:, B_WIDTH:]], axis=0).astype(BF16)
    o_ref[...] = (_dot(fn_ref[...], ucat) * scale).astype(BF16)


def _fourier_ctx(fb, fc):
    B, n, _ = fb.shape
    c, s = _dft_tables(n)
    fn = jnp.concatenate([c, s], axis=1).astype(BF16)
    scale = float((n * B_GROUP_DIM) ** -0.5)
    return pl.pallas_call(
        functools.partial(_fourier_ctx_kernel, scale),
        grid=(B,),
        in_specs=[pl.BlockSpec((None, n, B_WIDTH), lambda b: (b, 0, 0)),
                  pl.BlockSpec(fc.shape, lambda b: (0, 0)),
                  pl.BlockSpec(fn.shape, lambda b: (0, 0))],
        out_specs=pl.BlockSpec((None, n, B_WIDTH), lambda b: (b, 0, 0)),
        out_shape=jax.ShapeDtypeStruct((B, n, B_WIDTH), BF16),
        compiler_params=_cparams(("arbitrary",)),
        name="fourier_ctx",
    )(fb, fc, fn)


def _fourier_lat1_kernel(x_ref, fc_ref, g_ref, zr_ref, zi_ref):
    for cc in range(FOURIER_COLS):
        xc = x_ref[:, cc * B_WIDTH:(cc + 1) * B_WIDTH]
        u = _dot(xc, fc_ref[...]).astype(BF16)
        p = _dot(g_ref[cc], u)
        R = p.shape[0] // 2
        zr_ref[cc] = (p[:R, :B_WIDTH] - p[R:, B_WIDTH:]).astype(BF16)
        zi_ref[cc] = (p[:R, B_WIDTH:] + p[R:, :B_WIDTH]).astype(BF16)


def _fourier_lat2_kernel(scale, zr_ref, zi_ref, f_ref, o_ref):
    z = jnp.concatenate([zr_ref[...], zi_ref[...]], axis=0)
    o_ref[...] = (_dot(f_ref[...], z) * scale).astype(BF16)


def _fourier_lat(fb, fc):
    B, n, _ = fb.shape
    R = n // GRID_W
    W = GRID_W * B_WIDTH
    kr = jnp.arange(R, dtype=jnp.int32)
    pos = (GRID_W * jnp.arange(R, dtype=jnp.int32)[None, None, :]
           + jnp.arange(GRID_W, dtype=jnp.int32)[:, None, None])
    ang = ((kr[None, :, None] * pos) % n).astype(F32) * (2.0 * np.pi / n)
    g = jnp.concatenate([jnp.cos(ang), -jnp.sin(ang)], axis=1).astype(BF16)
    c64, s64 = _dft_tables(GRID_W)
    f2 = jnp.concatenate([c64, s64], axis=1).astype(BF16)
    nc = GRID_W // FOURIER_COLS
    wc = FOURIER_COLS * B_WIDTH
    zshape = jax.ShapeDtypeStruct((B, GRID_W, R, B_WIDTH), BF16)
    zr, zi = pl.pallas_call(
        _fourier_lat1_kernel,
        grid=(B, nc),
        in_specs=[pl.BlockSpec((None, R, wc), lambda b, j: (b, 0, j)),
                  pl.BlockSpec(fc.shape, lambda b, j: (0, 0)),
                  pl.BlockSpec((FOURIER_COLS, 2 * R, R), lambda b, j: (j, 0, 0))],
        out_specs=[pl.BlockSpec((None, FOURIER_COLS, R, B_WIDTH), lambda b, j: (b, j, 0, 0))] * 2,
        out_shape=[zshape, zshape],
        compiler_params=_cparams(("arbitrary", "arbitrary")),
        name="fourier_lat1",
    )(fb.reshape(B, R, W), fc, g)
    scale = float((n * B_GROUP_DIM) ** -0.5)
    w2 = R * B_WIDTH
    out = pl.pallas_call(
        functools.partial(_fourier_lat2_kernel, scale),
        grid=(B, nc),
        in_specs=[pl.BlockSpec((None, GRID_W, w2 // nc), lambda b, j: (b, 0, j))] * 2
        + [pl.BlockSpec(f2.shape, lambda b, j: (0, 0))],
        out_specs=pl.BlockSpec((None, GRID_W, w2 // nc), lambda b, j: (b, 0, j)),
        out_shape=jax.ShapeDtypeStruct((B, GRID_W, w2), BF16),
        compiler_params=_cparams(("arbitrary", "arbitrary")),
        name="fourier_lat2",
    )(zr.reshape(B, GRID_W, w2), zi.reshape(B, GRID_W, w2), f2)
    return out.reshape(B, n, B_WIDTH)


def _outproj_kernel(x_ref, a_ref, b_ref, c_ref, mod_ref, g_ref, wout_ref, x1_ref, h2_ref):
    mix = (_dot(a_ref[...], wout_ref[0:A_WIDTH, :])
           + _dot(b_ref[...], wout_ref[A_WIDTH:A_WIDTH + B_WIDTH, :])
           + _dot(c_ref[...], wout_ref[A_WIDTH + B_WIDTH:, :]))
    gt1 = mod_ref[:, 2 * D_MODEL:3 * D_MODEL]
    sh2 = mod_ref[:, 3 * D_MODEL:4 * D_MODEL]
    sc2 = mod_ref[:, 4 * D_MODEL:5 * D_MODEL]
    x1 = x_ref[...] + gt1 * mix
    x1_ref[...] = x1
    h2_ref[...] = (_rms(x1, g_ref[...], D_MODEL) * (1.0 + sc2) + sh2).astype(BF16)


def _outproj(latent, x, oa, ob, oc, mod_l, g_ffn, wout):
    n = x.shape[0]
    T = T_OUT
    nt = n // T
    tpb = (DEC_SEQ // T) if latent else nt
    base = 1 if latent else 0
    tok = lambda w: pl.BlockSpec((T, w), lambda i: (i, 0))
    return pl.pallas_call(
        _outproj_kernel,
        grid=(nt,),
        in_specs=[tok(D_MODEL), tok(A_WIDTH), tok(B_WIDTH), tok(C_WIDTH),
                  pl.BlockSpec((None, 1, 6 * D_MODEL), lambda i: (base + i // tpb, 0, 0)),
                  pl.BlockSpec(g_ffn.shape, lambda i: (0, 0)),
                  pl.BlockSpec(wout.shape, lambda i: (0, 0))],
        out_specs=[tok(D_MODEL), tok(D_MODEL)],
        out_shape=[jax.ShapeDtypeStruct((n, D_MODEL), F32), jax.ShapeDtypeStruct((n, D_MODEL), BF16)],
        compiler_params=_cparams(("arbitrary",)),
        name="outproj_lat" if latent else "outproj_ctx",
    )(x, oa, ob, oc, mod_l, g_ffn, wout)


def _ffn_kernel(T, S, final, x1_ref, hm_ref, hp_ref, hn_ref, mod_ref, wug_ref, cw_ref, cb_ref, wd_ref, gfin_ref,
                o_ref, hs_scr, u0_scr, u1_scr, act_scr):
    i = pl.program_id(0)
    H = BF16_ROWS
    at_start = (i * T) % S == 0
    at_end = ((i + 1) * T) % S == 0
    hs_scr[0:H] = jnp.where(at_start, jnp.zeros_like(hp_ref[...]), hp_ref[...])
    hs_scr[H:T + H] = hm_ref[...]
    hs_scr[T + H:T + 2 * H] = jnp.where(at_end, jnp.zeros_like(hn_ref[...]), hn_ref[...])
    if S < T:
        pos = lax.broadcasted_iota(jnp.int32, (T, 1), 0) % S
        has_prev = pos != 0
        has_next = pos != S - 1

    n_chunks = D_FF // F_CHUNK
    cols = lambda j: (pl.multiple_of(j * F_CHUNK, F_CHUNK), pl.multiple_of(j * F_CHUNK + D_FF, LANES))

    def up_proj(j, u_ref):
        for b, col in enumerate(cols(j)):
            u_ref[b] = _dot(hs_scr[...], wug_ref[:, pl.ds(col, F_CHUNK)])

    def conv(u_ref, b, col):
        u = u_ref[b]
        rows = u.shape[0]
        up = pltpu.roll(u, 1, 0)[H:T + H]
        un = pltpu.roll(u, rows - 1, 0)[H:T + H]
        if S < T:
            up = jnp.where(has_prev, up, 0.0)
            un = jnp.where(has_next, un, 0.0)
        cw = cw_ref[:, pl.ds(col, F_CHUNK)]
        return (up * cw[0:1, :] + u[H:T + H] * cw[1:2, :] + un * cw[2:3, :]
                + cb_ref[:, pl.ds(col, F_CHUNK)])

    def gate(j, u_ref):
        a, g = [conv(u_ref, b, col) for b, col in enumerate(cols(j))]
        act_scr[:, pl.ds(cols(j)[0], F_CHUNK)] = (g * jax.nn.sigmoid(g) * a).astype(BF16)

    def pair(k, carry):
        up_proj(2 * k + 1, u1_scr)
        gate(2 * k, u0_scr)
        up_proj(2 * k + 2, u0_scr)
        gate(2 * k + 1, u1_scr)
        return carry

    assert n_chunks % 2 == 1
    up_proj(0, u0_scr)
    lax.fori_loop(0, n_chunks // 2, pair, 0)
    gate(n_chunks - 1, u0_scr)
    gt2 = mod_ref[:, 5 * D_MODEL:6 * D_MODEL]
    x2 = x1_ref[...] + gt2 * _dot(act_scr[...], wd_ref[...])
    if final:
        x2 = _rms(x2, gfin_ref[...], D_MODEL)
    o_ref[...] = x2


def _ffn(latent, final, x1, h2, mod_l, wug, conv_w, conv_b, wdown, g_final):
    n = x1.shape[0]
    T = T_FFN
    S = DEC_SEQ if latent else SEQ
    nt = n // T
    tpb = (DEC_SEQ // T) if latent else nt
    base = 1 if latent else 0
    hb = T // BF16_ROWS
    nhb = n // BF16_ROWS
    conv_b2 = conv_b.reshape(1, 2 * D_FF)
    whole = lambda a: pl.BlockSpec(a.shape, lambda i: (0,) * a.ndim, pipeline_mode=pl.Buffered(1))
    return pl.pallas_call(
        functools.partial(_ffn_kernel, T, S, final),
        grid=(nt,),
        in_specs=[
            pl.BlockSpec((T, D_MODEL), lambda i: (i, 0)),
            pl.BlockSpec((T, D_MODEL), lambda i: (i, 0)),
            pl.BlockSpec((BF16_ROWS, D_MODEL), lambda i: (jnp.maximum(i * hb - 1, 0), 0)),
            pl.BlockSpec((BF16_ROWS, D_MODEL), lambda i: (jnp.minimum((i + 1) * hb, nhb - 1), 0)),
            pl.BlockSpec((None, 1, 6 * D_MODEL), lambda i: (base + i // tpb, 0, 0)),
            whole(wug), whole(conv_w), whole(conv_b2), whole(wdown), whole(g_final),
        ],
        out_specs=pl.BlockSpec((T, D_MODEL), lambda i: (i, 0)),
        out_shape=jax.ShapeDtypeStruct((n, D_MODEL), F32),
        scratch_shapes=[pltpu.VMEM((T + 2 * BF16_ROWS, D_MODEL), BF16),
                        pltpu.VMEM((2, T + 2 * BF16_ROWS, F_CHUNK), F32),
                        pltpu.VMEM((2, T + 2 * BF16_ROWS, F_CHUNK), F32),
                        pltpu.VMEM((T, D_FF), BF16)],
        compiler_params=_cparams(("arbitrary",)),
        name="ffn_lat" if latent else "ffn_ctx",
    )(x1, h2, h2, h2, mod_l, wug, conv_w, conv_b2, wdown, g_final)


def _rope_tables(n_tok, dim, lane0, width):
    rows = n_tok // GRID_W
    r = jnp.repeat(jnp.arange(rows), GRID_W).astype(F32)
    col = jnp.tile(jnp.arange(GRID_W), rows).astype(F32)
    quarter = dim // 4
    inv = ROPE_BASE ** (-jnp.arange(quarter, dtype=F32) / quarter)
    ang_r = r[:, None] * inv
    ang_c = col[:, None] * inv
    cos = jnp.concatenate([jnp.cos(ang_r)] * 2 + [jnp.cos(ang_c)] * 2, axis=1)
    sin = jnp.concatenate([-jnp.sin(ang_r), jnp.sin(ang_r), -jnp.sin(ang_c), jnp.sin(ang_c)], axis=1)
    reps = width // dim
    cos = jnp.concatenate([cos] * reps, axis=1)
    sin = jnp.concatenate([sin] * reps, axis=1)
    pad = ((0, 0), (lane0, LANES - lane0 - width))
    return jnp.pad(cos, pad, constant_values=1.0), jnp.pad(sin, pad)


def _layer_weights(w_in, g_cq, w_uq, w_ukv):
    z = lambda r, c: jnp.zeros((r, c), F32)
    o = np.cumsum([0, A_WIDTH, A_KV_WIDTH, A_KV_WIDTH, B_WIDTH, C_Q_LORA, C_KV_LORA, C_ROPE])
    win_p = jnp.concatenate([
        w_in[:, o[0]:o[1]].reshape(D_MODEL, A_HEADS, HEAD_DIM)[:, A_HEAD_ORDER, :].reshape(D_MODEL, A_WIDTH),
        w_in[:, o[1]:o[4]],
        w_in[:, o[4]:o[5]], z(D_MODEL, CQ_PAD - C_Q_LORA),
        w_in[:, o[5]:o[6]],
        z(D_MODEL, KR_LANE0), w_in[:, o[6]:o[7]], z(D_MODEL, LANES - KR_LANE0 - C_ROPE),
    ], axis=1).astype(BF16)
    gcq_p = jnp.pad(g_cq, (0, CQ_PAD - C_Q_LORA)).reshape(1, CQ_PAD)
    hq = C_NOPE + C_ROPE
    wuq_h = w_uq.reshape(C_Q_LORA, C_HEADS, hq)
    wuq_p = jnp.pad(wuq_h, ((0, CQ_PAD - C_Q_LORA), (0, 0), (0, C_HEAD_PAD - hq)))
    wuq_p = wuq_p.reshape(CQ_PAD, C_HEADS * C_HEAD_PAD).astype(BF16)
    wukv_h = w_ukv.reshape(C_KV_LORA, C_HEADS, C_NOPE + C_V)
    wk = jnp.pad(wukv_h[:, :, :C_NOPE], ((0, 0), (0, 0), (0, C_HEAD_PAD - C_NOPE)))
    zv = jnp.zeros((C_KV_LORA, C_HEAD_PAD - C_V), F32)
    wv = [wukv_h[:, h, C_NOPE:] for h in range(C_HEADS)]
    wv = jnp.concatenate([jnp.concatenate([zv, w] if h % 2 else [w, zv], axis=1) for h, w in enumerate(wv)], axis=1)
    wukv_p = jnp.concatenate([wk.reshape(C_KV_LORA, CW), wv], axis=1).astype(BF16)
    return win_p, gcq_p, wuq_p, wukv_p


def kernel(x_prompt, x_sample, cache_win_k, cache_win_v, cache_mla_ckv, cache_mla_krope, c, c_ctx,
           w_ada, b_ada, g_mix, w_in, sink, g_cq, w_uq, g_ckv, w_ukv, w_out, g_ffn, w_ug, conv_w,
           conv_b, w_down, g_final):
    n_ctx = BATCH * SEQ
    n_lat = DEC_BATCH * DEC_SEQ
    xp = x_prompt.reshape(n_ctx, D_MODEL)
    xs = x_sample.reshape(n_lat, D_MODEL)

    cvecs = jnp.concatenate([c_ctx[None, :], c, jnp.zeros((8 - 1 - DEC_BATCH, D_MODEL), F32)], axis=0)
    mod = _modulation(cvecs, w_ada, b_ada)

    cos_a, sin_a = _rope_tables(DEC_SEQ, HEAD_DIM, 0, LANES)
    cos_c, sin_c = _rope_tables(DEC_SEQ, C_ROPE, KR_LANE0, C_ROPE)
    tables = (cos_a, sin_a, cos_c, sin_c)
    fc = _channel_dft()
    g_final2 = g_final.reshape(1, D_MODEL)

    new_k, new_v, new_ckv, new_kr = [], [], [], []
    for l in range(DEPTH):
        win_p, gcq_p, wuq_p, wukv_p = _layer_weights(w_in[l], g_cq[l], w_uq[l], w_ukv[l])
        mod_l = mod[l].reshape(8, 1, 6 * D_MODEL)
        g_mix_l = g_mix[l].reshape(1, D_MODEL)
        g_ckv_l = g_ckv[l].reshape(1, C_KV_LORA)
        g_ffn_l = g_ffn[l].reshape(1, D_MODEL)
        sink_l = sink[l].reshape(1, A_HEADS)
        wout_a = w_out[l, :A_WIDTH].reshape(A_HEADS, HEAD_DIM, D_MODEL)[jnp.array(A_HEAD_ORDER)]
        wout_l = jnp.concatenate([wout_a.reshape(A_WIDTH, D_MODEL), w_out[l, A_WIDTH:]], axis=0).astype(BF16)
        wug_l = w_ug[l].astype(BF16)
        wdown_l = w_down[l].astype(BF16)
        final = l == DEPTH - 1

        (qa, ka, va, fb, qc, kc, vc, ska, sva, sckv, skr) = _projection(
            False, xp, mod_l, g_mix_l, win_p, gcq_p, wuq_p, g_ckv_l, wukv_p, None)
        r3 = lambda a: a.reshape(BATCH, SEQ, a.shape[-1])
        oa = _attn_a_ctx(sink_l, r3(qa), r3(ka), r3(va))
        ob = _fourier_ctx(r3(fb), fc)
        oc = _mla(r3(qc), r3(kc), r3(vc), None, None, SEQ, SEQ)
        f2 = lambda a: a.reshape(n_ctx, a.shape[-1])
        x1, h2 = _outproj(False, xp, f2(oa), f2(ob), f2(oc), mod_l, g_ffn_l, wout_l)
        xp = _ffn(False, final, x1, h2, mod_l, wug_l, conv_w[l], conv_b[l], wdown_l, g_final2)
        new_k.append(ska.reshape(BATCH, SEQ, A_KV_HEADS, HEAD_DIM))
        new_v.append(sva.reshape(BATCH, SEQ, A_KV_HEADS, HEAD_DIM))
        new_ckv.append(sckv.reshape(BATCH, SEQ, C_KV_LORA))
        new_kr.append(skr.reshape(BATCH, SEQ, C_ROPE))

        (qa, ka, va, fb, qc, kc, vc) = _projection(
            True, xs, mod_l, g_mix_l, win_p, gcq_p, wuq_p, g_ckv_l, wukv_p, tables)
        r3 = lambda a: a.reshape(DEC_BATCH, DEC_SEQ, a.shape[-1])
        kwin = cache_win_k[:, l].reshape(DEC_BATCH, PAST_LEN, A_KV_WIDTH).astype(BF16)
        vwin = cache_win_v[:, l].reshape(DEC_BATCH, PAST_LEN, A_KV_WIDTH).astype(BF16)
        oa = _attn_a_lat(sink_l, r3(qa), r3(ka), r3(va), kwin, vwin)
        ob = _fourier_lat(r3(fb), fc)
        kr_pad = jnp.pad(cache_mla_krope[:, l], ((0, 0), (0, 0), (KR_LANE0, LANES - KR_LANE0 - C_ROPE)))
        kc_ctx, vc_ctx = _kvcache(cache_mla_ckv[:, l], kr_pad, wukv_p)
        oc = _mla(r3(qc), r3(kc), r3(vc), kc_ctx, vc_ctx, TQ_MLA, TK_MLA)
        f2 = lambda a: a.reshape(n_lat, a.shape[-1])
        x1, h2 = _outproj(True, xs, f2(oa), f2(ob), f2(oc), mod_l, g_ffn_l, wout_l)
        xs = _ffn(True, final, x1, h2, mod_l, wug_l, conv_w[l], conv_b[l], wdown_l, g_final2)

    y_prompt = xp.reshape(BATCH, SEQ, D_MODEL)
    y_sample = xs.reshape(DEC_BATCH, DEC_SEQ, D_MODEL)
    return (y_prompt, y_sample, jnp.stack(new_k, axis=1), jnp.stack(new_v, axis=1),
            jnp.stack(new_ckv, axis=1), jnp.stack(new_kr, axis=1))
```

```python
import functools

import numpy as np
import jax
import jax.numpy as jnp
from jax import lax
from jax.experimental import pallas as pl
from jax.experimental.pallas import tpu as pltpu

F32 = jnp.float32
BF16 = jnp.bfloat16

D_MODEL = 1024
BATCH = 16
SEQ = 256
DEPTH = 2
DEC_BATCH = 4
DEC_SEQ = 4096
PAST_LEN = 256
GRID_W = 64
HEAD_DIM = 64
A_HEADS = 8
A_KV_HEADS = 2
A_GROUP = 4
A_WIDTH = 512
A_KV_WIDTH = 128
WINDOW = 128
BLOCK = 128
B_WIDTH = 256
B_GROUP_DIM = 64
B_GROUPS = 4
C_HEADS = 4
C_NOPE = 64
C_ROPE = 32
C_V = 64
C_Q_LORA = 192
C_KV_LORA = 128
C_WIDTH = 256
D_FF = 2816
ROPE_BASE = 10000.0
EPS = 1e-6
NEG = -1e30

LANES = 128
BF16_ROWS = 16
C_HEAD_PAD = 128
CW = C_HEADS * C_HEAD_PAD
CQ_PAD = 256
VMEM_LIMIT = 56 * 1024 * 1024

_QA0, _KA0, _VA0, _FB0, _CQ0, _CKV0, _KR0, _IN_PAD = 0, 512, 640, 768, 1024, 1280, 1408, 1536
KR_LANE0 = C_NOPE

T_PROJ = 512
T_OUT = 512
T_FFN = 1024
F_CHUNK = 256
TQ_MLA = 1024
TK_MLA = 2048
A_SUB = 4
A_HEAD_ORDER = [h for j in range(A_GROUP) for h in (j, A_GROUP + j)]
LOG2_E = float(np.log2(np.e))
FOURIER_COLS = 8


def _cparams(sem):
    return pltpu.CompilerParams(dimension_semantics=sem, vmem_limit_bytes=VMEM_LIMIT)


def _dot(a, b):
    return jnp.dot(a, b, preferred_element_type=F32)


def _dot_nt(a, b):
    return lax.dot_general(a, b, (((1,), (1,)), ((), ())), preferred_element_type=F32)


def _rms(x, g, n):
    ms = jnp.sum(x * x, axis=-1, keepdims=True) * (1.0 / n)
    return x * lax.rsqrt(ms + EPS) * g


def _mod_kernel(c_ref, w_ref, b_ref, o_ref):
    cv = c_ref[...]
    s = cv * jax.nn.sigmoid(cv)
    o_ref[...] = jnp.dot(s, w_ref[...], preferred_element_type=F32,
                         precision=lax.Precision.HIGHEST) + b_ref[...]


def _modulation(cvecs, w_ada, b_ada):
    nj = 6
    return pl.pallas_call(
        _mod_kernel,
        grid=(DEPTH, nj),
        in_specs=[
            pl.BlockSpec((8, D_MODEL), lambda l, j: (0, 0)),
            pl.BlockSpec((None, D_MODEL, D_MODEL), lambda l, j: (l, 0, j)),
            pl.BlockSpec((None, 1, D_MODEL), lambda l, j: (l, 0, j)),
        ],
        out_specs=pl.BlockSpec((None, 8, D_MODEL), lambda l, j: (l, 0, j)),
        out_shape=jax.ShapeDtypeStruct((DEPTH, 8, 6 * D_MODEL), F32),
        compiler_params=_cparams(("arbitrary", "arbitrary")),
        name="modulation",
    )(cvecs, w_ada, b_ada.reshape(DEPTH, 1, 6 * D_MODEL))


def _rope_block(x, cos, sin, half):
    lane = lax.broadcasted_iota(jnp.int32, x.shape, 1)
    first = (lane % (2 * half)) < half
    partner = jnp.where(first, pltpu.roll(x, LANES - half, 1), pltpu.roll(x, half, 1))
    return x * cos + partner * sin


def _rope(x, cos, sin, half):
    blocks = [_rope_block(x[:, j:j + LANES], cos, sin, half) for j in range(0, x.shape[1], LANES)]
    return blocks[0] if len(blocks) == 1 else jnp.concatenate(blocks, axis=1)


def _proj_kernel(latent, x_ref, mod_ref, g_ref, win_ref, gcq_ref, wuq_ref, gckv_ref, wukv_ref, *rest):
    if latent:
        cosa_ref, sina_ref, cosc_ref, sinc_ref = rest[:4]
        qa_ref, ka_ref, va_ref, fb_ref, qc_ref, kc_ref, vc_ref = rest[4:]
    else:
        qa_ref, ka_ref, va_ref, fb_ref, qc_ref, kc_ref, vc_ref, ska_ref, sva_ref, sckv_ref, skr_ref = rest
    x = x_ref[...]
    y = _rms(x, g_ref[...], D_MODEL)
    sh1 = mod_ref[:, 0:D_MODEL]
    sc1 = mod_ref[:, D_MODEL:2 * D_MODEL]
    h = (y * (1.0 + sc1) + sh1).astype(BF16)

    qa = _dot(h, win_ref[:, _QA0:_KA0])
    kva = _dot(h, win_ref[:, _KA0:_FB0])
    ka, va = kva[:, :A_KV_WIDTH], kva[:, A_KV_WIDTH:]
    fb = _dot(h, win_ref[:, _FB0:_CQ0])
    cq = _dot(h, win_ref[:, _CQ0:_CKV0])
    ckr = _dot(h, win_ref[:, _CKV0:_IN_PAD])
    ckv, kr = ckr[:, :C_KV_LORA], ckr[:, C_KV_LORA:]

    if not latent:
        ska_ref[...] = ka
        sva_ref[...] = va
        skr_ref[...] = kr[:, KR_LANE0:KR_LANE0 + C_ROPE]
    else:
        qa = _rope(qa, cosa_ref[...], sina_ref[...], HEAD_DIM // 4)
        ka = _rope(ka, cosa_ref[...], sina_ref[...], HEAD_DIM // 4)
        kr = _rope(kr, cosc_ref[...], sinc_ref[...], C_ROPE // 4)
    qa_ref[...] = (qa * (LOG2_E * HEAD_DIM ** -0.5)).astype(BF16)
    ka_ref[...] = ka.astype(BF16)
    va_ref[...] = va.astype(BF16)
    fb_ref[...] = fb.astype(fb_ref.dtype)

    cqn = _rms(cq, gcq_ref[...], C_Q_LORA).astype(BF16)
    qc = _dot(cqn, wuq_ref[...])
    if latent:
        qc = _rope(qc, cosc_ref[...], sinc_ref[...], C_ROPE // 4)
    qc_ref[...] = (qc * (LOG2_E * (C_NOPE + C_ROPE) ** -0.5)).astype(BF16)

    ckvn = _rms(ckv, gckv_ref[...], C_KV_LORA)
    if not latent:
        sckv_ref[...] = ckvn
    _store_kv(_dot(ckvn.astype(BF16), wukv_ref[...]), kr, kc_ref, vc_ref)


def _store_kv(kv, kr, kc_ref, vc_ref):
    kc_ref[...] = (kv[:, :CW] + jnp.concatenate([kr] * C_HEADS, axis=1)).astype(BF16)
    lane = lax.broadcasted_iota(jnp.int32, (1, CW), 1)
    upper = lane % C_HEAD_PAD >= C_V
    odd = (lane // C_HEAD_PAD) % 2 == 1
    vc_ref[...] = (kv[:, CW:] + jnp.where(upper != odd, 1.0, 0.0).astype(F32)).astype(BF16)


def _projection(latent, x, mod_l, g_mix, win_p, gcq_p, wuq_p, gckv, wukv_p, tables):
    n = x.shape[0]
    T = T_PROJ
    nt = n // T
    tpb = (DEC_SEQ // T) if latent else nt
    base = 1 if latent else 0
    tok = lambda w: pl.BlockSpec((T, w), lambda i: (i, 0))
    full = lambda a: pl.BlockSpec(a.shape, lambda i: (0,) * a.ndim)
    in_specs = [
        tok(D_MODEL),
        pl.BlockSpec((None, 1, 6 * D_MODEL), lambda i: (base + i // tpb, 0, 0)),
        full(g_mix), full(win_p), full(gcq_p), full(wuq_p), full(gckv), full(wukv_p),
    ]
    args = [x, mod_l, g_mix, win_p, gcq_p, wuq_p, gckv, wukv_p]
    if latent:
        in_specs += [pl.BlockSpec((T, LANES), lambda i: (i % tpb, 0))] * 4
        args += list(tables)
    widths = [A_WIDTH, A_KV_WIDTH, A_KV_WIDTH, B_WIDTH, CW, CW, CW]
    out_specs = [tok(w) for w in widths]
    out_shape = [jax.ShapeDtypeStruct((n, w), BF16) for w in widths]
    if latent:
        out_shape[3] = jax.ShapeDtypeStruct((n, B_WIDTH), F32)
    if not latent:
        sw = [A_KV_WIDTH, A_KV_WIDTH, C_KV_LORA, C_ROPE]
        out_specs += [tok(w) for w in sw]
        out_shape += [jax.ShapeDtypeStruct((n, w), F32) for w in sw]
    return pl.pallas_call(
        functools.partial(_proj_kernel, latent),
        grid=(nt,),
        in_specs=in_specs,
        out_specs=out_specs,
        out_shape=out_shape,
        compiler_params=_cparams(("arbitrary",)),
        name="proj_lat" if latent else "proj_ctx",
    )(*args)


def _kvcache_kernel(ckv_ref, kr_ref, wukv_ref, kc_ref, vc_ref):
    _store_kv(_dot(ckv_ref[...].astype(BF16), wukv_ref[...]), kr_ref[...], kc_ref, vc_ref)


def _kvcache(ckv, kr_pad, wukv_p):
    B, P, _ = ckv.shape
    return pl.pallas_call(
        _kvcache_kernel,
        grid=(B,),
        in_specs=[
            pl.BlockSpec((None, P, C_KV_LORA), lambda b: (b, 0, 0)),
            pl.BlockSpec((None, P, LANES), lambda b: (b, 0, 0)),
            pl.BlockSpec(wukv_p.shape, lambda b: (0, 0)),
        ],
        out_specs=[pl.BlockSpec((None, P, CW), lambda b: (b, 0, 0))] * 2,
        out_shape=[jax.ShapeDtypeStruct((B, P, CW), BF16)] * 2,
        compiler_params=_cparams(("arbitrary",)),
        name="kvcache",
    )(ckv, kr_pad, wukv_p)


def _gqa_group(g, q_blocks, keys, masks, values, sink_ref):
    tq = q_blocks[0].shape[0]
    lane = lax.broadcasted_iota(jnp.int32, (tq, LANES), 1)
    mine = (lane >= HEAD_DIM) if g else (lane < HEAD_DIM)
    qs = jnp.concatenate([jnp.where(mine, qb, jnp.zeros_like(qb)) for qb in q_blocks], axis=0)
    parts = []
    for k, mask in zip(keys, masks):
        s = _dot_nt(qs, k)
        parts.append(s if mask is None else jnp.where(mask, s, NEG))
    s = parts[0] if len(parts) == 1 else jnp.concatenate(parts, axis=1)
    rowblk = lax.broadcasted_iota(jnp.int32, (A_GROUP * tq, 1), 0) // tq
    sk = jnp.full((A_GROUP * tq, 1), sink_ref[0, g * A_GROUP] * LOG2_E, F32)
    for j in range(1, A_GROUP):
        sk = jnp.where(rowblk == j, sink_ref[0, g * A_GROUP + j] * LOG2_E, sk)
    m = jnp.maximum(jnp.max(s, axis=-1, keepdims=True), sk)
    p = jnp.exp2(s - m)
    l = jnp.sum(p, axis=-1, keepdims=True) + jnp.exp2(sk - m)
    v = values[0] if len(values) == 1 else jnp.concatenate(values, axis=0)
    return _dot(p.astype(BF16), v) / l


def _gqa_store(o_ref, rows, tq, o0, o1):
    low = lax.broadcasted_iota(jnp.int32, (tq, LANES), 1) < HEAD_DIM
    for j in range(A_GROUP):
        blk = jnp.where(low, o0[j * tq:(j + 1) * tq], o1[j * tq:(j + 1) * tq])
        o_ref[rows, j * LANES:(j + 1) * LANES] = blk.astype(BF16)


def _attn_a_ctx_kernel(sink_ref, q_ref, k_ref, v_ref, o_ref):
    n = q_ref.shape[0]
    q_blocks = [q_ref[:, j * LANES:(j + 1) * LANES] for j in range(A_GROUP)]
    outs = [_gqa_group(g, q_blocks, [k_ref[...]], [None], [v_ref[...]], sink_ref) for g in range(A_KV_HEADS)]
    _gqa_store(o_ref, slice(None), n, *outs)


def _attn_a_ctx(sink, q, k, v):
    B, n, _ = q.shape
    blk = lambda w: pl.BlockSpec((None, n, w), lambda b: (b, 0, 0))
    return pl.pallas_call(
        _attn_a_ctx_kernel,
        grid=(B,),
        in_specs=[pl.BlockSpec(memory_space=pltpu.SMEM), blk(A_WIDTH), blk(A_KV_WIDTH), blk(A_KV_WIDTH)],
        out_specs=blk(A_WIDTH),
        out_shape=jax.ShapeDtypeStruct((B, n, A_WIDTH), BF16),
        compiler_params=_cparams(("arbitrary",)),
        name="attn_a_ctx",
    )(sink, q, k, v)


def _attn_a_lat_kernel(sink_ref, q_ref, k_ref, v_ref, kc_ref, vc_ref, o_ref):
    n = k_ref.shape[0]
    band = 3 * BLOCK
    kcx = kc_ref[...]
    vcx = vc_ref[...]

    def body(sb, carry):
        blk = pl.program_id(1) * A_SUB + sb
        rows = pl.ds(pl.multiple_of(sb * BLOCK, BLOCK), BLOCK)
        start = pl.multiple_of(jnp.clip((blk - 1) * BLOCK, 0, n - band), BLOCK)
        kb = k_ref[pl.ds(start, band), :]
        vb = v_ref[pl.ds(start, band), :]
        shape = (A_GROUP * BLOCK, band)
        qpos = blk * BLOCK + lax.broadcasted_iota(jnp.int32, shape, 0) % BLOCK
        kpos = start + lax.broadcasted_iota(jnp.int32, shape, 1)
        mask = jnp.abs(qpos - kpos) <= WINDOW
        q_blocks = [q_ref[rows, j * LANES:(j + 1) * LANES] for j in range(A_GROUP)]
        outs = [_gqa_group(g, q_blocks, [kb, kcx], [mask, None], [vb, vcx], sink_ref)
                for g in range(A_KV_HEADS)]
        _gqa_store(o_ref, rows, BLOCK, *outs)
        return carry

    lax.fori_loop(0, A_SUB, body, 0, unroll=2)


def _attn_a_lat(sink, q, k, v, k_ctx, v_ctx):
    B, n, _ = q.shape
    P = k_ctx.shape[1]
    tq = A_SUB * BLOCK
    seq = lambda w, m: pl.BlockSpec((None, m, w), lambda b, i: (b, 0, 0))
    return pl.pallas_call(
        _attn_a_lat_kernel,
        grid=(B, n // tq),
        in_specs=[pl.BlockSpec(memory_space=pltpu.SMEM),
                  pl.BlockSpec((None, tq, A_WIDTH), lambda b, i: (b, i, 0)),
                  seq(A_KV_WIDTH, n), seq(A_KV_WIDTH, n), seq(A_KV_WIDTH, P), seq(A_KV_WIDTH, P)],
        out_specs=pl.BlockSpec((None, tq, A_WIDTH), lambda b, i: (b, i, 0)),
        out_shape=jax.ShapeDtypeStruct((B, n, A_WIDTH), BF16),
        compiler_params=_cparams(("arbitrary", "arbitrary")),
        name="attn_a_lat",
    )(sink, q, k, v, k_ctx, v_ctx)


def _mla_step(q_ref, k, v, m_scr, acc_scr):
    reps = k.shape[0] // LANES
    for h in range(C_HEADS):
        hs = slice(h * C_HEAD_PAD, (h + 1) * C_HEAD_PAD)
        s = _dot_nt(q_ref[:, hs], k[:, hs])
        m_old = m_scr[h]
        m_new = jnp.maximum(m_old, jnp.max(s, axis=-1, keepdims=True))
        alpha = jnp.exp2(m_old - m_new)
        p = jnp.exp2(s - jnp.concatenate([m_new] * reps, axis=1))
        m_scr[h] = m_new
        acc_scr[h] = acc_scr[h] * alpha + _dot(p.astype(BF16), v[:, hs])


def _mla_kernel(n_chunks, tk, has_ctx, q_ref, k_ref, v_ref, *rest):
    if has_ctx:
        kc_ref, vc_ref, o_ref, m_scr, acc_scr = rest
    else:
        o_ref, m_scr, acc_scr = rest
    tq = q_ref.shape[0]
    m_scr[...] = jnp.full(m_scr.shape, -jnp.inf, F32)
    acc_scr[...] = jnp.zeros(acc_scr.shape, F32)

    def body(c, carry):
        ks = pl.multiple_of(c * tk, tk)
        _mla_step(q_ref, k_ref[pl.ds(ks, tk), :], v_ref[pl.ds(ks, tk), :], m_scr, acc_scr)
        return carry

    if has_ctx:
        _mla_step(q_ref, kc_ref[...], vc_ref[...], m_scr, acc_scr)
    lax.fori_loop(0, n_chunks, body, 0)

    low = lax.broadcasted_iota(jnp.int32, (tq, LANES), 1) < C_V
    for j in range(C_HEADS // 2):
        even, odd = acc_scr[2 * j], acc_scr[2 * j + 1]
        sums = pltpu.roll(jnp.where(low, odd, even), C_V, 1)
        o_ref[:, j * LANES:(j + 1) * LANES] = (jnp.where(low, even, odd) / sums).astype(BF16)


def _mla(q, k, v, k_ctx, v_ctx, tq, tk):
    B, n, _ = q.shape
    nk = k.shape[1]
    has_ctx = k_ctx is not None
    seq = lambda m: pl.BlockSpec((None, m, CW), lambda b, i: (b, 0, 0))
    in_specs = [pl.BlockSpec((None, tq, CW), lambda b, i: (b, i, 0)), seq(nk), seq(nk)]
    args = [q, k, v]
    if has_ctx:
        in_specs += [seq(k_ctx.shape[1])] * 2
        args += [k_ctx, v_ctx]
    return pl.pallas_call(
        functools.partial(_mla_kernel, nk // tk, tk, has_ctx),
        grid=(B, n // tq),
        in_specs=in_specs,
        out_specs=pl.BlockSpec((None, tq, C_WIDTH), lambda b, i: (b, i, 0)),
        out_shape=jax.ShapeDtypeStruct((B, n, C_WIDTH), BF16),
        scratch_shapes=[pltpu.VMEM((C_HEADS, tq, LANES), F32), pltpu.VMEM((C_HEADS, tq, LANES), F32)],
        compiler_params=_cparams(("arbitrary", "arbitrary")),
        name="mla_lat" if has_ctx else "mla_ctx",
    )(*args)


def _dft_tables(n):
    j = np.arange(n, dtype=np.int64)
    ang = ((j[:, None] * j[None, :]) % n) * (2.0 * np.pi / n)
    return np.cos(ang), np.sin(ang)


def _channel_dft():
    c, s = _dft_tables(B_GROUP_DIM)
    eye = np.eye(B_GROUPS)
    return jnp.asarray(np.concatenate([np.kron(eye, c), -np.kron(eye, s)], axis=1), BF16)


def _fourier_ctx_kernel(scale, x_ref, fc_ref, fn_ref, o_ref):
    u = _dot(x_ref[...], fc_ref[...])
    ucat = jnp.concatenate([u[:, :B_WIDTH], u[:, B_WIDTH:]], axis=0).astype(BF16)
    o_ref[...] = (_dot(fn_ref[...], ucat) * scale).astype(BF16)


def _fourier_ctx(fb, fc):
    B, n, _ = fb.shape
    c, s = _dft_tables(n)
    fn = jnp.asarray(np.concatenate([c, s], axis=1), BF16)
    scale = float((n * B_GROUP_DIM) ** -0.5)
    return pl.pallas_call(
        functools.partial(_fourier_ctx_kernel, scale),
        grid=(B,),
        in_specs=[pl.BlockSpec((None, n, B_WIDTH), lambda b: (b, 0, 0)),
                  pl.BlockSpec(fc.shape, lambda b: (0, 0)),
                  pl.BlockSpec(fn.shape, lambda b: (0, 0))],
        out_specs=pl.BlockSpec((None, n, B_WIDTH), lambda b: (b, 0, 0)),
        out_shape=jax.ShapeDtypeStruct((B, n, B_WIDTH), BF16),
        compiler_params=_cparams(("arbitrary",)),
        name="fourier_ctx",
    )(fb, fc, fn)


def _fourier_lat1_kernel(x_ref, fc_ref, g_ref, zr_ref, zi_ref):
    for cc in range(FOURIER_COLS):
        xc = x_ref[:, cc, :].astype(BF16)
        u = _dot(xc, fc_ref[...]).astype(BF16)
        p = _dot(g_ref[cc], u)
        R = p.shape[0] // 2
        zr_ref[cc] = p[:R, :B_WIDTH] - p[R:, B_WIDTH:]
        zi_ref[cc] = p[:R, B_WIDTH:] + p[R:, :B_WIDTH]


def _fourier_lat2_kernel(scale, zr_ref, zi_ref, f_ref, o_ref):
    for kk in range(FOURIER_COLS):
        z = jnp.concatenate([zr_ref[:, kk, :], zi_ref[:, kk, :]], axis=0).astype(BF16)
        o_ref[:, kk, :] = _dot(f_ref[...], z) * scale


def _fourier_lat(fb, fc):
    B, n, _ = fb.shape
    R = n // GRID_W
    W = GRID_W * B_WIDTH
    kr = np.arange(R, dtype=np.int64)
    pos = GRID_W * np.arange(R, dtype=np.int64)[None, None, :] + np.arange(GRID_W, dtype=np.int64)[:, None, None]
    ang = ((kr[None, :, None] * pos) % n) * (2.0 * np.pi / n)
    g = jnp.asarray(np.concatenate([np.cos(ang), -np.sin(ang)], axis=1), BF16)
    c64, s64 = _dft_tables(GRID_W)
    f2 = jnp.asarray(np.concatenate([c64, s64], axis=1), BF16)
    assert R == GRID_W
    nc = GRID_W // FOURIER_COLS
    zshape = jax.ShapeDtypeStruct((B, GRID_W, R, B_WIDTH), F32)
    cols = pl.BlockSpec((None, GRID_W, FOURIER_COLS, B_WIDTH), lambda b, j: (b, 0, j, 0))
    zr, zi = pl.pallas_call(
        _fourier_lat1_kernel,
        grid=(B, nc),
        in_specs=[cols,
                  pl.BlockSpec(fc.shape, lambda b, j: (0, 0)),
                  pl.BlockSpec((FOURIER_COLS, 2 * R, R), lambda b, j: (j, 0, 0))],
        out_specs=[pl.BlockSpec((None, FOURIER_COLS, R, B_WIDTH), lambda b, j: (b, j, 0, 0))] * 2,
        out_shape=[zshape, zshape],
        compiler_params=_cparams(("arbitrary", "arbitrary")),
        name="fourier_lat1",
    )(fb.reshape(B, R, GRID_W, B_WIDTH), fc, g)
    scale = float((n * B_GROUP_DIM) ** -0.5)
    out = pl.pallas_call(
        functools.partial(_fourier_lat2_kernel, scale),
        grid=(B, nc),
        in_specs=[cols, cols, pl.BlockSpec(f2.shape, lambda b, j: (0, 0))],
        out_specs=cols,
        out_shape=jax.ShapeDtypeStruct((B, GRID_W, R, B_WIDTH), F32),
        compiler_params=_cparams(("arbitrary", "arbitrary")),
        name="fourier_lat2",
    )(zr, zi, f2)
    return out.reshape(B, n, B_WIDTH)


def _outproj_kernel(x_ref, a_ref, b_ref, c_ref, mod_ref, g_ref, wout_ref, x1_ref, h2_ref):
    mix = (_dot(a_ref[...], wout_ref[0:A_WIDTH, :])
           + _dot(b_ref[...].astype(BF16), wout_ref[A_WIDTH:A_WIDTH + B_WIDTH, :])
           + _dot(c_ref[...], wout_ref[A_WIDTH + B_WIDTH:, :]))
    gt1 = mod_ref[:, 2 * D_MODEL:3 * D_MODEL]
    sh2 = mod_ref[:, 3 * D_MODEL:4 * D_MODEL]
    sc2 = mod_ref[:, 4 * D_MODEL:5 * D_MODEL]
    x1 = x_ref[...] + gt1 * mix
    x1_ref[...] = x1
    h2_ref[...] = (_rms(x1, g_ref[...], D_MODEL) * (1.0 + sc2) + sh2).astype(BF16)


def _outproj(latent, x, oa, ob, oc, mod_l, g_ffn, wout):
    n = x.shape[0]
    T = T_OUT
    nt = n // T
    tpb = (DEC_SEQ // T) if latent else nt
    base = 1 if latent else 0
    tok = lambda w: pl.BlockSpec((T, w), lambda i: (i, 0))
    return pl.pallas_call(
        _outproj_kernel,
        grid=(nt,),
        in_specs=[tok(D_MODEL), tok(A_WIDTH), tok(B_WIDTH), tok(C_WIDTH),
                  pl.BlockSpec((None, 1, 6 * D_MODEL), lambda i: (base + i // tpb, 0, 0)),
                  pl.BlockSpec(g_ffn.shape, lambda i: (0, 0)),
                  pl.BlockSpec(wout.shape, lambda i: (0, 0))],
        out_specs=[tok(D_MODEL), tok(D_MODEL)],
        out_shape=[jax.ShapeDtypeStruct((n, D_MODEL), F32), jax.ShapeDtypeStruct((n, D_MODEL), BF16)],
        compiler_params=_cparams(("arbitrary",)),
        name="outproj_lat" if latent else "outproj_ctx",
    )(x, oa, ob, oc, mod_l, g_ffn, wout)


def _ffn_kernel(T, S, final, x1_ref, hm_ref, hp_ref, hn_ref, mod_ref, wug_ref, cw_ref, cb_ref, wd_ref, gfin_ref,
                o_ref, hs_scr, u0_scr, u1_scr, act_scr):
    i = pl.program_id(0)
    H = BF16_ROWS
    at_start = (i * T) % S == 0
    at_end = ((i + 1) * T) % S == 0
    hs_scr[0:H] = jnp.where(at_start, jnp.zeros_like(hp_ref[...]), hp_ref[...])
    hs_scr[H:T + H] = hm_ref[...]
    hs_scr[T + H:T + 2 * H] = jnp.where(at_end, jnp.zeros_like(hn_ref[...]), hn_ref[...])
    if S < T:
        pos = lax.broadcasted_iota(jnp.int32, (T, 1), 0) % S
        has_prev = pos != 0
        has_next = pos != S - 1

    n_chunks = D_FF // F_CHUNK
    cols = lambda j: (pl.multiple_of(j * F_CHUNK, F_CHUNK), pl.multiple_of(j * F_CHUNK + D_FF, LANES))

    def up_proj(j, u_ref):
        for b, col in enumerate(cols(j)):
            u_ref[b] = _dot(hs_scr[...], wug_ref[:, pl.ds(col, F_CHUNK)])

    def conv(u_ref, b, col):
        u = u_ref[b]
        rows = u.shape[0]
        up = pltpu.roll(u, 1, 0)[H:T + H]
        un = pltpu.roll(u, rows - 1, 0)[H:T + H]
        if S < T:
            up = jnp.where(has_prev, up, 0.0)
            un = jnp.where(has_next, un, 0.0)
        cw = cw_ref[:, pl.ds(col, F_CHUNK)]
        return (up * cw[0:1, :] + u[H:T + H] * cw[1:2, :] + un * cw[2:3, :]
                + cb_ref[:, pl.ds(col, F_CHUNK)])

    def gate(j, u_ref):
        a, g = [conv(u_ref, b, col) for b, col in enumerate(cols(j))]
        act_scr[:, pl.ds(cols(j)[0], F_CHUNK)] = (g * jax.nn.sigmoid(g) * a).astype(BF16)

    def pair(k, carry):
        up_proj(2 * k + 1, u1_scr)
        gate(2 * k, u0_scr)
        up_proj(2 * k + 2, u0_scr)
        gate(2 * k + 1, u1_scr)
        return carry

    assert n_chunks % 2 == 1
    up_proj(0, u0_scr)
    lax.fori_loop(0, n_chunks // 2, pair, 0)
    gate(n_chunks - 1, u0_scr)
    gt2 = mod_ref[:, 5 * D_MODEL:6 * D_MODEL]
    x2 = x1_ref[...] + gt2 * _dot(act_scr[...], wd_ref[...])
    if final:
        x2 = _rms(x2, gfin_ref[...], D_MODEL)
    o_ref[...] = x2


def _ffn(latent, final, x1, h2, mod_l, wug, conv_w, conv_b, wdown, g_final):
    n = x1.shape[0]
    T = T_FFN
    S = DEC_SEQ if latent else SEQ
    nt = n // T
    tpb = (DEC_SEQ // T) if latent else nt
    base = 1 if latent else 0
    hb = T // BF16_ROWS
    nhb = n // BF16_ROWS
    conv_b2 = conv_b.reshape(1, 2 * D_FF)
    whole = lambda a: pl.BlockSpec(a.shape, lambda i: (0,) * a.ndim, pipeline_mode=pl.Buffered(1))
    return pl.pallas_call(
        functools.partial(_ffn_kernel, T, S, final),
        grid=(nt,),
        in_specs=[
            pl.BlockSpec((T, D_MODEL), lambda i: (i, 0)),
            pl.BlockSpec((T, D_MODEL), lambda i: (i, 0)),
            pl.BlockSpec((BF16_ROWS, D_MODEL), lambda i: (jnp.maximum(i * hb - 1, 0), 0)),
            pl.BlockSpec((BF16_ROWS, D_MODEL), lambda i: (jnp.minimum((i + 1) * hb, nhb - 1), 0)),
            pl.BlockSpec((None, 1, 6 * D_MODEL), lambda i: (base + i // tpb, 0, 0)),
            whole(wug), whole(conv_w), whole(conv_b2), whole(wdown), whole(g_final),
        ],
        out_specs=pl.BlockSpec((T, D_MODEL), lambda i: (i, 0)),
        out_shape=jax.ShapeDtypeStruct((n, D_MODEL), F32),
        scratch_shapes=[pltpu.VMEM((T + 2 * BF16_ROWS, D_MODEL), BF16),
                        pltpu.VMEM((2, T + 2 * BF16_ROWS, F_CHUNK), F32),
                        pltpu.VMEM((2, T + 2 * BF16_ROWS, F_CHUNK), F32),
                        pltpu.VMEM((T, D_FF), BF16)],
        compiler_params=_cparams(("arbitrary",)),
        name="ffn_lat" if latent else "ffn_ctx",
    )(x1, h2, h2, h2, mod_l, wug, conv_w, conv_b2, wdown, g_final)


def _rope_tables(n_tok, dim, lane0, width):
    rows = n_tok // GRID_W
    r = np.repeat(np.arange(rows), GRID_W).astype(np.float64)
    col = np.tile(np.arange(GRID_W), rows).astype(np.float64)
    quarter = dim // 4
    inv = ROPE_BASE ** (-np.arange(quarter, dtype=np.float64) / quarter)
    ang_r = r[:, None] * inv
    ang_c = col[:, None] * inv
    cos = np.concatenate([np.cos(ang_r)] * 2 + [np.cos(ang_c)] * 2, axis=1)
    sin = np.concatenate([-np.sin(ang_r), np.sin(ang_r), -np.sin(ang_c), np.sin(ang_c)], axis=1)
    reps = width // dim
    cos = np.concatenate([cos] * reps, axis=1)
    sin = np.concatenate([sin] * reps, axis=1)
    pad = ((0, 0), (lane0, LANES - lane0 - width))
    return jnp.asarray(np.pad(cos, pad, constant_values=1.0), F32), jnp.asarray(np.pad(sin, pad), F32)


def _layer_weights(w_in, g_cq, w_uq, w_ukv):
    z = lambda r, c: jnp.zeros((r, c), F32)
    o = np.cumsum([0, A_WIDTH, A_KV_WIDTH, A_KV_WIDTH, B_WIDTH, C_Q_LORA, C_KV_LORA, C_ROPE])
    win_p = jnp.concatenate([
        w_in[:, o[0]:o[1]].reshape(D_MODEL, A_HEADS, HEAD_DIM)[:, A_HEAD_ORDER, :].reshape(D_MODEL, A_WIDTH),
        w_in[:, o[1]:o[4]],
        w_in[:, o[4]:o[5]], z(D_MODEL, CQ_PAD - C_Q_LORA),
        w_in[:, o[5]:o[6]],
        z(D_MODEL, KR_LANE0), w_in[:, o[6]:o[7]], z(D_MODEL, LANES - KR_LANE0 - C_ROPE),
    ], axis=1).astype(BF16)
    gcq_p = jnp.pad(g_cq, (0, CQ_PAD - C_Q_LORA)).reshape(1, CQ_PAD)
    hq = C_NOPE + C_ROPE
    wuq_h = w_uq.reshape(C_Q_LORA, C_HEADS, hq)
    wuq_p = jnp.pad(wuq_h, ((0, CQ_PAD - C_Q_LORA), (0, 0), (0, C_HEAD_PAD - hq)))
    wuq_p = wuq_p.reshape(CQ_PAD, C_HEADS * C_HEAD_PAD).astype(BF16)
    wukv_h = w_ukv.reshape(C_KV_LORA, C_HEADS, C_NOPE + C_V)
    wk = jnp.pad(wukv_h[:, :, :C_NOPE], ((0, 0), (0, 0), (0, C_HEAD_PAD - C_NOPE)))
    zv = jnp.zeros((C_KV_LORA, C_HEAD_PAD - C_V), F32)
    wv = [wukv_h[:, h, C_NOPE:] for h in range(C_HEADS)]
    wv = jnp.concatenate([jnp.concatenate([zv, w] if h % 2 else [w, zv], axis=1) for h, w in enumerate(wv)], axis=1)
    wukv_p = jnp.concatenate([wk.reshape(C_KV_LORA, CW), wv], axis=1).astype(BF16)
    return win_p, gcq_p, wuq_p, wukv_p


def kernel(x_prompt, x_sample, cache_win_k, cache_win_v, cache_mla_ckv, cache_mla_krope, c, c_ctx,
           w_ada, b_ada, g_mix, w_in, sink, g_cq, w_uq, g_ckv, w_ukv, w_out, g_ffn, w_ug, conv_w,
           conv_b, w_down, g_final):
    n_ctx = BATCH * SEQ
    n_lat = DEC_BATCH * DEC_SEQ
    xp = x_prompt.reshape(n_ctx, D_MODEL)
    xs = x_sample.reshape(n_lat, D_MODEL)

    cvecs = jnp.concatenate([c_ctx[None, :], c, jnp.zeros((8 - 1 - DEC_BATCH, D_MODEL), F32)], axis=0)
    mod = _modulation(cvecs, w_ada, b_ada)

    cos_a, sin_a = _rope_tables(DEC_SEQ, HEAD_DIM, 0, LANES)
    cos_c, sin_c = _rope_tables(DEC_SEQ, C_ROPE, KR_LANE0, C_ROPE)
    tables = (cos_a, sin_a, cos_c, sin_c)
    fc = _channel_dft()
    g_final2 = g_final.reshape(1, D_MODEL)

    new_k, new_v, new_ckv, new_kr = [], [], [], []
    for l in range(DEPTH):
        win_p, gcq_p, wuq_p, wukv_p = _layer_weights(w_in[l], g_cq[l], w_uq[l], w_ukv[l])
        mod_l = mod[l].reshape(8, 1, 6 * D_MODEL)
        g_mix_l = g_mix[l].reshape(1, D_MODEL)
        g_ckv_l = g_ckv[l].reshape(1, C_KV_LORA)
        g_ffn_l = g_ffn[l].reshape(1, D_MODEL)
        sink_l = sink[l].reshape(1, A_HEADS)
        wout_a = w_out[l, :A_WIDTH].reshape(A_HEADS, HEAD_DIM, D_MODEL)[jnp.array(A_HEAD_ORDER)]
        wout_l = jnp.concatenate([wout_a.reshape(A_WIDTH, D_MODEL), w_out[l, A_WIDTH:]], axis=0).astype(BF16)
        wug_l = w_ug[l].astype(BF16)
        wdown_l = w_down[l].astype(BF16)
        final = l == DEPTH - 1

        (qa, ka, va, fb, qc, kc, vc, ska, sva, sckv, skr) = _projection(
            False, xp, mod_l, g_mix_l, win_p, gcq_p, wuq_p, g_ckv_l, wukv_p, None)
        r3 = lambda a: a.reshape(BATCH, SEQ, a.shape[-1])
        oa = _attn_a_ctx(sink_l, r3(qa), r3(ka), r3(va))
        ob = _fourier_ctx(r3(fb), fc)
        oc = _mla(r3(qc), r3(kc), r3(vc), None, None, SEQ, SEQ)
        f2 = lambda a: a.reshape(n_ctx, a.shape[-1])
        x1, h2 = _outproj(False, xp, f2(oa), f2(ob), f2(oc), mod_l, g_ffn_l, wout_l)
        xp = _ffn(False, final, x1, h2, mod_l, wug_l, conv_w[l], conv_b[l], wdown_l, g_final2)
        new_k.append(ska.reshape(BATCH, SEQ, A_KV_HEADS, HEAD_DIM))
        new_v.append(sva.reshape(BATCH, SEQ, A_KV_HEADS, HEAD_DIM))
        new_ckv.append(sckv.reshape(BATCH, SEQ, C_KV_LORA))
        new_kr.append(skr.reshape(BATCH, SEQ, C_ROPE))

        (qa, ka, va, fb, qc, kc, vc) = _projection(
            True, xs, mod_l, g_mix_l, win_p, gcq_p, wuq_p, g_ckv_l, wukv_p, tables)
        r3 = lambda a: a.reshape(DEC_BATCH, DEC_SEQ, a.shape[-1])
        kwin = cache_win_k[:, l].reshape(DEC_BATCH, PAST_LEN, A_KV_WIDTH).astype(BF16)
        vwin = cache_win_v[:, l].reshape(DEC_BATCH, PAST_LEN, A_KV_WIDTH).astype(BF16)
        oa = _attn_a_lat(sink_l, r3(qa), r3(ka), r3(va), kwin, vwin)
        ob = _fourier_lat(r3(fb), fc)
        kr_pad = jnp.pad(cache_mla_krope[:, l], ((0, 0), (0, 0), (KR_LANE0, LANES - KR_LANE0 - C_ROPE)))
        kc_ctx, vc_ctx = _kvcache(cache_mla_ckv[:, l], kr_pad, wukv_p)
        oc = _mla(r3(qc), r3(kc), r3(vc), kc_ctx, vc_ctx, TQ_MLA, TK_MLA)
        f2 = lambda a: a.reshape(n_lat, a.shape[-1])
        x1, h2 = _outproj(True, xs, f2(oa), f2(ob), f2(oc), mod_l, g_ffn_l, wout_l)
        xs = _ffn(True, final, x1, h2, mod_l, wug_l, conv_w[l], conv_b[l], wdown_l, g_final2)

    y_prompt = xp.reshape(BATCH, SEQ, D_MODEL)
    y_sample = xs.reshape(DEC_BATCH, DEC_SEQ, D_MODEL)
    return (y_prompt, y_sample, jnp.stack(new_k, axis=1), jnp.stack(new_v, axis=1),
            jnp.stack(new_ckv, axis=1), jnp.stack(new_kr, axis=1))
```

```python
import functools

import numpy as np
import jax
import jax.numpy as jnp
from jax import lax
from jax.experimental import pallas as pl
from jax.experimental.pallas import tpu as pltpu

F32 = jnp.float32
BF16 = jnp.bfloat16

D_MODEL = 1024
BATCH = 16
SEQ = 256
DEPTH = 2
DEC_BATCH = 4
DEC_SEQ = 4096
PAST_LEN = 256
GRID_W = 64
HEAD_DIM = 64
A_HEADS = 8
A_KV_HEADS = 2
A_GROUP = 4
A_WIDTH = 512
A_KV_WIDTH = 128
WINDOW = 128
BLOCK = 128
B_WIDTH = 256
B_GROUP_DIM = 64
B_GROUPS = 4
C_HEADS = 4
C_NOPE = 64
C_ROPE = 32
C_V = 64
C_Q_LORA = 192
C_KV_LORA = 128
C_WIDTH = 256
D_FF = 2816
ROPE_BASE = 10000.0
EPS = 1e-6
NEG = -1e30

LANES = 128
BF16_ROWS = 16
C_HEAD_PAD = 128
CW = C_HEADS * C_HEAD_PAD
CQ_PAD = 256
VMEM_LIMIT = 56 * 1024 * 1024

_QA0, _KA0, _VA0, _FB0, _CQ0, _CKV0, _KR0, _IN_PAD = 0, 512, 640, 768, 1024, 1280, 1408, 1536
KR_LANE0 = C_NOPE

T_PROJ = 1024
T_OUT = 1024
T_FFN = 1024
F_CHUNK = 256
TQ_MLA = 1024
TK_MLA = 2048
A_SUB = 4
A_HEAD_ORDER = [h for j in range(A_GROUP) for h in (j, A_GROUP + j)]
LOG2_E = float(np.log2(np.e))
FOURIER_COLS = 8


def _cparams(sem):
    return pltpu.CompilerParams(dimension_semantics=sem, vmem_limit_bytes=VMEM_LIMIT)


def _dot(a, b):
    return jnp.dot(a, b, preferred_element_type=F32)


def _dot_nt(a, b):
    return lax.dot_general(a, b, (((1,), (1,)), ((), ())), preferred_element_type=F32)


def _rms(x, g, n):
    ms = jnp.sum(x * x, axis=-1, keepdims=True) * (1.0 / n)
    return x * lax.rsqrt(ms + EPS) * g


def _mod_kernel(c_ref, w_ref, b_ref, o_ref):
    cv = c_ref[...]
    s = cv * jax.nn.sigmoid(cv)
    o_ref[...] = jnp.dot(s, w_ref[...], preferred_element_type=F32,
                         precision=lax.Precision.HIGHEST) + b_ref[...]


def _modulation(cvecs, w_ada, b_ada):
    nj = 6
    return pl.pallas_call(
        _mod_kernel,
        grid=(DEPTH, nj),
        in_specs=[
            pl.BlockSpec((8, D_MODEL), lambda l, j: (0, 0)),
            pl.BlockSpec((None, D_MODEL, D_MODEL), lambda l, j: (l, 0, j)),
            pl.BlockSpec((None, 1, D_MODEL), lambda l, j: (l, 0, j)),
        ],
        out_specs=pl.BlockSpec((None, 8, D_MODEL), lambda l, j: (l, 0, j)),
        out_shape=jax.ShapeDtypeStruct((DEPTH, 8, 6 * D_MODEL), F32),
        compiler_params=_cparams(("arbitrary", "arbitrary")),
        name="modulation",
    )(cvecs, w_ada, b_ada.reshape(DEPTH, 1, 6 * D_MODEL))


def _rope_block(x, cos, sin, half):
    lane = lax.broadcasted_iota(jnp.int32, x.shape, 1)
    first = (lane % (2 * half)) < half
    partner = jnp.where(first, pltpu.roll(x, LANES - half, 1), pltpu.roll(x, half, 1))
    return x * cos + partner * sin


def _rope(x, cos, sin, half):
    blocks = [_rope_block(x[:, j:j + LANES], cos, sin, half) for j in range(0, x.shape[1], LANES)]
    return blocks[0] if len(blocks) == 1 else jnp.concatenate(blocks, axis=1)


def _proj_kernel(latent, x_ref, mod_ref, g_ref, win_ref, gcq_ref, wuq_ref, gckv_ref, wukv_ref, *rest):
    if latent:
        cosa_ref, sina_ref, cosc_ref, sinc_ref = rest[:4]
        qa_ref, ka_ref, va_ref, fb_ref, qc_ref, kc_ref, vc_ref = rest[4:]
    else:
        qa_ref, ka_ref, va_ref, fb_ref, qc_ref, kc_ref, vc_ref, ska_ref, sva_ref, sckv_ref, skr_ref = rest
    x = x_ref[...]
    y = _rms(x, g_ref[...], D_MODEL)
    sh1 = mod_ref[:, 0:D_MODEL]
    sc1 = mod_ref[:, D_MODEL:2 * D_MODEL]
    h = (y * (1.0 + sc1) + sh1).astype(BF16)

    qa = _dot(h, win_ref[:, _QA0:_KA0])
    kva = _dot(h, win_ref[:, _KA0:_FB0])
    ka, va = kva[:, :A_KV_WIDTH], kva[:, A_KV_WIDTH:]
    fb = _dot(h, win_ref[:, _FB0:_CQ0])
    cq = _dot(h, win_ref[:, _CQ0:_CKV0])
    ckr = _dot(h, win_ref[:, _CKV0:_IN_PAD])
    ckv, kr = ckr[:, :C_KV_LORA], ckr[:, C_KV_LORA:]

    if not latent:
        ska_ref[...] = ka
        sva_ref[...] = va
        skr_ref[...] = kr[:, KR_LANE0:KR_LANE0 + C_ROPE]
    else:
        qa = _rope(qa, cosa_ref[...], sina_ref[...], HEAD_DIM // 4)
        ka = _rope(ka, cosa_ref[...], sina_ref[...], HEAD_DIM // 4)
        kr = _rope(kr, cosc_ref[...], sinc_ref[...], C_ROPE // 4)
    qa_ref[...] = (qa * (LOG2_E * HEAD_DIM ** -0.5)).astype(BF16)
    ka_ref[...] = ka.astype(BF16)
    va_ref[...] = va.astype(BF16)
    fb_ref[...] = fb.astype(fb_ref.dtype)

    cqn = _rms(cq, gcq_ref[...], C_Q_LORA).astype(BF16)
    qc = _dot(cqn, wuq_ref[...])
    if latent:
        qc = _rope(qc, cosc_ref[...], sinc_ref[...], C_ROPE // 4)
    qc_ref[...] = (qc * (LOG2_E * (C_NOPE + C_ROPE) ** -0.5)).astype(BF16)

    ckvn = _rms(ckv, gckv_ref[...], C_KV_LORA)
    if not latent:
        sckv_ref[...] = ckvn
    _store_kv(_dot(ckvn.astype(BF16), wukv_ref[...]), kr, kc_ref, vc_ref)


def _store_kv(kv, kr, kc_ref, vc_ref):
    kc_ref[...] = (kv[:, :CW] + jnp.concatenate([kr] * C_HEADS, axis=1)).astype(BF16)
    lane = lax.broadcasted_iota(jnp.int32, (1, CW), 1)
    upper = lane % C_HEAD_PAD >= C_V
    odd = (lane // C_HEAD_PAD) % 2 == 1
    vc_ref[...] = (kv[:, CW:] + jnp.where(upper != odd, 1.0, 0.0).astype(F32)).astype(BF16)


def _projection(latent, x, mod_l, g_mix, win_p, gcq_p, wuq_p, gckv, wukv_p, tables):
    n = x.shape[0]
    T = T_PROJ
    nt = n // T
    tpb = (DEC_SEQ // T) if latent else nt
    base = 1 if latent else 0
    tok = lambda w: pl.BlockSpec((T, w), lambda i: (i, 0))
    full = lambda a: pl.BlockSpec(a.shape, lambda i: (0,) * a.ndim)
    in_specs = [
        tok(D_MODEL),
        pl.BlockSpec((None, 1, 6 * D_MODEL), lambda i: (base + i // tpb, 0, 0)),
        full(g_mix), full(win_p), full(gcq_p), full(wuq_p), full(gckv), full(wukv_p),
    ]
    args = [x, mod_l, g_mix, win_p, gcq_p, wuq_p, gckv, wukv_p]
    if latent:
        in_specs += [pl.BlockSpec((T, LANES), lambda i: (i % tpb, 0))] * 4
        args += list(tables)
    widths = [A_WIDTH, A_KV_WIDTH, A_KV_WIDTH, B_WIDTH, CW, CW, CW]
    out_specs = [tok(w) for w in widths]
    out_shape = [jax.ShapeDtypeStruct((n, w), BF16) for w in widths]
    if latent:
        out_shape[3] = jax.ShapeDtypeStruct((n, B_WIDTH), F32)
    if not latent:
        sw = [A_KV_WIDTH, A_KV_WIDTH, C_KV_LORA, C_ROPE]
        out_specs += [tok(w) for w in sw]
        out_shape += [jax.ShapeDtypeStruct((n, w), F32) for w in sw]
    return pl.pallas_call(
        functools.partial(_proj_kernel, latent),
        grid=(nt,),
        in_specs=in_specs,
        out_specs=out_specs,
        out_shape=out_shape,
        compiler_params=_cparams(("arbitrary",)),
        name="proj_lat" if latent else "proj_ctx",
    )(*args)


def _kvcache_kernel(ckv_ref, kr_ref, wukv_ref, kc_ref, vc_ref):
    _store_kv(_dot(ckv_ref[...].astype(BF16), wukv_ref[...]), kr_ref[...], kc_ref, vc_ref)


def _kvcache(ckv, kr_pad, wukv_p):
    B, P, _ = ckv.shape
    return pl.pallas_call(
        _kvcache_kernel,
        grid=(B,),
        in_specs=[
            pl.BlockSpec((None, P, C_KV_LORA), lambda b: (b, 0, 0)),
            pl.BlockSpec((None, P, LANES), lambda b: (b, 0, 0)),
            pl.BlockSpec(wukv_p.shape, lambda b: (0, 0)),
        ],
        out_specs=[pl.BlockSpec((None, P, CW), lambda b: (b, 0, 0))] * 2,
        out_shape=[jax.ShapeDtypeStruct((B, P, CW), BF16)] * 2,
        compiler_params=_cparams(("arbitrary",)),
        name="kvcache",
    )(ckv, kr_pad, wukv_p)


def _gqa_group(g, q_blocks, keys, masks, values, sink_ref):
    tq = q_blocks[0].shape[0]
    lane = lax.broadcasted_iota(jnp.int32, (tq, LANES), 1)
    mine = (lane >= HEAD_DIM) if g else (lane < HEAD_DIM)
    qs = jnp.concatenate([jnp.where(mine, qb, jnp.zeros_like(qb)) for qb in q_blocks], axis=0)
    parts = []
    for k, mask in zip(keys, masks):
        s = _dot_nt(qs, k)
        parts.append(s if mask is None else jnp.where(mask, s, NEG))
    s = parts[0] if len(parts) == 1 else jnp.concatenate(parts, axis=1)
    rowblk = lax.broadcasted_iota(jnp.int32, (A_GROUP * tq, 1), 0) // tq
    sk = jnp.full((A_GROUP * tq, 1), sink_ref[0, g * A_GROUP] * LOG2_E, F32)
    for j in range(1, A_GROUP):
        sk = jnp.where(rowblk == j, sink_ref[0, g * A_GROUP + j] * LOG2_E, sk)
    m = jnp.maximum(jnp.max(s, axis=-1, keepdims=True), sk)
    p = jnp.exp2(s - m)
    l = jnp.sum(p, axis=-1, keepdims=True) + jnp.exp2(sk - m)
    v = values[0] if len(values) == 1 else jnp.concatenate(values, axis=0)
    return _dot(p.astype(BF16), v) * (1.0 / l)


def _gqa_store(o_ref, rows, tq, o0, o1):
    low = lax.broadcasted_iota(jnp.int32, (tq, LANES), 1) < HEAD_DIM
    for j in range(A_GROUP):
        blk = jnp.where(low, o0[j * tq:(j + 1) * tq], o1[j * tq:(j + 1) * tq])
        o_ref[rows, j * LANES:(j + 1) * LANES] = blk.astype(BF16)


def _attn_a_ctx_kernel(sink_ref, q_ref, k_ref, v_ref, o_ref):
    n = q_ref.shape[0]
    q_blocks = [q_ref[:, j * LANES:(j + 1) * LANES] for j in range(A_GROUP)]
    outs = [_gqa_group(g, q_blocks, [k_ref[...]], [None], [v_ref[...]], sink_ref) for g in range(A_KV_HEADS)]
    _gqa_store(o_ref, slice(None), n, *outs)


def _attn_a_ctx(sink, q, k, v):
    B, n, _ = q.shape
    blk = lambda w: pl.BlockSpec((None, n, w), lambda b: (b, 0, 0))
    return pl.pallas_call(
        _attn_a_ctx_kernel,
        grid=(B,),
        in_specs=[pl.BlockSpec(memory_space=pltpu.SMEM), blk(A_WIDTH), blk(A_KV_WIDTH), blk(A_KV_WIDTH)],
        out_specs=blk(A_WIDTH),
        out_shape=jax.ShapeDtypeStruct((B, n, A_WIDTH), BF16),
        compiler_params=_cparams(("arbitrary",)),
        name="attn_a_ctx",
    )(sink, q, k, v)


def _attn_a_lat_kernel(sink_ref, q_ref, k_ref, v_ref, kc_ref, vc_ref, o_ref):
    n = k_ref.shape[0]
    band = 3 * BLOCK
    kcx = kc_ref[...]
    vcx = vc_ref[...]

    def body(sb, carry):
        blk = pl.program_id(1) * A_SUB + sb
        rows = pl.ds(pl.multiple_of(sb * BLOCK, BLOCK), BLOCK)
        start = pl.multiple_of(jnp.clip((blk - 1) * BLOCK, 0, n - band), BLOCK)
        kb = k_ref[pl.ds(start, band), :]
        vb = v_ref[pl.ds(start, band), :]
        qpos = blk * BLOCK + lax.broadcasted_iota(jnp.int32, (BLOCK, band), 0)
        kpos = start + lax.broadcasted_iota(jnp.int32, (BLOCK, band), 1)
        mask = jnp.concatenate([jnp.abs(qpos - kpos) <= WINDOW] * A_GROUP, axis=0)
        q_blocks = [q_ref[rows, j * LANES:(j + 1) * LANES] for j in range(A_GROUP)]
        outs = [_gqa_group(g, q_blocks, [kb, kcx], [mask, None], [vb, vcx], sink_ref)
                for g in range(A_KV_HEADS)]
        _gqa_store(o_ref, rows, BLOCK, *outs)
        return carry

    lax.fori_loop(0, A_SUB, body, 0, unroll=True)


def _attn_a_lat(sink, q, k, v, k_ctx, v_ctx):
    B, n, _ = q.shape
    P = k_ctx.shape[1]
    tq = A_SUB * BLOCK
    seq = lambda w, m: pl.BlockSpec((None, m, w), lambda b, i: (b, 0, 0))
    return pl.pallas_call(
        _attn_a_lat_kernel,
        grid=(B, n // tq),
        in_specs=[pl.BlockSpec(memory_space=pltpu.SMEM),
                  pl.BlockSpec((None, tq, A_WIDTH), lambda b, i: (b, i, 0)),
                  seq(A_KV_WIDTH, n), seq(A_KV_WIDTH, n), seq(A_KV_WIDTH, P), seq(A_KV_WIDTH, P)],
        out_specs=pl.BlockSpec((None, tq, A_WIDTH), lambda b, i: (b, i, 0)),
        out_shape=jax.ShapeDtypeStruct((B, n, A_WIDTH), BF16),
        compiler_params=_cparams(("arbitrary", "arbitrary")),
        name="attn_a_lat",
    )(sink, q, k, v, k_ctx, v_ctx)


def _mla_step(q_ref, k, v, m_scr, acc_scr):
    reps = k.shape[0] // LANES
    for h in range(C_HEADS):
        hs = slice(h * C_HEAD_PAD, (h + 1) * C_HEAD_PAD)
        s = _dot_nt(q_ref[:, hs], k[:, hs])
        m_old = m_scr[h]
        m_new = jnp.maximum(m_old, jnp.max(s, axis=-1, keepdims=True))
        alpha = jnp.exp2(m_old - m_new)
        p = jnp.exp2(s - jnp.concatenate([m_new] * reps, axis=1))
        m_scr[h] = m_new
        acc_scr[h] = acc_scr[h] * alpha + _dot(p.astype(BF16), v[:, hs])


def _mla_kernel(n_chunks, tk, has_ctx, q_ref, k_ref, v_ref, *rest):
    if has_ctx:
        kc_ref, vc_ref, o_ref, m_scr, acc_scr = rest
    else:
        o_ref, m_scr, acc_scr = rest
    tq = q_ref.shape[0]
    m_scr[...] = jnp.full(m_scr.shape, -jnp.inf, F32)
    acc_scr[...] = jnp.zeros(acc_scr.shape, F32)

    def body(c, carry):
        ks = pl.multiple_of(c * tk, tk)
        _mla_step(q_ref, k_ref[pl.ds(ks, tk), :], v_ref[pl.ds(ks, tk), :], m_scr, acc_scr)
        return carry

    if has_ctx:
        _mla_step(q_ref, kc_ref[...], vc_ref[...], m_scr, acc_scr)
    lax.fori_loop(0, n_chunks, body, 0)

    low = lax.broadcasted_iota(jnp.int32, (tq, LANES), 1) < C_V
    for j in range(C_HEADS // 2):
        even, odd = acc_scr[2 * j], acc_scr[2 * j + 1]
        sums = pltpu.roll(jnp.where(low, odd, even), C_V, 1)
        o_ref[:, j * LANES:(j + 1) * LANES] = (jnp.where(low, even, odd) / sums).astype(BF16)


def _mla(q, k, v, k_ctx, v_ctx, tq, tk):
    B, n, _ = q.shape
    nk = k.shape[1]
    has_ctx = k_ctx is not None
    seq = lambda m: pl.BlockSpec((None, m, CW), lambda b, i: (b, 0, 0))
    in_specs = [pl.BlockSpec((None, tq, CW), lambda b, i: (b, i, 0)), seq(nk), seq(nk)]
    args = [q, k, v]
    if has_ctx:
        in_specs += [seq(k_ctx.shape[1])] * 2
        args += [k_ctx, v_ctx]
    return pl.pallas_call(
        functools.partial(_mla_kernel, nk // tk, tk, has_ctx),
        grid=(B, n // tq),
        in_specs=in_specs,
        out_specs=pl.BlockSpec((None, tq, C_WIDTH), lambda b, i: (b, i, 0)),
        out_shape=jax.ShapeDtypeStruct((B, n, C_WIDTH), BF16),
        scratch_shapes=[pltpu.VMEM((C_HEADS, tq, LANES), F32), pltpu.VMEM((C_HEADS, tq, LANES), F32)],
        compiler_params=_cparams(("arbitrary", "arbitrary")),
        name="mla_lat" if has_ctx else "mla_ctx",
    )(*args)


def _dft_tables(n):
    j = np.arange(n, dtype=np.int64)
    ang = ((j[:, None] * j[None, :]) % n) * (2.0 * np.pi / n)
    return np.cos(ang), np.sin(ang)


def _channel_dft():
    c, s = _dft_tables(B_GROUP_DIM)
    eye = np.eye(B_GROUPS)
    return _table_bf16(np.concatenate([np.kron(eye, c), -np.kron(eye, s)], axis=1))


def _table_bf16(t):
    return jnp.asarray(t, F32).astype(BF16)


def _fourier_ctx_kernel(scale, x_ref, fc_ref, fn_ref, o_ref):
    u = _dot(x_ref[...], fc_ref[...])
    ucat = jnp.concatenate([u[:, :B_WIDTH], u[:, B_WIDTH:]], axis=0).astype(BF16)
    o_ref[...] = (_dot(fn_ref[...], ucat) * scale).astype(BF16)


def _fourier_ctx(fb, fc):
    B, n, _ = fb.shape
    c, s = _dft_tables(n)
    fn = _table_bf16(np.concatenate([c, s], axis=1))
    scale = float((n * B_GROUP_DIM) ** -0.5)
    return pl.pallas_call(
        functools.partial(_fourier_ctx_kernel, scale),
        grid=(B,),
        in_specs=[pl.BlockSpec((None, n, B_WIDTH), lambda b: (b, 0, 0)),
                  pl.BlockSpec(fc.shape, lambda b: (0, 0)),
                  pl.BlockSpec(fn.shape, lambda b: (0, 0))],
        out_specs=pl.BlockSpec((None, n, B_WIDTH), lambda b: (b, 0, 0)),
        out_shape=jax.ShapeDtypeStruct((B, n, B_WIDTH), BF16),
        compiler_params=_cparams(("arbitrary",)),
        name="fourier_ctx",
    )(fb, fc, fn)


def _fourier_lat1_kernel(x_ref, fc_ref, g_ref, zr_ref, zi_ref):
    for cc in range(FOURIER_COLS):
        xc = x_ref[:, cc, :].astype(BF16)
        u = _dot(xc, fc_ref[...]).astype(BF16)
        p = _dot(g_ref[cc], u)
        R = p.shape[0] // 2
        zr_ref[cc] = p[:R, :B_WIDTH] - p[R:, B_WIDTH:]
        zi_ref[cc] = p[:R, B_WIDTH:] + p[R:, :B_WIDTH]


def _fourier_lat2_kernel(scale, zr_ref, zi_ref, f_ref, o_ref):
    for kk in range(FOURIER_COLS):
        z = jnp.concatenate([zr_ref[:, kk, :], zi_ref[:, kk, :]], axis=0).astype(BF16)
        o_ref[:, kk, :] = _dot(f_ref[...], z) * scale


def _fourier_lat(fb, fc):
    B, n, _ = fb.shape
    R = n // GRID_W
    W = GRID_W * B_WIDTH
    kr = np.arange(R, dtype=np.int64)
    pos = GRID_W * np.arange(R, dtype=np.int64)[None, None, :] + np.arange(GRID_W, dtype=np.int64)[:, None, None]
    ang = ((kr[None, :, None] * pos) % n) * (2.0 * np.pi / n)
    g = _table_bf16(np.concatenate([np.cos(ang), -np.sin(ang)], axis=1))
    c64, s64 = _dft_tables(GRID_W)
    f2 = _table_bf16(np.concatenate([c64, s64], axis=1))
    assert R == GRID_W
    nc = GRID_W // FOURIER_COLS
    zshape = jax.ShapeDtypeStruct((B, GRID_W, R, B_WIDTH), F32)
    cols = pl.BlockSpec((None, GRID_W, FOURIER_COLS, B_WIDTH), lambda b, j: (b, 0, j, 0))
    zr, zi = pl.pallas_call(
        _fourier_lat1_kernel,
        grid=(B, nc),
        in_specs=[cols,
                  pl.BlockSpec(fc.shape, lambda b, j: (0, 0)),
                  pl.BlockSpec((FOURIER_COLS, 2 * R, R), lambda b, j: (j, 0, 0))],
        out_specs=[pl.BlockSpec((None, FOURIER_COLS, R, B_WIDTH), lambda b, j: (b, j, 0, 0))] * 2,
        out_shape=[zshape, zshape],
        compiler_params=_cparams(("arbitrary", "arbitrary")),
        name="fourier_lat1",
    )(fb.reshape(B, R, GRID_W, B_WIDTH), fc, g)
    scale = float((n * B_GROUP_DIM) ** -0.5)
    out = pl.pallas_call(
        functools.partial(_fourier_lat2_kernel, scale),
        grid=(B, nc),
        in_specs=[cols, cols, pl.BlockSpec(f2.shape, lambda b, j: (0, 0))],
        out_specs=cols,
        out_shape=jax.ShapeDtypeStruct((B, GRID_W, R, B_WIDTH), F32),
        compiler_params=_cparams(("arbitrary", "arbitrary")),
        name="fourier_lat2",
    )(zr, zi, f2)
    return out.reshape(B, n, B_WIDTH)


def _outproj_kernel(x_ref, a_ref, b_ref, c_ref, mod_ref, g_ref, wout_ref, x1_ref, h2_ref):
    mix = (_dot(a_ref[...], wout_ref[0:A_WIDTH, :])
           + _dot(b_ref[...].astype(BF16), wout_ref[A_WIDTH:A_WIDTH + B_WIDTH, :])
           + _dot(c_ref[...], wout_ref[A_WIDTH + B_WIDTH:, :]))
    gt1 = mod_ref[:, 2 * D_MODEL:3 * D_MODEL]
    sh2 = mod_ref[:, 3 * D_MODEL:4 * D_MODEL]
    sc2 = mod_ref[:, 4 * D_MODEL:5 * D_MODEL]
    x1 = x_ref[...] + gt1 * mix
    x1_ref[...] = x1
    h2_ref[...] = (_rms(x1, g_ref[...], D_MODEL) * (1.0 + sc2) + sh2).astype(BF16)


def _outproj(latent, x, oa, ob, oc, mod_l, g_ffn, wout):
    n = x.shape[0]
    T = T_OUT
    nt = n // T
    tpb = (DEC_SEQ // T) if latent else nt
    base = 1 if latent else 0
    tok = lambda w: pl.BlockSpec((T, w), lambda i: (i, 0))
    return pl.pallas_call(
        _outproj_kernel,
        grid=(nt,),
        in_specs=[tok(D_MODEL), tok(A_WIDTH), tok(B_WIDTH), tok(C_WIDTH),
                  pl.BlockSpec((None, 1, 6 * D_MODEL), lambda i: (base + i // tpb, 0, 0)),
                  pl.BlockSpec(g_ffn.shape, lambda i: (0, 0)),
                  pl.BlockSpec(wout.shape, lambda i: (0, 0))],
        out_specs=[tok(D_MODEL), tok(D_MODEL)],
        out_shape=[jax.ShapeDtypeStruct((n, D_MODEL), F32), jax.ShapeDtypeStruct((n, D_MODEL), BF16)],
        compiler_params=_cparams(("arbitrary",)),
        name="outproj_lat" if latent else "outproj_ctx",
    )(x, oa, ob, oc, mod_l, g_ffn, wout)


def _ffn_kernel(T, S, final, x1_ref, hm_ref, hp_ref, hn_ref, mod_ref, wug_ref, cw_ref, cb_ref, wd_ref, gfin_ref,
                o_ref, hs_scr, u0_scr, u1_scr, act_scr):
    i = pl.program_id(0)
    H = BF16_ROWS
    at_start = (i * T) % S == 0
    at_end = ((i + 1) * T) % S == 0
    hs_scr[0:H] = jnp.where(at_start, jnp.zeros_like(hp_ref[...]), hp_ref[...])
    hs_scr[H:T + H] = hm_ref[...]
    hs_scr[T + H:T + 2 * H] = jnp.where(at_end, jnp.zeros_like(hn_ref[...]), hn_ref[...])
    if S < T:
        pos = lax.broadcasted_iota(jnp.int32, (T, 1), 0) % S
        has_prev = pos != 0
        has_next = pos != S - 1

    n_chunks = D_FF // F_CHUNK
    cols = lambda j: (pl.multiple_of(j * F_CHUNK, F_CHUNK), pl.multiple_of(j * F_CHUNK + D_FF, LANES))

    def up_proj(j, u_ref):
        for b, col in enumerate(cols(j)):
            u_ref[b] = _dot(hs_scr[...], wug_ref[:, pl.ds(col, F_CHUNK)])

    def conv(u_ref, b, col):
        u = u_ref[b]
        rows = u.shape[0]
        up = pltpu.roll(u, 1, 0)[H:T + H]
        un = pltpu.roll(u, rows - 1, 0)[H:T + H]
        if S < T:
            up = jnp.where(has_prev, up, 0.0)
            un = jnp.where(has_next, un, 0.0)
        cw = cw_ref[:, pl.ds(col, F_CHUNK)]
        return (up * cw[0:1, :] + u[H:T + H] * cw[1:2, :] + un * cw[2:3, :]
                + cb_ref[:, pl.ds(col, F_CHUNK)])

    def gate(j, u_ref):
        a, g = [conv(u_ref, b, col) for b, col in enumerate(cols(j))]
        act_scr[:, pl.ds(cols(j)[0], F_CHUNK)] = (g * jax.nn.sigmoid(g) * a).astype(BF16)

    def pair(k, carry):
        up_proj(2 * k + 1, u1_scr)
        gate(2 * k, u0_scr)
        up_proj(2 * k + 2, u0_scr)
        gate(2 * k + 1, u1_scr)
        return carry

    assert n_chunks % 2 == 1
    up_proj(0, u0_scr)
    lax.fori_loop(0, n_chunks // 2, pair, 0, unroll=True)
    gate(n_chunks - 1, u0_scr)
    gt2 = mod_ref[:, 5 * D_MODEL:6 * D_MODEL]
    x2 = x1_ref[...] + gt2 * _dot(act_scr[...], wd_ref[...])
    if final:
        x2 = _rms(x2, gfin_ref[...], D_MODEL)
    o_ref[...] = x2


def _ffn(latent, final, x1, h2, mod_l, wug, conv_w, conv_b, wdown, g_final):
    n = x1.shape[0]
    T = T_FFN
    S = DEC_SEQ if latent else SEQ
    nt = n // T
    tpb = (DEC_SEQ // T) if latent else nt
    base = 1 if latent else 0
    hb = T // BF16_ROWS
    nhb = n // BF16_ROWS
    conv_b2 = conv_b.reshape(1, 2 * D_FF)
    whole = lambda a: pl.BlockSpec(a.shape, lambda i: (0,) * a.ndim, pipeline_mode=pl.Buffered(1))
    return pl.pallas_call(
        functools.partial(_ffn_kernel, T, S, final),
        grid=(nt,),
        in_specs=[
            pl.BlockSpec((T, D_MODEL), lambda i: (i, 0)),
            pl.BlockSpec((T, D_MODEL), lambda i: (i, 0)),
            pl.BlockSpec((BF16_ROWS, D_MODEL), lambda i: (jnp.maximum(i * hb - 1, 0), 0)),
            pl.BlockSpec((BF16_ROWS, D_MODEL), lambda i: (jnp.minimum((i + 1) * hb, nhb - 1), 0)),
            pl.BlockSpec((None, 1, 6 * D_MODEL), lambda i: (base + i // tpb, 0, 0)),
            whole(wug), whole(conv_w), whole(conv_b2), whole(wdown), whole(g_final),
        ],
        out_specs=pl.BlockSpec((T, D_MODEL), lambda i: (i, 0)),
        out_shape=jax.ShapeDtypeStruct((n, D_MODEL), F32),
        scratch_shapes=[pltpu.VMEM((T + 2 * BF16_ROWS, D_MODEL), BF16),
                        pltpu.VMEM((2, T + 2 * BF16_ROWS, F_CHUNK), F32),
                        pltpu.VMEM((2, T + 2 * BF16_ROWS, F_CHUNK), F32),
                        pltpu.VMEM((T, D_FF), BF16)],
        compiler_params=_cparams(("arbitrary",)),
        name="ffn_lat" if latent else "ffn_ctx",
    )(x1, h2, h2, h2, mod_l, wug, conv_w, conv_b2, wdown, g_final)


def _rope_tables(n_tok, dim, lane0, width):
    rows = n_tok // GRID_W
    r = np.repeat(np.arange(rows), GRID_W).astype(np.float64)
    col = np.tile(np.arange(GRID_W), rows).astype(np.float64)
    quarter = dim // 4
    inv = ROPE_BASE ** (-np.arange(quarter, dtype=np.float64) / quarter)
    ang_r = r[:, None] * inv
    ang_c = col[:, None] * inv
    cos = np.concatenate([np.cos(ang_r)] * 2 + [np.cos(ang_c)] * 2, axis=1)
    sin = np.concatenate([-np.sin(ang_r), np.sin(ang_r), -np.sin(ang_c), np.sin(ang_c)], axis=1)
    reps = width // dim
    cos = np.concatenate([cos] * reps, axis=1)
    sin = np.concatenate([sin] * reps, axis=1)
    pad = ((0, 0), (lane0, LANES - lane0 - width))
    return jnp.asarray(np.pad(cos, pad, constant_values=1.0), F32), jnp.asarray(np.pad(sin, pad), F32)


def _layer_weights(w_in, g_cq, w_uq, w_ukv):
    z = lambda r, c: jnp.zeros((r, c), F32)
    o = np.cumsum([0, A_WIDTH, A_KV_WIDTH, A_KV_WIDTH, B_WIDTH, C_Q_LORA, C_KV_LORA, C_ROPE])
    win_p = jnp.concatenate([
        w_in[:, o[0]:o[1]].reshape(D_MODEL, A_HEADS, HEAD_DIM)[:, A_HEAD_ORDER, :].reshape(D_MODEL, A_WIDTH),
        w_in[:, o[1]:o[4]],
        w_in[:, o[4]:o[5]], z(D_MODEL, CQ_PAD - C_Q_LORA),
        w_in[:, o[5]:o[6]],
        z(D_MODEL, KR_LANE0), w_in[:, o[6]:o[7]], z(D_MODEL, LANES - KR_LANE0 - C_ROPE),
    ], axis=1).astype(BF16)
    gcq_p = jnp.pad(g_cq, (0, CQ_PAD - C_Q_LORA)).reshape(1, CQ_PAD)
    hq = C_NOPE + C_ROPE
    wuq_h = w_uq.reshape(C_Q_LORA, C_HEADS, hq)
    wuq_p = jnp.pad(wuq_h, ((0, CQ_PAD - C_Q_LORA), (0, 0), (0, C_HEAD_PAD - hq)))
    wuq_p = wuq_p.reshape(CQ_PAD, C_HEADS * C_HEAD_PAD).astype(BF16)
    wukv_h = w_ukv.reshape(C_KV_LORA, C_HEADS, C_NOPE + C_V)
    wk = jnp.pad(wukv_h[:, :, :C_NOPE], ((0, 0), (0, 0), (0, C_HEAD_PAD - C_NOPE)))
    zv = jnp.zeros((C_KV_LORA, C_HEAD_PAD - C_V), F32)
    wv = [wukv_h[:, h, C_NOPE:] for h in range(C_HEADS)]
    wv = jnp.concatenate([jnp.concatenate([zv, w] if h % 2 else [w, zv], axis=1) for h, w in enumerate(wv)], axis=1)
    wukv_p = jnp.concatenate([wk.reshape(C_KV_LORA, CW), wv], axis=1).astype(BF16)
    return win_p, gcq_p, wuq_p, wukv_p


def kernel(x_prompt, x_sample, cache_win_k, cache_win_v, cache_mla_ckv, cache_mla_krope, c, c_ctx,
           w_ada, b_ada, g_mix, w_in, sink, g_cq, w_uq, g_ckv, w_ukv, w_out, g_ffn, w_ug, conv_w,
           conv_b, w_down, g_final):
    n_ctx = BATCH * SEQ
    n_lat = DEC_BATCH * DEC_SEQ
    xp = x_prompt.reshape(n_ctx, D_MODEL)
    xs = x_sample.reshape(n_lat, D_MODEL)

    cvecs = jnp.concatenate([c_ctx[None, :], c, jnp.zeros((8 - 1 - DEC_BATCH, D_MODEL), F32)], axis=0)
    mod = _modulation(cvecs, w_ada, b_ada)

    cos_a, sin_a = _rope_tables(DEC_SEQ, HEAD_DIM, 0, LANES)
    cos_c, sin_c = _rope_tables(DEC_SEQ, C_ROPE, KR_LANE0, C_ROPE)
    tables = (cos_a, sin_a, cos_c, sin_c)
    fc = _channel_dft()
    g_final2 = g_final.reshape(1, D_MODEL)

    new_k, new_v, new_ckv, new_kr = [], [], [], []
    for l in range(DEPTH):
        win_p, gcq_p, wuq_p, wukv_p = _layer_weights(w_in[l], g_cq[l], w_uq[l], w_ukv[l])
        mod_l = mod[l].reshape(8, 1, 6 * D_MODEL)
        g_mix_l = g_mix[l].reshape(1, D_MODEL)
        g_ckv_l = g_ckv[l].reshape(1, C_KV_LORA)
        g_ffn_l = g_ffn[l].reshape(1, D_MODEL)
        sink_l = sink[l].reshape(1, A_HEADS)
        wout_a = w_out[l, :A_WIDTH].reshape(A_HEADS, HEAD_DIM, D_MODEL)[jnp.array(A_HEAD_ORDER)]
        wout_l = jnp.concatenate([wout_a.reshape(A_WIDTH, D_MODEL), w_out[l, A_WIDTH:]], axis=0).astype(BF16)
        wug_l = w_ug[l].astype(BF16)
        wdown_l = w_down[l].astype(BF16)
        final = l == DEPTH - 1

        (qa, ka, va, fb, qc, kc, vc, ska, sva, sckv, skr) = _projection(
            False, xp, mod_l, g_mix_l, win_p, gcq_p, wuq_p, g_ckv_l, wukv_p, None)
        r3 = lambda a: a.reshape(BATCH, SEQ, a.shape[-1])
        oa = _attn_a_ctx(sink_l, r3(qa), r3(ka), r3(va))
        ob = _fourier_ctx(r3(fb), fc)
        oc = _mla(r3(qc), r3(kc), r3(vc), None, None, SEQ, SEQ)
        f2 = lambda a: a.reshape(n_ctx, a.shape[-1])
        x1, h2 = _outproj(False, xp, f2(oa), f2(ob), f2(oc), mod_l, g_ffn_l, wout_l)
        xp = _ffn(False, final, x1, h2, mod_l, wug_l, conv_w[l], conv_b[l], wdown_l, g_final2)
        new_k.append(ska.reshape(BATCH, SEQ, A_KV_HEADS, HEAD_DIM))
        new_v.append(sva.reshape(BATCH, SEQ, A_KV_HEADS, HEAD_DIM))
        new_ckv.append(sckv.reshape(BATCH, SEQ, C_KV_LORA))
        new_kr.append(skr.reshape(BATCH, SEQ, C_ROPE))

        (qa, ka, va, fb, qc, kc, vc) = _projection(
            True, xs, mod_l, g_mix_l, win_p, gcq_p, wuq_p, g_ckv_l, wukv_p, tables)
        r3 = lambda a: a.reshape(DEC_BATCH, DEC_SEQ, a.shape[-1])
        kwin = cache_win_k[:, l].reshape(DEC_BATCH, PAST_LEN, A_KV_WIDTH).astype(BF16)
        vwin = cache_win_v[:, l].reshape(DEC_BATCH, PAST_LEN, A_KV_WIDTH).astype(BF16)
        oa = _attn_a_lat(sink_l, r3(qa), r3(ka), r3(va), kwin, vwin)
        ob = _fourier_lat(r3(fb), fc)
        kr_pad = jnp.pad(cache_mla_krope[:, l], ((0, 0), (0, 0), (KR_LANE0, LANES - KR_LANE0 - C_ROPE)))
        kc_ctx, vc_ctx = _kvcache(cache_mla_ckv[:, l], kr_pad, wukv_p)
        oc = _mla(r3(qc), r3(kc), r3(vc), kc_ctx, vc_ctx, TQ_MLA, TK_MLA)
        f2 = lambda a: a.reshape(n_lat, a.shape[-1])
        x1, h2 = _outproj(True, xs, f2(oa), f2(ob), f2(oc), mod_l, g_ffn_l, wout_l)
        xs = _ffn(True, final, x1, h2, mod_l, wug_l, conv_w[l], conv_b[l], wdown_l, g_final2)

    y_prompt = xp.reshape(BATCH, SEQ, D_MODEL)
    y_sample = xs.reshape(DEC_BATCH, DEC_SEQ, D_MODEL)
    return (y_prompt, y_sample, jnp.stack(new_k, axis=1), jnp.stack(new_v, axis=1),
            jnp.stack(new_ckv, axis=1), jnp.stack(new_kr, axis=1))
```

```python
import functools

import numpy as np
import jax
import jax.numpy as jnp
from jax import lax
from jax.experimental import pallas as pl
from jax.experimental.pallas import tpu as pltpu

F32 = jnp.float32
BF16 = jnp.bfloat16

D_MODEL = 1024
BATCH = 16
SEQ = 256
DEPTH = 2
DEC_BATCH = 4
DEC_SEQ = 4096
PAST_LEN = 256
GRID_W = 64
HEAD_DIM = 64
A_HEADS = 8
A_KV_HEADS = 2
A_GROUP = 4
A_WIDTH = 512
A_KV_WIDTH = 128
WINDOW = 128
BLOCK = 128
B_WIDTH = 256
B_GROUP_DIM = 64
B_GROUPS = 4
C_HEADS = 4
C_NOPE = 64
C_ROPE = 32
C_V = 64
C_Q_LORA = 192
C_KV_LORA = 128
C_WIDTH = 256
D_FF = 2816
ROPE_BASE = 10000.0
EPS = 1e-6
NEG = -1e30

LANES = 128
BF16_ROWS = 16
C_HEAD_PAD = 128
CW = C_HEADS * C_HEAD_PAD
CQ_PAD = 256
VMEM_LIMIT = 56 * 1024 * 1024

_QA0, _KA0, _VA0, _FB0, _CQ0, _CKV0, _KR0, _IN_PAD = 0, 512, 640, 768, 1024, 1280, 1408, 1536
KR_LANE0 = C_NOPE

T_PROJ = 1024
T_OUT = 1024
T_FFN = 1024
F_CHUNK = 256
TQ_MLA = 1024
TK_MLA = 2048
A_SUB = 4
A_HEAD_ORDER = [h for j in range(A_GROUP) for h in (j, A_GROUP + j)]
LOG2_E = float(np.log2(np.e))
FOURIER_COLS = 8


def _cparams(sem):
    return pltpu.CompilerParams(dimension_semantics=sem, vmem_limit_bytes=VMEM_LIMIT)


def _dot(a, b):
    return jnp.dot(a, b, preferred_element_type=F32)


def _dot_nt(a, b):
    return lax.dot_general(a, b, (((1,), (1,)), ((), ())), preferred_element_type=F32)


def _rms(x, g, n):
    ms = jnp.sum(x * x, axis=-1, keepdims=True) * (1.0 / n)
    return x * lax.rsqrt(ms + EPS) * g


def _mod_kernel(c_ref, w_ref, b_ref, o_ref):
    cv = c_ref[...]
    s = cv * jax.nn.sigmoid(cv)
    o_ref[...] = jnp.dot(s, w_ref[...], preferred_element_type=F32,
                         precision=lax.Precision.HIGHEST) + b_ref[...]


def _modulation(cvecs, w_ada, b_ada):
    nj = 6
    return pl.pallas_call(
        _mod_kernel,
        grid=(DEPTH, nj),
        in_specs=[
            pl.BlockSpec((8, D_MODEL), lambda l, j: (0, 0)),
            pl.BlockSpec((None, D_MODEL, D_MODEL), lambda l, j: (l, 0, j)),
            pl.BlockSpec((None, 1, D_MODEL), lambda l, j: (l, 0, j)),
        ],
        out_specs=pl.BlockSpec((None, 8, D_MODEL), lambda l, j: (l, 0, j)),
        out_shape=jax.ShapeDtypeStruct((DEPTH, 8, 6 * D_MODEL), F32),
        compiler_params=_cparams(("arbitrary", "arbitrary")),
        name="modulation",
    )(cvecs, w_ada, b_ada.reshape(DEPTH, 1, 6 * D_MODEL))


def _rope_block(x, cos, sin, half):
    lane = lax.broadcasted_iota(jnp.int32, x.shape, 1)
    first = (lane % (2 * half)) < half
    partner = jnp.where(first, pltpu.roll(x, LANES - half, 1), pltpu.roll(x, half, 1))
    return x * cos + partner * sin


def _rope(x, cos, sin, half):
    blocks = [_rope_block(x[:, j:j + LANES], cos, sin, half) for j in range(0, x.shape[1], LANES)]
    return blocks[0] if len(blocks) == 1 else jnp.concatenate(blocks, axis=1)


def _proj_kernel(latent, x_ref, mod_ref, g_ref, win_ref, gcq_ref, wuq_ref, gckv_ref, wukv_ref, *rest):
    if latent:
        cosa_ref, sina_ref, cosc_ref, sinc_ref = rest[:4]
        qa_ref, ka_ref, va_ref, fb_ref, qc_ref, kc_ref, vc_ref = rest[4:]
    else:
        qa_ref, ka_ref, va_ref, fb_ref, qc_ref, kc_ref, vc_ref, ska_ref, sva_ref, sckv_ref, skr_ref = rest
    x = x_ref[...]
    y = _rms(x, g_ref[...], D_MODEL)
    sh1 = mod_ref[:, 0:D_MODEL]
    sc1 = mod_ref[:, D_MODEL:2 * D_MODEL]
    h = (y * (1.0 + sc1) + sh1).astype(BF16)

    qa = _dot(h, win_ref[:, _QA0:_KA0])
    kva = _dot(h, win_ref[:, _KA0:_FB0])
    ka, va = kva[:, :A_KV_WIDTH], kva[:, A_KV_WIDTH:]
    fb = _dot(h, win_ref[:, _FB0:_CQ0])
    cq = _dot(h, win_ref[:, _CQ0:_CKV0])
    ckr = _dot(h, win_ref[:, _CKV0:_IN_PAD])
    ckv, kr = ckr[:, :C_KV_LORA], ckr[:, C_KV_LORA:]

    if not latent:
        ska_ref[...] = ka
        sva_ref[...] = va
        skr_ref[...] = kr[:, KR_LANE0:KR_LANE0 + C_ROPE]
    else:
        qa = _rope(qa, cosa_ref[...], sina_ref[...], HEAD_DIM // 4)
        ka = _rope(ka, cosa_ref[...], sina_ref[...], HEAD_DIM // 4)
        kr = _rope(kr, cosc_ref[...], sinc_ref[...], C_ROPE // 4)
    qa_ref[...] = (qa * (LOG2_E * HEAD_DIM ** -0.5)).astype(BF16)
    ka_ref[...] = ka.astype(BF16)
    va_ref[...] = va.astype(BF16)
    fb_ref[...] = fb.astype(fb_ref.dtype)

    cqn = _rms(cq, gcq_ref[...], C_Q_LORA).astype(BF16)
    qc = _dot(cqn, wuq_ref[...])
    if latent:
        qc = _rope(qc, cosc_ref[...], sinc_ref[...], C_ROPE // 4)
    qc_ref[...] = (qc * (LOG2_E * (C_NOPE + C_ROPE) ** -0.5)).astype(BF16)

    ckvn = _rms(ckv, gckv_ref[...], C_KV_LORA)
    if not latent:
        sckv_ref[...] = ckvn
    _store_kv(_dot(ckvn.astype(BF16), wukv_ref[...]), kr, kc_ref, vc_ref)


def _store_kv(kv, kr, kc_ref, vc_ref):
    kc_ref[...] = (kv[:, :CW] + jnp.concatenate([kr] * C_HEADS, axis=1)).astype(BF16)
    lane = lax.broadcasted_iota(jnp.int32, (1, CW), 1)
    upper = lane % C_HEAD_PAD >= C_V
    odd = (lane // C_HEAD_PAD) % 2 == 1
    vc_ref[...] = (kv[:, CW:] + jnp.where(upper != odd, 1.0, 0.0).astype(F32)).astype(BF16)


def _projection(latent, x, mod_l, g_mix, win_p, gcq_p, wuq_p, gckv, wukv_p, tables):
    n = x.shape[0]
    T = T_PROJ
    nt = n // T
    tpb = (DEC_SEQ // T) if latent else nt
    base = 1 if latent else 0
    tok = lambda w: pl.BlockSpec((T, w), lambda i: (i, 0))
    full = lambda a: pl.BlockSpec(a.shape, lambda i: (0,) * a.ndim)
    in_specs = [
        tok(D_MODEL),
        pl.BlockSpec((None, 1, 6 * D_MODEL), lambda i: (base + i // tpb, 0, 0)),
        full(g_mix), full(win_p), full(gcq_p), full(wuq_p), full(gckv), full(wukv_p),
    ]
    args = [x, mod_l, g_mix, win_p, gcq_p, wuq_p, gckv, wukv_p]
    if latent:
        in_specs += [pl.BlockSpec((T, LANES), lambda i: (i % tpb, 0))] * 4
        args += list(tables)
    widths = [A_WIDTH, A_KV_WIDTH, A_KV_WIDTH, B_WIDTH, CW, CW, CW]
    out_specs = [tok(w) for w in widths]
    out_shape = [jax.ShapeDtypeStruct((n, w), BF16) for w in widths]
    if latent:
        out_shape[3] = jax.ShapeDtypeStruct((n, B_WIDTH), F32)
    if not latent:
        sw = [A_KV_WIDTH, A_KV_WIDTH, C_KV_LORA, C_ROPE]
        out_specs += [tok(w) for w in sw]
        out_shape += [jax.ShapeDtypeStruct((n, w), F32) for w in sw]
    return pl.pallas_call(
        functools.partial(_proj_kernel, latent),
        grid=(nt,),
        in_specs=in_specs,
        out_specs=out_specs,
        out_shape=out_shape,
        compiler_params=_cparams(("arbitrary",)),
        name="proj_lat" if latent else "proj_ctx",
    )(*args)


def _kvcache_kernel(ckv_ref, kr_ref, wukv_ref, kc_ref, vc_ref):
    _store_kv(_dot(ckv_ref[...].astype(BF16), wukv_ref[...]), kr_ref[...], kc_ref, vc_ref)


def _kvcache(ckv, kr_pad, wukv_p):
    B, P, _ = ckv.shape
    return pl.pallas_call(
        _kvcache_kernel,
        grid=(B,),
        in_specs=[
            pl.BlockSpec((None, P, C_KV_LORA), lambda b: (b, 0, 0)),
            pl.BlockSpec((None, P, LANES), lambda b: (b, 0, 0)),
            pl.BlockSpec(wukv_p.shape, lambda b: (0, 0)),
        ],
        out_specs=[pl.BlockSpec((None, P, CW), lambda b: (b, 0, 0))] * 2,
        out_shape=[jax.ShapeDtypeStruct((B, P, CW), BF16)] * 2,
        compiler_params=_cparams(("arbitrary",)),
        name="kvcache",
    )(ckv, kr_pad, wukv_p)


def _gqa_group(g, q_blocks, keys, masks, values, sink_ref):
    tq = q_blocks[0].shape[0]
    lane = lax.broadcasted_iota(jnp.int32, (tq, LANES), 1)
    mine = (lane >= HEAD_DIM) if g else (lane < HEAD_DIM)
    qs = jnp.concatenate([jnp.where(mine, qb, jnp.zeros_like(qb)) for qb in q_blocks], axis=0)
    parts = []
    for k, mask in zip(keys, masks):
        s = _dot_nt(qs, k)
        parts.append(s if mask is None else jnp.where(mask, s, NEG))
    s = parts[0] if len(parts) == 1 else jnp.concatenate(parts, axis=1)
    rowblk = lax.broadcasted_iota(jnp.int32, (A_GROUP * tq, 1), 0) // tq
    sk = jnp.full((A_GROUP * tq, 1), sink_ref[0, g * A_GROUP] * LOG2_E, F32)
    for j in range(1, A_GROUP):
        sk = jnp.where(rowblk == j, sink_ref[0, g * A_GROUP + j] * LOG2_E, sk)
    m = jnp.maximum(jnp.max(s, axis=-1, keepdims=True), sk)
    p = jnp.exp2(s - m)
    l = jnp.sum(p, axis=-1, keepdims=True) + jnp.exp2(sk - m)
    v = values[0] if len(values) == 1 else jnp.concatenate(values, axis=0)
    return _dot(p.astype(BF16), v) * (1.0 / l)


def _gqa_store(o_ref, rows, tq, o0, o1):
    low = lax.broadcasted_iota(jnp.int32, (tq, LANES), 1) < HEAD_DIM
    for j in range(A_GROUP):
        blk = jnp.where(low, o0[j * tq:(j + 1) * tq], o1[j * tq:(j + 1) * tq])
        o_ref[rows, j * LANES:(j + 1) * LANES] = blk.astype(BF16)


def _attn_a_ctx_kernel(sink_ref, q_ref, k_ref, v_ref, o_ref):
    n = q_ref.shape[0]
    q_blocks = [q_ref[:, j * LANES:(j + 1) * LANES] for j in range(A_GROUP)]
    outs = [_gqa_group(g, q_blocks, [k_ref[...]], [None], [v_ref[...]], sink_ref) for g in range(A_KV_HEADS)]
    _gqa_store(o_ref, slice(None), n, *outs)


def _attn_a_ctx(sink, q, k, v):
    B, n, _ = q.shape
    blk = lambda w: pl.BlockSpec((None, n, w), lambda b: (b, 0, 0))
    return pl.pallas_call(
        _attn_a_ctx_kernel,
        grid=(B,),
        in_specs=[pl.BlockSpec(memory_space=pltpu.SMEM), blk(A_WIDTH), blk(A_KV_WIDTH), blk(A_KV_WIDTH)],
        out_specs=blk(A_WIDTH),
        out_shape=jax.ShapeDtypeStruct((B, n, A_WIDTH), BF16),
        compiler_params=_cparams(("arbitrary",)),
        name="attn_a_ctx",
    )(sink, q, k, v)


def _attn_a_lat_kernel(sink_ref, q_ref, k_ref, v_ref, kc_ref, vc_ref, o_ref):
    n = k_ref.shape[0]
    band = 3 * BLOCK
    kcx = kc_ref[...]
    vcx = vc_ref[...]

    def body(sb, carry):
        blk = pl.program_id(1) * A_SUB + sb
        rows = pl.ds(pl.multiple_of(sb * BLOCK, BLOCK), BLOCK)
        start = pl.multiple_of(jnp.clip((blk - 1) * BLOCK, 0, n - band), BLOCK)
        kb = k_ref[pl.ds(start, band), :]
        vb = v_ref[pl.ds(start, band), :]
        qpos = blk * BLOCK + lax.broadcasted_iota(jnp.int32, (BLOCK, band), 0)
        kpos = start + lax.broadcasted_iota(jnp.int32, (BLOCK, band), 1)
        mask = jnp.concatenate([jnp.abs(qpos - kpos) <= WINDOW] * A_GROUP, axis=0)
        q_blocks = [q_ref[rows, j * LANES:(j + 1) * LANES] for j in range(A_GROUP)]
        outs = [_gqa_group(g, q_blocks, [kb, kcx], [mask, None], [vb, vcx], sink_ref)
                for g in range(A_KV_HEADS)]
        _gqa_store(o_ref, rows, BLOCK, *outs)
        return carry

    lax.fori_loop(0, A_SUB, body, 0, unroll=True)


def _attn_a_lat(sink, q, k, v, k_ctx, v_ctx):
    B, n, _ = q.shape
    P = k_ctx.shape[1]
    tq = A_SUB * BLOCK
    seq = lambda w, m: pl.BlockSpec((None, m, w), lambda b, i: (b, 0, 0))
    return pl.pallas_call(
        _attn_a_lat_kernel,
        grid=(B, n // tq),
        in_specs=[pl.BlockSpec(memory_space=pltpu.SMEM),
                  pl.BlockSpec((None, tq, A_WIDTH), lambda b, i: (b, i, 0)),
                  seq(A_KV_WIDTH, n), seq(A_KV_WIDTH, n), seq(A_KV_WIDTH, P), seq(A_KV_WIDTH, P)],
        out_specs=pl.BlockSpec((None, tq, A_WIDTH), lambda b, i: (b, i, 0)),
        out_shape=jax.ShapeDtypeStruct((B, n, A_WIDTH), BF16),
        compiler_params=_cparams(("arbitrary", "arbitrary")),
        name="attn_a_lat",
    )(sink, q, k, v, k_ctx, v_ctx)


def _mla_step(q_ref, k, v, m_scr, acc_scr):
    reps = k.shape[0] // LANES
    for h in range(C_HEADS):
        hs = slice(h * C_HEAD_PAD, (h + 1) * C_HEAD_PAD)
        s = _dot_nt(q_ref[:, hs], k[:, hs])
        m_old = m_scr[h]
        m_new = jnp.maximum(m_old, jnp.max(s, axis=-1, keepdims=True))
        alpha = jnp.exp2(m_old - m_new)
        p = jnp.exp2(s - jnp.concatenate([m_new] * reps, axis=1))
        m_scr[h] = m_new
        acc_scr[h] = acc_scr[h] * alpha + _dot(p.astype(BF16), v[:, hs])


def _mla_kernel(n_chunks, tk, has_ctx, q_ref, k_ref, v_ref, *rest):
    if has_ctx:
        kc_ref, vc_ref = rest[:2]
        rest = rest[2:]
    x_ref, a_ref, b_ref, mod_ref, g_ref, wout_ref, x1_ref, h2_ref, m_scr, acc_scr = rest
    tq = q_ref.shape[0]
    m_scr[...] = jnp.full(m_scr.shape, -jnp.inf, F32)
    acc_scr[...] = jnp.zeros(acc_scr.shape, F32)

    def body(c, carry):
        ks = pl.multiple_of(c * tk, tk)
        _mla_step(q_ref, k_ref[pl.ds(ks, tk), :], v_ref[pl.ds(ks, tk), :], m_scr, acc_scr)
        return carry

    if has_ctx:
        _mla_step(q_ref, kc_ref[...], vc_ref[...], m_scr, acc_scr)
    lax.fori_loop(0, n_chunks, body, 0)

    low = lax.broadcasted_iota(jnp.int32, (tq, LANES), 1) < C_V
    oc = []
    for j in range(C_HEADS // 2):
        even, odd = acc_scr[2 * j], acc_scr[2 * j + 1]
        sums = pltpu.roll(jnp.where(low, odd, even), C_V, 1)
        oc.append((jnp.where(low, even, odd) / sums).astype(BF16))
    _outproj(x_ref[...], a_ref[...], b_ref[...].astype(BF16), jnp.concatenate(oc, axis=1), mod_ref, g_ref,
             wout_ref, x1_ref, h2_ref)


def _outproj(x, oa, ob, oc, mod_ref, g_ref, wout_ref, x1_ref, h2_ref):
    mix = (_dot(oa, wout_ref[0:A_WIDTH, :]) + _dot(ob, wout_ref[A_WIDTH:A_WIDTH + B_WIDTH, :])
           + _dot(oc, wout_ref[A_WIDTH + B_WIDTH:, :]))
    gt1 = mod_ref[:, 2 * D_MODEL:3 * D_MODEL]
    sh2 = mod_ref[:, 3 * D_MODEL:4 * D_MODEL]
    sc2 = mod_ref[:, 4 * D_MODEL:5 * D_MODEL]
    x1 = x + gt1 * mix
    x1_ref[...] = x1
    h2_ref[...] = (_rms(x1, g_ref[...], D_MODEL) * (1.0 + sc2) + sh2).astype(BF16)


def _mla(q, k, v, k_ctx, v_ctx, tq, tk, x, oa, ob, mod_l, g_ffn, wout):
    B, n, _ = q.shape
    nk = k.shape[1]
    has_ctx = k_ctx is not None
    base = 1 if has_ctx else 0
    once = dict(pipeline_mode=pl.Buffered(1))
    seq = lambda m: pl.BlockSpec((None, m, CW), lambda b, i: (b, 0, 0), **once)
    tile = lambda w: pl.BlockSpec((None, tq, w), lambda b, i: (b, i, 0))
    in_specs = [tile(CW), seq(nk), seq(nk)]
    args = [q, k, v]
    if has_ctx:
        in_specs += [seq(k_ctx.shape[1])] * 2
        args += [k_ctx, v_ctx]
    in_specs += [tile(D_MODEL), tile(A_WIDTH), tile(B_WIDTH),
                 pl.BlockSpec((None, 1, 6 * D_MODEL), lambda b, i: ((base + b) if has_ctx else 0, 0, 0)),
                 pl.BlockSpec(g_ffn.shape, lambda b, i: (0, 0)),
                 pl.BlockSpec(wout.shape, lambda b, i: (0, 0), **once)]
    args += [x, oa, ob, mod_l, g_ffn, wout]
    return pl.pallas_call(
        functools.partial(_mla_kernel, nk // tk, tk, has_ctx),
        grid=(B, n // tq),
        in_specs=in_specs,
        out_specs=[tile(D_MODEL), tile(D_MODEL)],
        out_shape=[jax.ShapeDtypeStruct((B, n, D_MODEL), F32), jax.ShapeDtypeStruct((B, n, D_MODEL), BF16)],
        scratch_shapes=[pltpu.VMEM((C_HEADS, tq, LANES), F32), pltpu.VMEM((C_HEADS, tq, LANES), F32)],
        compiler_params=_cparams(("arbitrary", "arbitrary")),
        name="mla_lat" if has_ctx else "mla_ctx",
    )(*args)


def _dft_tables(n):
    j = np.arange(n, dtype=np.int64)
    ang = ((j[:, None] * j[None, :]) % n) * (2.0 * np.pi / n)
    return np.cos(ang), np.sin(ang)


def _channel_dft():
    c, s = _dft_tables(B_GROUP_DIM)
    eye = np.eye(B_GROUPS)
    return _table_bf16(np.concatenate([np.kron(eye, c), -np.kron(eye, s)], axis=1))


def _table_bf16(t):
    return jnp.asarray(t, F32).astype(BF16)


def _fourier_ctx_kernel(scale, x_ref, fc_ref, fn_ref, o_ref):
    u = _dot(x_ref[...], fc_ref[...])
    ucat = jnp.concatenate([u[:, :B_WIDTH], u[:, B_WIDTH:]], axis=0).astype(BF16)
    o_ref[...] = (_dot(fn_ref[...], ucat) * scale).astype(BF16)


def _fourier_ctx(fb, fc):
    B, n, _ = fb.shape
    c, s = _dft_tables(n)
    fn = _table_bf16(np.concatenate([c, s], axis=1))
    scale = float((n * B_GROUP_DIM) ** -0.5)
    return pl.pallas_call(
        functools.partial(_fourier_ctx_kernel, scale),
        grid=(B,),
        in_specs=[pl.BlockSpec((None, n, B_WIDTH), lambda b: (b, 0, 0)),
                  pl.BlockSpec(fc.shape, lambda b: (0, 0)),
                  pl.BlockSpec(fn.shape, lambda b: (0, 0))],
        out_specs=pl.BlockSpec((None, n, B_WIDTH), lambda b: (b, 0, 0)),
        out_shape=jax.ShapeDtypeStruct((B, n, B_WIDTH), BF16),
        compiler_params=_cparams(("arbitrary",)),
        name="fourier_ctx",
    )(fb, fc, fn)


def _fourier_lat_kernel(scale, x_ref, fc_ref, g_ref, f_ref, o_ref, zr_scr, zi_scr):
    j = pl.program_id(1)
    nc = pl.num_programs(1) // 2

    @pl.when(j < nc)
    def _():
        for cc in range(FOURIER_COLS):
            xc = x_ref[:, cc, :].astype(BF16)
            u = _dot(xc, fc_ref[...]).astype(BF16)
            p = _dot(g_ref[cc], u)
            R = p.shape[0] // 2
            zshape = (R // FOURIER_COLS, FOURIER_COLS, B_WIDTH)
            zr_scr[j * FOURIER_COLS + cc] = (p[:R, :B_WIDTH] - p[R:, B_WIDTH:]).reshape(zshape)
            zi_scr[j * FOURIER_COLS + cc] = (p[:R, B_WIDTH:] + p[R:, :B_WIDTH]).reshape(zshape)

    @pl.when(j >= nc)
    def _():
        for kk in range(FOURIER_COLS):
            z = jnp.concatenate([zr_scr[:, j - nc, kk, :], zi_scr[:, j - nc, kk, :]], axis=0).astype(BF16)
            o_ref[:, kk, :] = _dot(f_ref[...], z) * scale


def _fourier_lat(fb, fc):
    B, n, _ = fb.shape
    R = n // GRID_W
    W = GRID_W * B_WIDTH
    kr = np.arange(R, dtype=np.int64)
    pos = GRID_W * np.arange(R, dtype=np.int64)[None, None, :] + np.arange(GRID_W, dtype=np.int64)[:, None, None]
    ang = ((kr[None, :, None] * pos) % n) * (2.0 * np.pi / n)
    g = _table_bf16(np.concatenate([np.cos(ang), -np.sin(ang)], axis=1))
    c64, s64 = _dft_tables(GRID_W)
    f2 = _table_bf16(np.concatenate([c64, s64], axis=1))
    assert R == GRID_W
    nc = GRID_W // FOURIER_COLS
    scale = float((n * B_GROUP_DIM) ** -0.5)
    blk = (None, GRID_W, FOURIER_COLS, B_WIDTH)
    zscr = pltpu.VMEM((GRID_W, R // FOURIER_COLS, FOURIER_COLS, B_WIDTH), F32)
    out = pl.pallas_call(
        functools.partial(_fourier_lat_kernel, scale),
        grid=(B, 2 * nc),
        in_specs=[pl.BlockSpec(blk, lambda b, j: (b, 0, jnp.minimum(j, nc - 1), 0)),
                  pl.BlockSpec(fc.shape, lambda b, j: (0, 0)),
                  pl.BlockSpec((FOURIER_COLS, 2 * R, R), lambda b, j: (jnp.minimum(j, nc - 1), 0, 0)),
                  pl.BlockSpec(f2.shape, lambda b, j: (0, 0))],
        out_specs=pl.BlockSpec(blk, lambda b, j: (b, 0, jnp.maximum(j - nc, 0), 0)),
        out_shape=jax.ShapeDtypeStruct((B, GRID_W, R, B_WIDTH), F32),
        scratch_shapes=[zscr, zscr],
        compiler_params=_cparams(("arbitrary", "arbitrary")),
        name="fourier_lat",
    )(fb.reshape(B, R, GRID_W, B_WIDTH), fc, g, f2)
    return out.reshape(B, n, B_WIDTH)


def _ffn_kernel(T, S, final, x1_ref, hm_ref, hp_ref, hn_ref, mod_ref, wug_ref, cw_ref, cb_ref, wd_ref, gfin_ref,
                o_ref, hs_scr, u0_scr, u1_scr, act_scr):
    i = pl.program_id(0)
    H = BF16_ROWS
    at_start = (i * T) % S == 0
    at_end = ((i + 1) * T) % S == 0
    hs_scr[0:H] = jnp.where(at_start, jnp.zeros_like(hp_ref[...]), hp_ref[...])
    hs_scr[H:T + H] = hm_ref[...]
    hs_scr[T + H:T + 2 * H] = jnp.where(at_end, jnp.zeros_like(hn_ref[...]), hn_ref[...])
    if S < T:
        pos = lax.broadcasted_iota(jnp.int32, (T, 1), 0) % S
        has_prev = pos != 0
        has_next = pos != S - 1

    n_chunks = D_FF // F_CHUNK
    cols = lambda j: (pl.multiple_of(j * F_CHUNK, F_CHUNK), pl.multiple_of(j * F_CHUNK + D_FF, LANES))

    def up_proj(j, u_ref):
        for b, col in enumerate(cols(j)):
            u_ref[b] = _dot(hs_scr[...], wug_ref[:, pl.ds(col, F_CHUNK)])

    def conv(u_ref, b, col):
        u = u_ref[b]
        rows = u.shape[0]
        up = pltpu.roll(u, 1, 0)[H:T + H]
        un = pltpu.roll(u, rows - 1, 0)[H:T + H]
        if S < T:
            up = jnp.where(has_prev, up, 0.0)
            un = jnp.where(has_next, un, 0.0)
        cw = cw_ref[:, pl.ds(col, F_CHUNK)]
        return (up * cw[0:1, :] + u[H:T + H] * cw[1:2, :] + un * cw[2:3, :]
                + cb_ref[:, pl.ds(col, F_CHUNK)])

    def gate(j, u_ref):
        a, g = [conv(u_ref, b, col) for b, col in enumerate(cols(j))]
        act_scr[:, pl.ds(cols(j)[0], F_CHUNK)] = (g * jax.nn.sigmoid(g) * a).astype(BF16)

    def pair(k, carry):
        up_proj(2 * k + 1, u1_scr)
        gate(2 * k, u0_scr)
        up_proj(2 * k + 2, u0_scr)
        gate(2 * k + 1, u1_scr)
        return carry

    assert n_chunks % 2 == 1
    up_proj(0, u0_scr)
    lax.fori_loop(0, n_chunks // 2, pair, 0, unroll=True)
    gate(n_chunks - 1, u0_scr)
    gt2 = mod_ref[:, 5 * D_MODEL:6 * D_MODEL]
    x2 = x1_ref[...] + gt2 * _dot(act_scr[...], wd_ref[...])
    if final:
        x2 = _rms(x2, gfin_ref[...], D_MODEL)
    o_ref[...] = x2


def _ffn(latent, final, x1, h2, mod_l, wug, conv_w, conv_b, wdown, g_final):
    n = x1.shape[0]
    T = T_FFN
    S = DEC_SEQ if latent else SEQ
    nt = n // T
    tpb = (DEC_SEQ // T) if latent else nt
    base = 1 if latent else 0
    hb = T // BF16_ROWS
    nhb = n // BF16_ROWS
    conv_b2 = conv_b.reshape(1, 2 * D_FF)
    whole = lambda a: pl.BlockSpec(a.shape, lambda i: (0,) * a.ndim, pipeline_mode=pl.Buffered(1))
    return pl.pallas_call(
        functools.partial(_ffn_kernel, T, S, final),
        grid=(nt,),
        in_specs=[
            pl.BlockSpec((T, D_MODEL), lambda i: (i, 0)),
            pl.BlockSpec((T, D_MODEL), lambda i: (i, 0)),
            pl.BlockSpec((BF16_ROWS, D_MODEL), lambda i: (jnp.maximum(i * hb - 1, 0), 0)),
            pl.BlockSpec((BF16_ROWS, D_MODEL), lambda i: (jnp.minimum((i + 1) * hb, nhb - 1), 0)),
            pl.BlockSpec((None, 1, 6 * D_MODEL), lambda i: (base + i // tpb, 0, 0)),
            whole(wug), whole(conv_w), whole(conv_b2), whole(wdown), whole(g_final),
        ],
        out_specs=pl.BlockSpec((T, D_MODEL), lambda i: (i, 0)),
        out_shape=jax.ShapeDtypeStruct((n, D_MODEL), F32),
        scratch_shapes=[pltpu.VMEM((T + 2 * BF16_ROWS, D_MODEL), BF16),
                        pltpu.VMEM((2, T + 2 * BF16_ROWS, F_CHUNK), F32),
                        pltpu.VMEM((2, T + 2 * BF16_ROWS, F_CHUNK), F32),
                        pltpu.VMEM((T, D_FF), BF16)],
        compiler_params=_cparams(("arbitrary",)),
        name="ffn_lat" if latent else "ffn_ctx",
    )(x1, h2, h2, h2, mod_l, wug, conv_w, conv_b2, wdown, g_final)


def _rope_tables(n_tok, dim, lane0, width):
    rows = n_tok // GRID_W
    r = np.repeat(np.arange(rows), GRID_W).astype(np.float64)
    col = np.tile(np.arange(GRID_W), rows).astype(np.float64)
    quarter = dim // 4
    inv = ROPE_BASE ** (-np.arange(quarter, dtype=np.float64) / quarter)
    ang_r = r[:, None] * inv
    ang_c = col[:, None] * inv
    cos = np.concatenate([np.cos(ang_r)] * 2 + [np.cos(ang_c)] * 2, axis=1)
    sin = np.concatenate([-np.sin(ang_r), np.sin(ang_r), -np.sin(ang_c), np.sin(ang_c)], axis=1)
    reps = width // dim
    cos = np.concatenate([cos] * reps, axis=1)
    sin = np.concatenate([sin] * reps, axis=1)
    pad = ((0, 0), (lane0, LANES - lane0 - width))
    return jnp.asarray(np.pad(cos, pad, constant_values=1.0), F32), jnp.asarray(np.pad(sin, pad), F32)


def _layer_weights(w_in, g_cq, w_uq, w_ukv):
    z = lambda r, c: jnp.zeros((r, c), F32)
    o = np.cumsum([0, A_WIDTH, A_KV_WIDTH, A_KV_WIDTH, B_WIDTH, C_Q_LORA, C_KV_LORA, C_ROPE])
    win_p = jnp.concatenate([
        w_in[:, o[0]:o[1]].reshape(D_MODEL, A_HEADS, HEAD_DIM)[:, A_HEAD_ORDER, :].reshape(D_MODEL, A_WIDTH),
        w_in[:, o[1]:o[4]],
        w_in[:, o[4]:o[5]], z(D_MODEL, CQ_PAD - C_Q_LORA),
        w_in[:, o[5]:o[6]],
        z(D_MODEL, KR_LANE0), w_in[:, o[6]:o[7]], z(D_MODEL, LANES - KR_LANE0 - C_ROPE),
    ], axis=1).astype(BF16)
    gcq_p = jnp.pad(g_cq, (0, CQ_PAD - C_Q_LORA)).reshape(1, CQ_PAD)
    hq = C_NOPE + C_ROPE
    wuq_h = w_uq.reshape(C_Q_LORA, C_HEADS, hq)
    wuq_p = jnp.pad(wuq_h, ((0, CQ_PAD - C_Q_LORA), (0, 0), (0, C_HEAD_PAD - hq)))
    wuq_p = wuq_p.reshape(CQ_PAD, C_HEADS * C_HEAD_PAD).astype(BF16)
    wukv_h = w_ukv.reshape(C_KV_LORA, C_HEADS, C_NOPE + C_V)
    wk = jnp.pad(wukv_h[:, :, :C_NOPE], ((0, 0), (0, 0), (0, C_HEAD_PAD - C_NOPE)))
    zv = jnp.zeros((C_KV_LORA, C_HEAD_PAD - C_V), F32)
    wv = [wukv_h[:, h, C_NOPE:] for h in range(C_HEADS)]
    wv = jnp.concatenate([jnp.concatenate([zv, w] if h % 2 else [w, zv], axis=1) for h, w in enumerate(wv)], axis=1)
    wukv_p = jnp.concatenate([wk.reshape(C_KV_LORA, CW), wv], axis=1).astype(BF16)
    return win_p, gcq_p, wuq_p, wukv_p


def kernel(x_prompt, x_sample, cache_win_k, cache_win_v, cache_mla_ckv, cache_mla_krope, c, c_ctx,
           w_ada, b_ada, g_mix, w_in, sink, g_cq, w_uq, g_ckv, w_ukv, w_out, g_ffn, w_ug, conv_w,
           conv_b, w_down, g_final):
    n_ctx = BATCH * SEQ
    n_lat = DEC_BATCH * DEC_SEQ
    xp = x_prompt.reshape(n_ctx, D_MODEL)
    xs = x_sample.reshape(n_lat, D_MODEL)

    cvecs = jnp.concatenate([c_ctx[None, :], c, jnp.zeros((8 - 1 - DEC_BATCH, D_MODEL), F32)], axis=0)
    mod = _modulation(cvecs, w_ada, b_ada)

    cos_a, sin_a = _rope_tables(DEC_SEQ, HEAD_DIM, 0, LANES)
    cos_c, sin_c = _rope_tables(DEC_SEQ, C_ROPE, KR_LANE0, C_ROPE)
    tables = (cos_a, sin_a, cos_c, sin_c)
    fc = _channel_dft()
    g_final2 = g_final.reshape(1, D_MODEL)

    new_k, new_v, new_ckv, new_kr = [], [], [], []
    for l in range(DEPTH):
        win_p, gcq_p, wuq_p, wukv_p = _layer_weights(w_in[l], g_cq[l], w_uq[l], w_ukv[l])
        mod_l = mod[l].reshape(8, 1, 6 * D_MODEL)
        g_mix_l = g_mix[l].reshape(1, D_MODEL)
        g_ckv_l = g_ckv[l].reshape(1, C_KV_LORA)
        g_ffn_l = g_ffn[l].reshape(1, D_MODEL)
        sink_l = sink[l].reshape(1, A_HEADS)
        wout_a = w_out[l, :A_WIDTH].reshape(A_HEADS, HEAD_DIM, D_MODEL)[jnp.array(A_HEAD_ORDER)]
        wout_l = jnp.concatenate([wout_a.reshape(A_WIDTH, D_MODEL), w_out[l, A_WIDTH:]], axis=0).astype(BF16)
        wug_l = w_ug[l].astype(BF16)
        wdown_l = w_down[l].astype(BF16)
        final = l == DEPTH - 1

        (qa, ka, va, fb, qc, kc, vc, ska, sva, sckv, skr) = _projection(
            False, xp, mod_l, g_mix_l, win_p, gcq_p, wuq_p, g_ckv_l, wukv_p, None)
        r3 = lambda a: a.reshape(BATCH, SEQ, a.shape[-1])
        oa = _attn_a_ctx(sink_l, r3(qa), r3(ka), r3(va))
        ob = _fourier_ctx(r3(fb), fc)
        x1, h2 = _mla(r3(qc), r3(kc), r3(vc), None, None, SEQ, SEQ, r3(xp), oa, ob, mod_l, g_ffn_l, wout_l)
        f2 = lambda a: a.reshape(n_ctx, a.shape[-1])
        xp = _ffn(False, final, f2(x1), f2(h2), mod_l, wug_l, conv_w[l], conv_b[l], wdown_l, g_final2)
        new_k.append(ska.reshape(BATCH, SEQ, A_KV_HEADS, HEAD_DIM))
        new_v.append(sva.reshape(BATCH, SEQ, A_KV_HEADS, HEAD_DIM))
        new_ckv.append(sckv.reshape(BATCH, SEQ, C_KV_LORA))
        new_kr.append(skr.reshape(BATCH, SEQ, C_ROPE))

        (qa, ka, va, fb, qc, kc, vc) = _projection(
            True, xs, mod_l, g_mix_l, win_p, gcq_p, wuq_p, g_ckv_l, wukv_p, tables)
        r3 = lambda a: a.reshape(DEC_BATCH, DEC_SEQ, a.shape[-1])
        kwin = cache_win_k[:, l].reshape(DEC_BATCH, PAST_LEN, A_KV_WIDTH).astype(BF16)
        vwin = cache_win_v[:, l].reshape(DEC_BATCH, PAST_LEN, A_KV_WIDTH).astype(BF16)
        oa = _attn_a_lat(sink_l, r3(qa), r3(ka), r3(va), kwin, vwin)
        ob = _fourier_lat(r3(fb), fc)
        kr_pad = jnp.pad(cache_mla_krope[:, l], ((0, 0), (0, 0), (KR_LANE0, LANES - KR_LANE0 - C_ROPE)))
        kc_ctx, vc_ctx = _kvcache(cache_mla_ckv[:, l], kr_pad, wukv_p)
        x1, h2 = _mla(r3(qc), r3(kc), r3(vc), kc_ctx, vc_ctx, TQ_MLA, TK_MLA, r3(xs), oa, ob, mod_l, g_ffn_l,
                      wout_l)
        f2 = lambda a: a.reshape(n_lat, a.shape[-1])
        xs = _ffn(True, final, f2(x1), f2(h2), mod_l, wug_l, conv_w[l], conv_b[l], wdown_l, g_final2)

    y_prompt = xp.reshape(BATCH, SEQ, D_MODEL)
    y_sample = xs.reshape(DEC_BATCH, DEC_SEQ, D_MODEL)
    return (y_prompt, y_sample, jnp.stack(new_k, axis=1), jnp.stack(new_v, axis=1),
            jnp.stack(new_ckv, axis=1), jnp.stack(new_kr, axis=1))
```

```python
import functools

import numpy as np
import jax
import jax.numpy as jnp
from jax import lax
from jax.experimental import pallas as pl
from jax.experimental.pallas import tpu as pltpu

F32 = jnp.float32
BF16 = jnp.bfloat16

D_MODEL = 1024
BATCH = 16
SEQ = 256
DEPTH = 2
DEC_BATCH = 4
DEC_SEQ = 4096
PAST_LEN = 256
GRID_W = 64
HEAD_DIM = 64
A_HEADS = 8
A_KV_HEADS = 2
A_GROUP = 4
A_WIDTH = 512
A_KV_WIDTH = 128
WINDOW = 128
BLOCK = 128
B_WIDTH = 256
B_GROUP_DIM = 64
B_GROUPS = 4
C_HEADS = 4
C_NOPE = 64
C_ROPE = 32
C_V = 64
C_Q_LORA = 192
C_KV_LORA = 128
C_WIDTH = 256
D_FF = 2816
ROPE_BASE = 10000.0
EPS = 1e-6
NEG = -1e30

LANES = 128
BF16_ROWS = 16
C_HEAD_PAD = 128
CW = C_HEADS * C_HEAD_PAD
CQ_PAD = 256
VMEM_LIMIT = 56 * 1024 * 1024

_QA0, _KA0, _VA0, _FB0, _CQ0, _CKV0, _KR0, _IN_PAD = 0, 512, 640, 768, 1024, 1280, 1408, 1536
KR_LANE0 = C_NOPE

T_PROJ = 1024
T_OUT = 1024
T_FFN = 1024
F_CHUNK = 256
TQ_MLA = 1024
TK_MLA = 2048
A_SUB = 4
A_HEAD_ORDER = [h for j in range(A_GROUP) for h in (j, A_GROUP + j)]
LOG2_E = float(np.log2(np.e))
FOURIER_COLS = 8


def _cparams(sem):
    return pltpu.CompilerParams(dimension_semantics=sem, vmem_limit_bytes=VMEM_LIMIT)


def _dot(a, b):
    return jnp.dot(a, b, preferred_element_type=F32)


def _dot_nt(a, b):
    return lax.dot_general(a, b, (((1,), (1,)), ((), ())), preferred_element_type=F32)


def _rms(x, g, n):
    ms = jnp.sum(x * x, axis=-1, keepdims=True) * (1.0 / n)
    return x * lax.rsqrt(ms + EPS) * g


def _mod_kernel(c_ref, w_ref, b_ref, o_ref):
    cv = c_ref[...]
    s = cv * jax.nn.sigmoid(cv)
    o_ref[...] = jnp.dot(s, w_ref[...], preferred_element_type=F32,
                         precision=lax.Precision.HIGHEST) + b_ref[...]


def _modulation(cvecs, w_ada, b_ada):
    nj = 6
    return pl.pallas_call(
        _mod_kernel,
        grid=(DEPTH, nj),
        in_specs=[
            pl.BlockSpec((8, D_MODEL), lambda l, j: (0, 0)),
            pl.BlockSpec((None, D_MODEL, D_MODEL), lambda l, j: (l, 0, j)),
            pl.BlockSpec((None, 1, D_MODEL), lambda l, j: (l, 0, j)),
        ],
        out_specs=pl.BlockSpec((None, 8, D_MODEL), lambda l, j: (l, 0, j)),
        out_shape=jax.ShapeDtypeStruct((DEPTH, 8, 6 * D_MODEL), F32),
        compiler_params=_cparams(("arbitrary", "arbitrary")),
        name="modulation",
    )(cvecs, w_ada, b_ada.reshape(DEPTH, 1, 6 * D_MODEL))


def _rope_block(x, cos, sin, half):
    lane = lax.broadcasted_iota(jnp.int32, x.shape, 1)
    first = (lane % (2 * half)) < half
    partner = jnp.where(first, pltpu.roll(x, LANES - half, 1), pltpu.roll(x, half, 1))
    return x * cos + partner * sin


def _rope(x, cos, sin, half):
    blocks = [_rope_block(x[:, j:j + LANES], cos, sin, half) for j in range(0, x.shape[1], LANES)]
    return blocks[0] if len(blocks) == 1 else jnp.concatenate(blocks, axis=1)


def _proj_kernel(latent, x_ref, mod_ref, g_ref, win_ref, gcq_ref, wuq_ref, gckv_ref, wukv_ref, *rest):
    if latent:
        cosa_ref, sina_ref, cosc_ref, sinc_ref = rest[:4]
        qa_ref, ka_ref, va_ref, fb_ref, qc_ref, kc_ref, vc_ref = rest[4:]
    else:
        qa_ref, ka_ref, va_ref, fb_ref, qc_ref, kc_ref, vc_ref, ska_ref, sva_ref, sckv_ref, skr_ref = rest
    x = x_ref[...]
    y = _rms(x, g_ref[...], D_MODEL)
    sh1 = mod_ref[:, 0:D_MODEL]
    sc1 = mod_ref[:, D_MODEL:2 * D_MODEL]
    h = (y * (1.0 + sc1) + sh1).astype(BF16)

    qa = _dot(h, win_ref[:, _QA0:_KA0])
    kva = _dot(h, win_ref[:, _KA0:_FB0])
    ka, va = kva[:, :A_KV_WIDTH], kva[:, A_KV_WIDTH:]
    fb = _dot(h, win_ref[:, _FB0:_CQ0])
    cq = _dot(h, win_ref[:, _CQ0:_CKV0])
    ckr = _dot(h, win_ref[:, _CKV0:_IN_PAD])
    ckv, kr = ckr[:, :C_KV_LORA], ckr[:, C_KV_LORA:]

    if not latent:
        ska_ref[...] = ka
        sva_ref[...] = va
        skr_ref[...] = kr[:, KR_LANE0:KR_LANE0 + C_ROPE]
    else:
        qa = _rope(qa, cosa_ref[...], sina_ref[...], HEAD_DIM // 4)
        ka = _rope(ka, cosa_ref[...], sina_ref[...], HEAD_DIM // 4)
        kr = _rope(kr, cosc_ref[...], sinc_ref[...], C_ROPE // 4)
    qa_ref[...] = (qa * (LOG2_E * HEAD_DIM ** -0.5)).astype(BF16)
    ka_ref[...] = ka.astype(BF16)
    va_ref[...] = va.astype(BF16)
    fb_ref[...] = fb.astype(fb_ref.dtype)

    cqn = _rms(cq, gcq_ref[...], C_Q_LORA).astype(BF16)
    qc = _dot(cqn, wuq_ref[...])
    if latent:
        qc = _rope(qc, cosc_ref[...], sinc_ref[...], C_ROPE // 4)
    qc_ref[...] = (qc * (LOG2_E * (C_NOPE + C_ROPE) ** -0.5)).astype(BF16)

    ckvn = _rms(ckv, gckv_ref[...], C_KV_LORA)
    if not latent:
        sckv_ref[...] = ckvn
    _store_kv(_dot(ckvn.astype(BF16), wukv_ref[...]), kr, kc_ref, vc_ref)


def _store_kv(kv, kr, kc_ref, vc_ref):
    kc_ref[...] = (kv[:, :CW] + jnp.concatenate([kr] * C_HEADS, axis=1)).astype(BF16)
    lane = lax.broadcasted_iota(jnp.int32, (1, CW), 1)
    upper = lane % C_HEAD_PAD >= C_V
    odd = (lane // C_HEAD_PAD) % 2 == 1
    vc_ref[...] = (kv[:, CW:] + jnp.where(upper != odd, 1.0, 0.0).astype(F32)).astype(BF16)


def _projection(latent, x, mod_l, g_mix, win_p, gcq_p, wuq_p, gckv, wukv_p, tables):
    n = x.shape[0]
    T = T_PROJ
    nt = n // T
    tpb = (DEC_SEQ // T) if latent else nt
    base = 1 if latent else 0
    tok = lambda w: pl.BlockSpec((T, w), lambda i: (i, 0))
    full = lambda a: pl.BlockSpec(a.shape, lambda i: (0,) * a.ndim)
    in_specs = [
        tok(D_MODEL),
        pl.BlockSpec((None, 1, 6 * D_MODEL), lambda i: (base + i // tpb, 0, 0)),
        full(g_mix), full(win_p), full(gcq_p), full(wuq_p), full(gckv), full(wukv_p),
    ]
    args = [x, mod_l, g_mix, win_p, gcq_p, wuq_p, gckv, wukv_p]
    if latent:
        in_specs += [pl.BlockSpec((T, LANES), lambda i: (i % tpb, 0))] * 4
        args += list(tables)
    widths = [A_WIDTH, A_KV_WIDTH, A_KV_WIDTH, B_WIDTH, CW, CW, CW]
    out_specs = [tok(w) for w in widths]
    out_shape = [jax.ShapeDtypeStruct((n, w), BF16) for w in widths]
    if latent:
        out_shape[3] = jax.ShapeDtypeStruct((n, B_WIDTH), F32)
    if not latent:
        sw = [A_KV_WIDTH, A_KV_WIDTH, C_KV_LORA, C_ROPE]
        out_specs += [tok(w) for w in sw]
        out_shape += [jax.ShapeDtypeStruct((n, w), F32) for w in sw]
    return pl.pallas_call(
        functools.partial(_proj_kernel, latent),
        grid=(nt,),
        in_specs=in_specs,
        out_specs=out_specs,
        out_shape=out_shape,
        compiler_params=_cparams(("arbitrary",)),
        name="proj_lat" if latent else "proj_ctx",
    )(*args)


def _kvcache_kernel(ckv_ref, kr_ref, wukv_ref, kc_ref, vc_ref):
    _store_kv(_dot(ckv_ref[...].astype(BF16), wukv_ref[...]), kr_ref[...], kc_ref, vc_ref)


def _kvcache(ckv, kr_pad, wukv_p):
    B, P, _ = ckv.shape
    return pl.pallas_call(
        _kvcache_kernel,
        grid=(B,),
        in_specs=[
            pl.BlockSpec((None, P, C_KV_LORA), lambda b: (b, 0, 0)),
            pl.BlockSpec((None, P, LANES), lambda b: (b, 0, 0)),
            pl.BlockSpec(wukv_p.shape, lambda b: (0, 0)),
        ],
        out_specs=[pl.BlockSpec((None, P, CW), lambda b: (b, 0, 0))] * 2,
        out_shape=[jax.ShapeDtypeStruct((B, P, CW), BF16)] * 2,
        compiler_params=_cparams(("arbitrary",)),
        name="kvcache",
    )(ckv, kr_pad, wukv_p)


def _gqa_group(g, q_blocks, keys, masks, values, sink_ref):
    tq = q_blocks[0].shape[0]
    lane = lax.broadcasted_iota(jnp.int32, (tq, LANES), 1)
    mine = (lane >= HEAD_DIM) if g else (lane < HEAD_DIM)
    qs = jnp.concatenate([jnp.where(mine, qb, jnp.zeros_like(qb)) for qb in q_blocks], axis=0)
    parts = []
    for k, mask in zip(keys, masks):
        s = _dot_nt(qs, k)
        parts.append(s if mask is None else jnp.where(mask, s, NEG))
    s = parts[0] if len(parts) == 1 else jnp.concatenate(parts, axis=1)
    rowblk = lax.broadcasted_iota(jnp.int32, (A_GROUP * tq, 1), 0) // tq
    sk = jnp.full((A_GROUP * tq, 1), sink_ref[0, g * A_GROUP] * LOG2_E, F32)
    for j in range(1, A_GROUP):
        sk = jnp.where(rowblk == j, sink_ref[0, g * A_GROUP + j] * LOG2_E, sk)
    m = jnp.maximum(jnp.max(s, axis=-1, keepdims=True), sk)
    p = jnp.exp2(s - m)
    l = jnp.sum(p, axis=-1, keepdims=True) + jnp.exp2(sk - m)
    v = values[0] if len(values) == 1 else jnp.concatenate(values, axis=0)
    return _dot(p.astype(BF16), v) * (1.0 / l)


def _gqa_store(o_ref, rows, tq, o0, o1):
    low = lax.broadcasted_iota(jnp.int32, (tq, LANES), 1) < HEAD_DIM
    for j in range(A_GROUP):
        blk = jnp.where(low, o0[j * tq:(j + 1) * tq], o1[j * tq:(j + 1) * tq])
        o_ref[rows, j * LANES:(j + 1) * LANES] = blk.astype(BF16)


def _attn_a_ctx_kernel(sink_ref, q_ref, k_ref, v_ref, o_ref):
    n = q_ref.shape[0]
    q_blocks = [q_ref[:, j * LANES:(j + 1) * LANES] for j in range(A_GROUP)]
    outs = [_gqa_group(g, q_blocks, [k_ref[...]], [None], [v_ref[...]], sink_ref) for g in range(A_KV_HEADS)]
    _gqa_store(o_ref, slice(None), n, *outs)


def _attn_a_ctx(sink, q, k, v):
    B, n, _ = q.shape
    blk = lambda w: pl.BlockSpec((None, n, w), lambda b: (b, 0, 0))
    return pl.pallas_call(
        _attn_a_ctx_kernel,
        grid=(B,),
        in_specs=[pl.BlockSpec(memory_space=pltpu.SMEM), blk(A_WIDTH), blk(A_KV_WIDTH), blk(A_KV_WIDTH)],
        out_specs=blk(A_WIDTH),
        out_shape=jax.ShapeDtypeStruct((B, n, A_WIDTH), BF16),
        compiler_params=_cparams(("arbitrary",)),
        name="attn_a_ctx",
    )(sink, q, k, v)


def _attn_a_lat_kernel(sink_ref, q_ref, k_ref, v_ref, kc_ref, vc_ref, o_ref):
    n = k_ref.shape[0]
    band = 3 * BLOCK
    kcx = kc_ref[...]
    vcx = vc_ref[...]

    def body(sb, carry):
        blk = pl.program_id(1) * A_SUB + sb
        rows = pl.ds(pl.multiple_of(sb * BLOCK, BLOCK), BLOCK)
        start = pl.multiple_of(jnp.clip((blk - 1) * BLOCK, 0, n - band), BLOCK)
        kb = k_ref[pl.ds(start, band), :]
        vb = v_ref[pl.ds(start, band), :]
        qpos = blk * BLOCK + lax.broadcasted_iota(jnp.int32, (BLOCK, band), 0)
        kpos = start + lax.broadcasted_iota(jnp.int32, (BLOCK, band), 1)
        mask = jnp.concatenate([jnp.abs(qpos - kpos) <= WINDOW] * A_GROUP, axis=0)
        q_blocks = [q_ref[rows, j * LANES:(j + 1) * LANES] for j in range(A_GROUP)]
        outs = [_gqa_group(g, q_blocks, [kb, kcx], [mask, None], [vb, vcx], sink_ref)
                for g in range(A_KV_HEADS)]
        _gqa_store(o_ref, rows, BLOCK, *outs)
        return carry

    lax.fori_loop(0, A_SUB, body, 0, unroll=True)


def _attn_a_lat(sink, q, k, v, k_ctx, v_ctx):
    B, n, _ = q.shape
    P = k_ctx.shape[1]
    tq = A_SUB * BLOCK
    seq = lambda w, m: pl.BlockSpec((None, m, w), lambda b, i: (b, 0, 0))
    return pl.pallas_call(
        _attn_a_lat_kernel,
        grid=(B, n // tq),
        in_specs=[pl.BlockSpec(memory_space=pltpu.SMEM),
                  pl.BlockSpec((None, tq, A_WIDTH), lambda b, i: (b, i, 0)),
                  seq(A_KV_WIDTH, n), seq(A_KV_WIDTH, n), seq(A_KV_WIDTH, P), seq(A_KV_WIDTH, P)],
        out_specs=pl.BlockSpec((None, tq, A_WIDTH), lambda b, i: (b, i, 0)),
        out_shape=jax.ShapeDtypeStruct((B, n, A_WIDTH), BF16),
        compiler_params=_cparams(("arbitrary", "arbitrary")),
        name="attn_a_lat",
    )(sink, q, k, v, k_ctx, v_ctx)


def _mla_step(q_ref, k, v, m_scr, acc_scr):
    reps = k.shape[0] // LANES
    for h in range(C_HEADS):
        hs = slice(h * C_HEAD_PAD, (h + 1) * C_HEAD_PAD)
        s = _dot_nt(q_ref[:, hs], k[:, hs])
        m_old = m_scr[h]
        m_new = jnp.maximum(m_old, jnp.max(s, axis=-1, keepdims=True))
        alpha = jnp.exp2(m_old - m_new)
        p = jnp.exp2(s - jnp.concatenate([m_new] * reps, axis=1))
        m_scr[h] = m_new
        acc_scr[h] = acc_scr[h] * alpha + _dot(p.astype(BF16), v[:, hs])


def _mla_kernel(n_chunks, tk, has_ctx, q_ref, k_ref, v_ref, *rest):
    if has_ctx:
        kc_ref, vc_ref, o_ref, m_scr, acc_scr = rest
    else:
        o_ref, m_scr, acc_scr = rest
    tq = q_ref.shape[0]
    m_scr[...] = jnp.full(m_scr.shape, -jnp.inf, F32)
    acc_scr[...] = jnp.zeros(acc_scr.shape, F32)

    def body(c, carry):
        ks = pl.multiple_of(c * tk, tk)
        _mla_step(q_ref, k_ref[pl.ds(ks, tk), :], v_ref[pl.ds(ks, tk), :], m_scr, acc_scr)
        return carry

    if has_ctx:
        _mla_step(q_ref, kc_ref[...], vc_ref[...], m_scr, acc_scr)
    lax.fori_loop(0, n_chunks, body, 0)

    low = lax.broadcasted_iota(jnp.int32, (tq, LANES), 1) < C_V
    for j in range(C_HEADS // 2):
        even, odd = acc_scr[2 * j], acc_scr[2 * j + 1]
        sums = pltpu.roll(jnp.where(low, odd, even), C_V, 1)
        o_ref[:, j * LANES:(j + 1) * LANES] = (jnp.where(low, even, odd) / sums).astype(BF16)


def _mla(q, k, v, k_ctx, v_ctx, tq, tk):
    B, n, _ = q.shape
    nk = k.shape[1]
    has_ctx = k_ctx is not None
    seq = lambda m: pl.BlockSpec((None, m, CW), lambda b, i: (b, 0, 0))
    in_specs = [pl.BlockSpec((None, tq, CW), lambda b, i: (b, i, 0)), seq(nk), seq(nk)]
    args = [q, k, v]
    if has_ctx:
        in_specs += [seq(k_ctx.shape[1])] * 2
        args += [k_ctx, v_ctx]
    return pl.pallas_call(
        functools.partial(_mla_kernel, nk // tk, tk, has_ctx),
        grid=(B, n // tq),
        in_specs=in_specs,
        out_specs=pl.BlockSpec((None, tq, C_WIDTH), lambda b, i: (b, i, 0)),
        out_shape=jax.ShapeDtypeStruct((B, n, C_WIDTH), BF16),
        scratch_shapes=[pltpu.VMEM((C_HEADS, tq, LANES), F32), pltpu.VMEM((C_HEADS, tq, LANES), F32)],
        compiler_params=_cparams(("arbitrary", "arbitrary")),
        name="mla_lat" if has_ctx else "mla_ctx",
    )(*args)


def _dft_tables(n):
    j = np.arange(n, dtype=np.int64)
    ang = ((j[:, None] * j[None, :]) % n) * (2.0 * np.pi / n)
    return np.cos(ang), np.sin(ang)


def _channel_dft():
    c, s = _dft_tables(B_GROUP_DIM)
    eye = np.eye(B_GROUPS)
    return _table_bf16(np.concatenate([np.kron(eye, c), -np.kron(eye, s)], axis=1))


def _table_bf16(t):
    return jnp.asarray(t, F32).astype(BF16)


def _fourier_ctx_kernel(scale, x_ref, fc_ref, fn_ref, o_ref):
    u = _dot(x_ref[...], fc_ref[...])
    ucat = jnp.concatenate([u[:, :B_WIDTH], u[:, B_WIDTH:]], axis=0).astype(BF16)
    o_ref[...] = (_dot(fn_ref[...], ucat) * scale).astype(BF16)


def _fourier_ctx(fb, fc):
    B, n, _ = fb.shape
    c, s = _dft_tables(n)
    fn = _table_bf16(np.concatenate([c, s], axis=1))
    scale = float((n * B_GROUP_DIM) ** -0.5)
    return pl.pallas_call(
        functools.partial(_fourier_ctx_kernel, scale),
        grid=(B,),
        in_specs=[pl.BlockSpec((None, n, B_WIDTH), lambda b: (b, 0, 0)),
                  pl.BlockSpec(fc.shape, lambda b: (0, 0)),
                  pl.BlockSpec(fn.shape, lambda b: (0, 0))],
        out_specs=pl.BlockSpec((None, n, B_WIDTH), lambda b: (b, 0, 0)),
        out_shape=jax.ShapeDtypeStruct((B, n, B_WIDTH), BF16),
        compiler_params=_cparams(("arbitrary",)),
        name="fourier_ctx",
    )(fb, fc, fn)


def _fourier_lat_kernel(scale, x_ref, fc_ref, g_ref, f_ref, o_ref, zr_scr, zi_scr):
    j = pl.program_id(1)
    nc = pl.num_programs(1) // 2

    @pl.when(j < nc)
    def _():
        for cc in range(FOURIER_COLS):
            xc = x_ref[:, cc, :].astype(BF16)
            u = _dot(xc, fc_ref[...]).astype(BF16)
            p = _dot(g_ref[cc], u)
            R = p.shape[0] // 2
            zshape = (R // FOURIER_COLS, FOURIER_COLS, B_WIDTH)
            zr_scr[j * FOURIER_COLS + cc] = (p[:R, :B_WIDTH] - p[R:, B_WIDTH:]).reshape(zshape)
            zi_scr[j * FOURIER_COLS + cc] = (p[:R, B_WIDTH:] + p[R:, :B_WIDTH]).reshape(zshape)

    @pl.when(j >= nc)
    def _():
        for kk in range(FOURIER_COLS):
            z = jnp.concatenate([zr_scr[:, j - nc, kk, :], zi_scr[:, j - nc, kk, :]], axis=0).astype(BF16)
            o_ref[:, kk, :] = _dot(f_ref[...], z) * scale


def _fourier_lat(fb, fc):
    B, n, _ = fb.shape
    R = n // GRID_W
    W = GRID_W * B_WIDTH
    kr = np.arange(R, dtype=np.int64)
    pos = GRID_W * np.arange(R, dtype=np.int64)[None, None, :] + np.arange(GRID_W, dtype=np.int64)[:, None, None]
    ang = ((kr[None, :, None] * pos) % n) * (2.0 * np.pi / n)
    g = _table_bf16(np.concatenate([np.cos(ang), -np.sin(ang)], axis=1))
    c64, s64 = _dft_tables(GRID_W)
    f2 = _table_bf16(np.concatenate([c64, s64], axis=1))
    assert R == GRID_W
    nc = GRID_W // FOURIER_COLS
    scale = float((n * B_GROUP_DIM) ** -0.5)
    blk = (None, GRID_W, FOURIER_COLS, B_WIDTH)
    zscr = pltpu.VMEM((GRID_W, R // FOURIER_COLS, FOURIER_COLS, B_WIDTH), F32)
    out = pl.pallas_call(
        functools.partial(_fourier_lat_kernel, scale),
        grid=(B, 2 * nc),
        in_specs=[pl.BlockSpec(blk, lambda b, j: (b, 0, jnp.minimum(j, nc - 1), 0)),
                  pl.BlockSpec(fc.shape, lambda b, j: (0, 0)),
                  pl.BlockSpec((FOURIER_COLS, 2 * R, R), lambda b, j: (jnp.minimum(j, nc - 1), 0, 0)),
                  pl.BlockSpec(f2.shape, lambda b, j: (0, 0))],
        out_specs=pl.BlockSpec(blk, lambda b, j: (b, 0, jnp.maximum(j - nc, 0), 0)),
        out_shape=jax.ShapeDtypeStruct((B, GRID_W, R, B_WIDTH), F32),
        scratch_shapes=[zscr, zscr],
        compiler_params=_cparams(("arbitrary", "arbitrary")),
        name="fourier_lat",
    )(fb.reshape(B, R, GRID_W, B_WIDTH), fc, g, f2)
    return out.reshape(B, n, B_WIDTH)


def _outproj_kernel(x_ref, a_ref, b_ref, c_ref, mod_ref, g_ref, wout_ref, x1_ref, h2_ref):
    mix = (_dot(a_ref[...], wout_ref[0:A_WIDTH, :])
           + _dot(b_ref[...].astype(BF16), wout_ref[A_WIDTH:A_WIDTH + B_WIDTH, :])
           + _dot(c_ref[...], wout_ref[A_WIDTH + B_WIDTH:, :]))
    gt1 = mod_ref[:, 2 * D_MODEL:3 * D_MODEL]
    sh2 = mod_ref[:, 3 * D_MODEL:4 * D_MODEL]
    sc2 = mod_ref[:, 4 * D_MODEL:5 * D_MODEL]
    x1 = x_ref[...] + gt1 * mix
    x1_ref[...] = x1
    h2_ref[...] = (_rms(x1, g_ref[...], D_MODEL) * (1.0 + sc2) + sh2).astype(BF16)


def _outproj(latent, x, oa, ob, oc, mod_l, g_ffn, wout):
    n = x.shape[0]
    T = T_OUT
    nt = n // T
    tpb = (DEC_SEQ // T) if latent else nt
    base = 1 if latent else 0
    tok = lambda w: pl.BlockSpec((T, w), lambda i: (i, 0))
    return pl.pallas_call(
        _outproj_kernel,
        grid=(nt,),
        in_specs=[tok(D_MODEL), tok(A_WIDTH), tok(B_WIDTH), tok(C_WIDTH),
                  pl.BlockSpec((None, 1, 6 * D_MODEL), lambda i: (base + i // tpb, 0, 0)),
                  pl.BlockSpec(g_ffn.shape, lambda i: (0, 0)),
                  pl.BlockSpec(wout.shape, lambda i: (0, 0))],
        out_specs=[tok(D_MODEL), tok(D_MODEL)],
        out_shape=[jax.ShapeDtypeStruct((n, D_MODEL), F32), jax.ShapeDtypeStruct((n, D_MODEL), BF16)],
        compiler_params=_cparams(("arbitrary",)),
        name="outproj_lat" if latent else "outproj_ctx",
    )(x, oa, ob, oc, mod_l, g_ffn, wout)


def _ffn_kernel(T, S, final, x1_ref, hm_ref, hp_ref, hn_ref, mod_ref, wug_ref, cw_ref, cb_ref, wd_ref, gfin_ref,
                o_ref, hs_scr, u0_scr, u1_scr, act_scr):
    i = pl.program_id(0)
    H = BF16_ROWS
    at_start = (i * T) % S == 0
    at_end = ((i + 1) * T) % S == 0
    hs_scr[0:H] = jnp.where(at_start, jnp.zeros_like(hp_ref[...]), hp_ref[...])
    hs_scr[H:T + H] = hm_ref[...]
    hs_scr[T + H:T + 2 * H] = jnp.where(at_end, jnp.zeros_like(hn_ref[...]), hn_ref[...])
    if S < T:
        pos = lax.broadcasted_iota(jnp.int32, (T, 1), 0) % S
        has_prev = pos != 0
        has_next = pos != S - 1

    n_chunks = D_FF // F_CHUNK
    cols = lambda j: (pl.multiple_of(j * F_CHUNK, F_CHUNK), pl.multiple_of(j * F_CHUNK + D_FF, LANES))

    def up_proj(j, u_ref):
        for b, col in enumerate(cols(j)):
            u_ref[b] = _dot(hs_scr[...], wug_ref[:, pl.ds(col, F_CHUNK)])

    def conv(u_ref, b, col):
        u = u_ref[b]
        rows = u.shape[0]
        up = pltpu.roll(u, 1, 0)[H:T + H]
        un = pltpu.roll(u, rows - 1, 0)[H:T + H]
        if S < T:
            up = jnp.where(has_prev, up, 0.0)
            un = jnp.where(has_next, un, 0.0)
        cw = cw_ref[:, pl.ds(col, F_CHUNK)]
        return (up * cw[0:1, :] + u[H:T + H] * cw[1:2, :] + un * cw[2:3, :]
                + cb_ref[:, pl.ds(col, F_CHUNK)])

    def gate(j, u_ref):
        a, g = [conv(u_ref, b, col) for b, col in enumerate(cols(j))]
        act_scr[:, pl.ds(cols(j)[0], F_CHUNK)] = (g * jax.nn.sigmoid(g) * a).astype(BF16)

    def pair(k, carry):
        up_proj(2 * k + 1, u1_scr)
        gate(2 * k, u0_scr)
        up_proj(2 * k + 2, u0_scr)
        gate(2 * k + 1, u1_scr)
        return carry

    assert n_chunks % 2 == 1
    up_proj(0, u0_scr)
    lax.fori_loop(0, n_chunks // 2, pair, 0, unroll=True)
    gate(n_chunks - 1, u0_scr)
    gt2 = mod_ref[:, 5 * D_MODEL:6 * D_MODEL]
    x2 = x1_ref[...] + gt2 * _dot(act_scr[...], wd_ref[...])
    if final:
        x2 = _rms(x2, gfin_ref[...], D_MODEL)
    o_ref[...] = x2


def _ffn(latent, final, layer, x1, h2, mod_l, wug, conv_w, conv_b, wdown, g_final):
    n = x1.shape[0]
    T = T_FFN
    S = DEC_SEQ if latent else SEQ
    nt = n // T
    tpb = (DEC_SEQ // T) if latent else nt
    base = 1 if latent else 0
    hb = T // BF16_ROWS
    nhb = n // BF16_ROWS
    once = dict(pipeline_mode=pl.Buffered(1))
    whole = lambda a: pl.BlockSpec(a.shape, lambda i: (0,) * a.ndim, **once)
    of_layer = lambda a: pl.BlockSpec((None,) + a.shape[1:], lambda i: (layer,) + (0,) * (a.ndim - 1), **once)
    return pl.pallas_call(
        functools.partial(_ffn_kernel, T, S, final),
        grid=(nt,),
        in_specs=[
            pl.BlockSpec((T, D_MODEL), lambda i: (i, 0)),
            pl.BlockSpec((T, D_MODEL), lambda i: (i, 0)),
            pl.BlockSpec((BF16_ROWS, D_MODEL), lambda i: (jnp.maximum(i * hb - 1, 0), 0)),
            pl.BlockSpec((BF16_ROWS, D_MODEL), lambda i: (jnp.minimum((i + 1) * hb, nhb - 1), 0)),
            pl.BlockSpec((None, 1, 6 * D_MODEL), lambda i: (base + i // tpb, 0, 0)),
            of_layer(wug), of_layer(conv_w), of_layer(conv_b), of_layer(wdown), whole(g_final),
        ],
        out_specs=pl.BlockSpec((T, D_MODEL), lambda i: (i, 0)),
        out_shape=jax.ShapeDtypeStruct((n, D_MODEL), F32),
        scratch_shapes=[pltpu.VMEM((T + 2 * BF16_ROWS, D_MODEL), BF16),
                        pltpu.VMEM((2, T + 2 * BF16_ROWS, F_CHUNK), F32),
                        pltpu.VMEM((2, T + 2 * BF16_ROWS, F_CHUNK), F32),
                        pltpu.VMEM((T, D_FF), BF16)],
        compiler_params=_cparams(("arbitrary",)),
        name="ffn_lat" if latent else "ffn_ctx",
    )(x1, h2, h2, h2, mod_l, wug, conv_w, conv_b, wdown, g_final)


def _rope_tables(n_tok, dim, lane0, width):
    rows = n_tok // GRID_W
    r = np.repeat(np.arange(rows), GRID_W).astype(np.float64)
    col = np.tile(np.arange(GRID_W), rows).astype(np.float64)
    quarter = dim // 4
    inv = ROPE_BASE ** (-np.arange(quarter, dtype=np.float64) / quarter)
    ang_r = r[:, None] * inv
    ang_c = col[:, None] * inv
    cos = np.concatenate([np.cos(ang_r)] * 2 + [np.cos(ang_c)] * 2, axis=1)
    sin = np.concatenate([-np.sin(ang_r), np.sin(ang_r), -np.sin(ang_c), np.sin(ang_c)], axis=1)
    reps = width // dim
    cos = np.concatenate([cos] * reps, axis=1)
    sin = np.concatenate([sin] * reps, axis=1)
    pad = ((0, 0), (lane0, LANES - lane0 - width))
    return jnp.asarray(np.pad(cos, pad, constant_values=1.0), F32), jnp.asarray(np.pad(sin, pad), F32)


def _layer_weights(w_in, g_cq, w_uq, w_ukv):
    z = lambda r, c: jnp.zeros((r, c), F32)
    o = np.cumsum([0, A_WIDTH, A_KV_WIDTH, A_KV_WIDTH, B_WIDTH, C_Q_LORA, C_KV_LORA, C_ROPE])
    win_p = jnp.concatenate([
        w_in[:, o[0]:o[1]].reshape(D_MODEL, A_HEADS, HEAD_DIM)[:, A_HEAD_ORDER, :].reshape(D_MODEL, A_WIDTH),
        w_in[:, o[1]:o[4]],
        w_in[:, o[4]:o[5]], z(D_MODEL, CQ_PAD - C_Q_LORA),
        w_in[:, o[5]:o[6]],
        z(D_MODEL, KR_LANE0), w_in[:, o[6]:o[7]], z(D_MODEL, LANES - KR_LANE0 - C_ROPE),
    ], axis=1).astype(BF16)
    gcq_p = jnp.pad(g_cq, (0, CQ_PAD - C_Q_LORA)).reshape(1, CQ_PAD)
    hq = C_NOPE + C_ROPE
    wuq_h = w_uq.reshape(C_Q_LORA, C_HEADS, hq)
    wuq_p = jnp.pad(wuq_h, ((0, CQ_PAD - C_Q_LORA), (0, 0), (0, C_HEAD_PAD - hq)))
    wuq_p = wuq_p.reshape(CQ_PAD, C_HEADS * C_HEAD_PAD).astype(BF16)
    wukv_h = w_ukv.reshape(C_KV_LORA, C_HEADS, C_NOPE + C_V)
    wk = jnp.pad(wukv_h[:, :, :C_NOPE], ((0, 0), (0, 0), (0, C_HEAD_PAD - C_NOPE)))
    zv = jnp.zeros((C_KV_LORA, C_HEAD_PAD - C_V), F32)
    wv = [wukv_h[:, h, C_NOPE:] for h in range(C_HEADS)]
    wv = jnp.concatenate([jnp.concatenate([zv, w] if h % 2 else [w, zv], axis=1) for h, w in enumerate(wv)], axis=1)
    wukv_p = jnp.concatenate([wk.reshape(C_KV_LORA, CW), wv], axis=1).astype(BF16)
    return win_p, gcq_p, wuq_p, wukv_p


def kernel(x_prompt, x_sample, cache_win_k, cache_win_v, cache_mla_ckv, cache_mla_krope, c, c_ctx,
           w_ada, b_ada, g_mix, w_in, sink, g_cq, w_uq, g_ckv, w_ukv, w_out, g_ffn, w_ug, conv_w,
           conv_b, w_down, g_final):
    n_ctx = BATCH * SEQ
    n_lat = DEC_BATCH * DEC_SEQ
    xp = x_prompt.reshape(n_ctx, D_MODEL)
    xs = x_sample.reshape(n_lat, D_MODEL)

    cvecs = jnp.concatenate([c_ctx[None, :], c, jnp.zeros((8 - 1 - DEC_BATCH, D_MODEL), F32)], axis=0)
    mod = _modulation(cvecs, w_ada, b_ada)

    cos_a, sin_a = _rope_tables(DEC_SEQ, HEAD_DIM, 0, LANES)
    cos_c, sin_c = _rope_tables(DEC_SEQ, C_ROPE, KR_LANE0, C_ROPE)
    tables = (cos_a, sin_a, cos_c, sin_c)
    fc = _channel_dft()
    g_final2 = g_final.reshape(1, D_MODEL)
    wug_all = w_ug.astype(BF16)
    wdown_all = w_down.astype(BF16)
    conv_b_all = conv_b.reshape(DEPTH, 1, 2 * D_FF)

    new_k, new_v, new_ckv, new_kr = [], [], [], []
    for l in range(DEPTH):
        win_p, gcq_p, wuq_p, wukv_p = _layer_weights(w_in[l], g_cq[l], w_uq[l], w_ukv[l])
        mod_l = mod[l].reshape(8, 1, 6 * D_MODEL)
        g_mix_l = g_mix[l].reshape(1, D_MODEL)
        g_ckv_l = g_ckv[l].reshape(1, C_KV_LORA)
        g_ffn_l = g_ffn[l].reshape(1, D_MODEL)
        sink_l = sink[l].reshape(1, A_HEADS)
        wout_a = w_out[l, :A_WIDTH].reshape(A_HEADS, HEAD_DIM, D_MODEL)[jnp.array(A_HEAD_ORDER)]
        wout_l = jnp.concatenate([wout_a.reshape(A_WIDTH, D_MODEL), w_out[l, A_WIDTH:]], axis=0).astype(BF16)
        final = l == DEPTH - 1

        (qa, ka, va, fb, qc, kc, vc, ska, sva, sckv, skr) = _projection(
            False, xp, mod_l, g_mix_l, win_p, gcq_p, wuq_p, g_ckv_l, wukv_p, None)
        r3 = lambda a: a.reshape(BATCH, SEQ, a.shape[-1])
        oa = _attn_a_ctx(sink_l, r3(qa), r3(ka), r3(va))
        ob = _fourier_ctx(r3(fb), fc)
        oc = _mla(r3(qc), r3(kc), r3(vc), None, None, SEQ, SEQ)
        f2 = lambda a: a.reshape(n_ctx, a.shape[-1])
        x1, h2 = _outproj(False, xp, f2(oa), f2(ob), f2(oc), mod_l, g_ffn_l, wout_l)
        xp = _ffn(False, final, l, x1, h2, mod_l, wug_all, conv_w, conv_b_all, wdown_all, g_final2)
        new_k.append(ska.reshape(BATCH, SEQ, A_KV_HEADS, HEAD_DIM))
        new_v.append(sva.reshape(BATCH, SEQ, A_KV_HEADS, HEAD_DIM))
        new_ckv.append(sckv.reshape(BATCH, SEQ, C_KV_LORA))
        new_kr.append(skr.reshape(BATCH, SEQ, C_ROPE))

        (qa, ka, va, fb, qc, kc, vc) = _projection(
            True, xs, mod_l, g_mix_l, win_p, gcq_p, wuq_p, g_ckv_l, wukv_p, tables)
        r3 = lambda a: a.reshape(DEC_BATCH, DEC_SEQ, a.shape[-1])
        kwin = cache_win_k[:, l].reshape(DEC_BATCH, PAST_LEN, A_KV_WIDTH).astype(BF16)
        vwin = cache_win_v[:, l].reshape(DEC_BATCH, PAST_LEN, A_KV_WIDTH).astype(BF16)
        oa = _attn_a_lat(sink_l, r3(qa), r3(ka), r3(va), kwin, vwin)
        ob = _fourier_lat(r3(fb), fc)
        kr_pad = jnp.pad(cache_mla_krope[:, l], ((0, 0), (0, 0), (KR_LANE0, LANES - KR_LANE0 - C_ROPE)))
        kc_ctx, vc_ctx = _kvcache(cache_mla_ckv[:, l], kr_pad, wukv_p)
        oc = _mla(r3(qc), r3(kc), r3(vc), kc_ctx, vc_ctx, TQ_MLA, TK_MLA)
        f2 = lambda a: a.reshape(n_lat, a.shape[-1])
        x1, h2 = _outproj(True, xs, f2(oa), f2(ob), f2(oc), mod_l, g_ffn_l, wout_l)
        xs = _ffn(True, final, l, x1, h2, mod_l, wug_all, conv_w, conv_b_all, wdown_all, g_final2)

    y_prompt = xp.reshape(BATCH, SEQ, D_MODEL)
    y_sample = xs.reshape(DEC_BATCH, DEC_SEQ, D_MODEL)
    return (y_prompt, y_sample, jnp.stack(new_k, axis=1), jnp.stack(new_v, axis=1),
            jnp.stack(new_ckv, axis=1), jnp.stack(new_kr, axis=1))
```

```python
import functools

import numpy as np
import jax
import jax.numpy as jnp
from jax import lax
from jax.experimental import pallas as pl
from jax.experimental.pallas import tpu as pltpu

F32 = jnp.float32
BF16 = jnp.bfloat16

D_MODEL = 1024
BATCH = 16
SEQ = 256
DEPTH = 2
DEC_BATCH = 4
DEC_SEQ = 4096
PAST_LEN = 256
GRID_W = 64
HEAD_DIM = 64
A_HEADS = 8
A_KV_HEADS = 2
A_GROUP = 4
A_WIDTH = 512
A_KV_WIDTH = 128
WINDOW = 128
BLOCK = 128
B_WIDTH = 256
B_GROUP_DIM = 64
B_GROUPS = 4
C_HEADS = 4
C_NOPE = 64
C_ROPE = 32
C_V = 64
C_Q_LORA = 192
C_KV_LORA = 128
C_WIDTH = 256
D_FF = 2816
ROPE_BASE = 10000.0
EPS = 1e-6
NEG = -1e30

LANES = 128
BF16_ROWS = 16
C_HEAD_PAD = 128
CW = C_HEADS * C_HEAD_PAD
CQ_PAD = 256
VMEM_LIMIT = 56 * 1024 * 1024

_QA0, _KA0, _VA0, _FB0, _CQ0, _CKV0, _KR0, _IN_PAD = 0, 512, 640, 768, 1024, 1280, 1408, 1536
KR_LANE0 = C_NOPE

T_PROJ = 1024
T_FFN = 1024
F_CHUNK = 256
TQ_MLA = 1024
TK_MLA = 2048
A_SUB = 8
A_HEAD_ORDER = [h for j in range(A_GROUP) for h in (j, A_GROUP + j)]
LOG2_E = float(np.log2(np.e))
FOURIER_COLS = 8


def _cparams(sem):
    return pltpu.CompilerParams(dimension_semantics=sem, vmem_limit_bytes=VMEM_LIMIT)


def _dot(a, b):
    return jnp.dot(a, b, preferred_element_type=F32)


def _dot_nt(a, b):
    return lax.dot_general(a, b, (((1,), (1,)), ((), ())), preferred_element_type=F32)


def _rms(x, g, n):
    ms = jnp.sum(x * x, axis=-1, keepdims=True) * (1.0 / n)
    return x * lax.rsqrt(ms + EPS) * g


def _mod_kernel(c_ref, w_ref, b_ref, o_ref):
    cv = c_ref[...]
    s = cv * jax.nn.sigmoid(cv)
    o_ref[...] = jnp.dot(s, w_ref[...], preferred_element_type=F32,
                         precision=lax.Precision.HIGHEST) + b_ref[...]


def _modulation(cvecs, w_ada, b_ada):
    nj = 6
    return pl.pallas_call(
        _mod_kernel,
        grid=(DEPTH, nj),
        in_specs=[
            pl.BlockSpec((8, D_MODEL), lambda l, j: (0, 0)),
            pl.BlockSpec((None, D_MODEL, D_MODEL), lambda l, j: (l, 0, j)),
            pl.BlockSpec((None, 1, D_MODEL), lambda l, j: (l, 0, j)),
        ],
        out_specs=pl.BlockSpec((None, 8, D_MODEL), lambda l, j: (l, 0, j)),
        out_shape=jax.ShapeDtypeStruct((DEPTH, 8, 6 * D_MODEL), F32),
        compiler_params=_cparams(("arbitrary", "arbitrary")),
        name="modulation",
    )(cvecs, w_ada, b_ada.reshape(DEPTH, 1, 6 * D_MODEL))


def _rope_block(x, cos, sin, half):
    lane = lax.broadcasted_iota(jnp.int32, x.shape, 1)
    first = (lane % (2 * half)) < half
    partner = jnp.where(first, pltpu.roll(x, LANES - half, 1), pltpu.roll(x, half, 1))
    return x * cos + partner * sin


def _rope(x, cos, sin, half):
    blocks = [_rope_block(x[:, j:j + LANES], cos, sin, half) for j in range(0, x.shape[1], LANES)]
    return blocks[0] if len(blocks) == 1 else jnp.concatenate(blocks, axis=1)


def _proj_kernel(latent, x_ref, mod_ref, g_ref, win_ref, gcq_ref, wuq_ref, gckv_ref, wukv_ref, *rest):
    if latent:
        cosa_ref, sina_ref, cosc_ref, sinc_ref = rest[:4]
        qa_ref, ka_ref, va_ref, fb_ref, qc_ref, kc_ref, vc_ref = rest[4:]
    else:
        qa_ref, ka_ref, va_ref, fb_ref, qc_ref, kc_ref, vc_ref, ska_ref, sva_ref, sckv_ref, skr_ref = rest
    x = x_ref[...]
    y = _rms(x, g_ref[...], D_MODEL)
    sh1 = mod_ref[:, 0:D_MODEL]
    sc1 = mod_ref[:, D_MODEL:2 * D_MODEL]
    h = (y * (1.0 + sc1) + sh1).astype(BF16)

    qa = _dot(h, win_ref[:, _QA0:_KA0])
    kva = _dot(h, win_ref[:, _KA0:_FB0])
    ka, va = kva[:, :A_KV_WIDTH], kva[:, A_KV_WIDTH:]
    fb = _dot(h, win_ref[:, _FB0:_CQ0])
    cq = _dot(h, win_ref[:, _CQ0:_CKV0])
    ckr = _dot(h, win_ref[:, _CKV0:_IN_PAD])
    ckv, kr = ckr[:, :C_KV_LORA], ckr[:, C_KV_LORA:]

    if not latent:
        ska_ref[...] = ka
        sva_ref[...] = va
        skr_ref[...] = kr[:, KR_LANE0:KR_LANE0 + C_ROPE]
    else:
        qa = _rope(qa, cosa_ref[...], sina_ref[...], HEAD_DIM // 4)
        ka = _rope(ka, cosa_ref[...], sina_ref[...], HEAD_DIM // 4)
        kr = _rope(kr, cosc_ref[...], sinc_ref[...], C_ROPE // 4)
    qa_ref[...] = (qa * (LOG2_E * HEAD_DIM ** -0.5)).astype(BF16)
    ka_ref[...] = ka.astype(BF16)
    va_ref[...] = va.astype(BF16)
    fb_ref[...] = fb.astype(fb_ref.dtype)

    cqn = _rms(cq, gcq_ref[...], C_Q_LORA).astype(BF16)
    qc = _dot(cqn, wuq_ref[...])
    if latent:
        qc = _rope(qc, cosc_ref[...], sinc_ref[...], C_ROPE // 4)
    qc_ref[...] = (qc * (LOG2_E * (C_NOPE + C_ROPE) ** -0.5)).astype(BF16)

    ckvn = _rms(ckv, gckv_ref[...], C_KV_LORA)
    if not latent:
        sckv_ref[...] = ckvn
    _store_kv(_dot(ckvn.astype(BF16), wukv_ref[...]), kr, kc_ref, vc_ref)


def _store_kv(kv, kr, kc_ref, vc_ref):
    kc_ref[...] = (kv[:, :CW] + jnp.concatenate([kr] * C_HEADS, axis=1)).astype(BF16)
    lane = lax.broadcasted_iota(jnp.int32, (1, CW), 1)
    upper = lane % C_HEAD_PAD >= C_V
    odd = (lane // C_HEAD_PAD) % 2 == 1
    vc_ref[...] = (kv[:, CW:] + jnp.where(upper != odd, 1.0, 0.0).astype(F32)).astype(BF16)


def _projection(latent, x, mod_l, g_mix, win_p, gcq_p, wuq_p, gckv, wukv_p, tables):
    n = x.shape[0]
    T = T_PROJ
    nt = n // T
    tpb = (DEC_SEQ // T) if latent else nt
    base = 1 if latent else 0
    tok = lambda w: pl.BlockSpec((T, w), lambda i: (i, 0))
    full = lambda a: pl.BlockSpec(a.shape, lambda i: (0,) * a.ndim)
    in_specs = [
        tok(D_MODEL),
        pl.BlockSpec((None, 1, 6 * D_MODEL), lambda i: (base + i // tpb, 0, 0)),
        full(g_mix), full(win_p), full(gcq_p), full(wuq_p), full(gckv), full(wukv_p),
    ]
    args = [x, mod_l, g_mix, win_p, gcq_p, wuq_p, gckv, wukv_p]
    if latent:
        in_specs += [pl.BlockSpec((T, LANES), lambda i: (i % tpb, 0))] * 4
        args += list(tables)
    widths = [A_WIDTH, A_KV_WIDTH, A_KV_WIDTH, B_WIDTH, CW, CW, CW]
    out_specs = [tok(w) for w in widths]
    out_shape = [jax.ShapeDtypeStruct((n, w), BF16) for w in widths]
    if latent:
        out_shape[3] = jax.ShapeDtypeStruct((n, B_WIDTH), F32)
    if not latent:
        sw = [A_KV_WIDTH, A_KV_WIDTH, C_KV_LORA, C_ROPE]
        out_specs += [tok(w) for w in sw]
        out_shape += [jax.ShapeDtypeStruct((n, w), F32) for w in sw]
    return pl.pallas_call(
        functools.partial(_proj_kernel, latent),
        grid=(nt,),
        in_specs=in_specs,
        out_specs=out_specs,
        out_shape=out_shape,
        compiler_params=_cparams(("arbitrary",)),
        name="proj_lat" if latent else "proj_ctx",
    )(*args)


def _kvcache_kernel(ckv_ref, kr_ref, wukv_ref, kc_ref, vc_ref):
    _store_kv(_dot(ckv_ref[...].astype(BF16), wukv_ref[...]), kr_ref[...], kc_ref, vc_ref)


def _kvcache(ckv, kr_pad, wukv_p):
    B, P, _ = ckv.shape
    return pl.pallas_call(
        _kvcache_kernel,
        grid=(B,),
        in_specs=[
            pl.BlockSpec((None, P, C_KV_LORA), lambda b: (b, 0, 0)),
            pl.BlockSpec((None, P, LANES), lambda b: (b, 0, 0)),
            pl.BlockSpec(wukv_p.shape, lambda b: (0, 0)),
        ],
        out_specs=[pl.BlockSpec((None, P, CW), lambda b: (b, 0, 0))] * 2,
        out_shape=[jax.ShapeDtypeStruct((B, P, CW), BF16)] * 2,
        compiler_params=_cparams(("arbitrary",)),
        name="kvcache",
    )(ckv, kr_pad, wukv_p)


def _gqa_group(g, q_blocks, keys, masks, values, sink_ref):
    tq = q_blocks[0].shape[0]
    lane = lax.broadcasted_iota(jnp.int32, (tq, LANES), 1)
    mine = (lane >= HEAD_DIM) if g else (lane < HEAD_DIM)
    qs = jnp.concatenate([jnp.where(mine, qb, jnp.zeros_like(qb)) for qb in q_blocks], axis=0)
    parts = []
    for k, mask in zip(keys, masks):
        s = _dot_nt(qs, k)
        parts.append(s if mask is None else jnp.where(mask, s, NEG))
    s = parts[0] if len(parts) == 1 else jnp.concatenate(parts, axis=1)
    rowblk = lax.broadcasted_iota(jnp.int32, (A_GROUP * tq, 1), 0) // tq
    sk = jnp.full((A_GROUP * tq, 1), sink_ref[0, g * A_GROUP] * LOG2_E, F32)
    for j in range(1, A_GROUP):
        sk = jnp.where(rowblk == j, sink_ref[0, g * A_GROUP + j] * LOG2_E, sk)
    m = jnp.maximum(jnp.max(s, axis=-1, keepdims=True), sk)
    p = jnp.exp2(s - m)
    l = jnp.sum(p, axis=-1, keepdims=True) + jnp.exp2(sk - m)
    v = values[0] if len(values) == 1 else jnp.concatenate(values, axis=0)
    return _dot(p.astype(BF16), v) * (1.0 / l)


def _gqa_store(o_ref, rows, tq, o0, o1):
    low = lax.broadcasted_iota(jnp.int32, (tq, LANES), 1) < HEAD_DIM
    for j in range(A_GROUP):
        blk = jnp.where(low, o0[j * tq:(j + 1) * tq], o1[j * tq:(j + 1) * tq])
        o_ref[rows, j * LANES:(j + 1) * LANES] = blk.astype(BF16)


def _attn_a_ctx_kernel(sink_ref, q_ref, k_ref, v_ref, o_ref):
    n = q_ref.shape[0]
    q_blocks = [q_ref[:, j * LANES:(j + 1) * LANES] for j in range(A_GROUP)]
    outs = [_gqa_group(g, q_blocks, [k_ref[...]], [None], [v_ref[...]], sink_ref) for g in range(A_KV_HEADS)]
    _gqa_store(o_ref, slice(None), n, *outs)


def _attn_a_ctx(sink, q, k, v):
    B, n, _ = q.shape
    blk = lambda w: pl.BlockSpec((None, n, w), lambda b: (b, 0, 0))
    return pl.pallas_call(
        _attn_a_ctx_kernel,
        grid=(B,),
        in_specs=[pl.BlockSpec(memory_space=pltpu.SMEM), blk(A_WIDTH), blk(A_KV_WIDTH), blk(A_KV_WIDTH)],
        out_specs=blk(A_WIDTH),
        out_shape=jax.ShapeDtypeStruct((B, n, A_WIDTH), BF16),
        compiler_params=_cparams(("arbitrary",)),
        name="attn_a_ctx",
    )(sink, q, k, v)


def _attn_a_lat_kernel(sink_ref, q_ref, k_ref, v_ref, kc_ref, vc_ref, o_ref):
    n = k_ref.shape[0]
    band = 3 * BLOCK
    kcx = kc_ref[...]
    vcx = vc_ref[...]

    def body(sb, carry):
        blk = pl.program_id(1) * A_SUB + sb
        rows = pl.ds(pl.multiple_of(sb * BLOCK, BLOCK), BLOCK)
        start = pl.multiple_of(jnp.clip((blk - 1) * BLOCK, 0, n - band), BLOCK)
        kb = k_ref[pl.ds(start, band), :]
        vb = v_ref[pl.ds(start, band), :]
        qpos = blk * BLOCK + lax.broadcasted_iota(jnp.int32, (BLOCK, band), 0)
        kpos = start + lax.broadcasted_iota(jnp.int32, (BLOCK, band), 1)
        mask = jnp.concatenate([jnp.abs(qpos - kpos) <= WINDOW] * A_GROUP, axis=0)
        q_blocks = [q_ref[rows, j * LANES:(j + 1) * LANES] for j in range(A_GROUP)]
        outs = [_gqa_group(g, q_blocks, [kb, kcx], [mask, None], [vb, vcx], sink_ref)
                for g in range(A_KV_HEADS)]
        _gqa_store(o_ref, rows, BLOCK, *outs)
        return carry

    lax.fori_loop(0, A_SUB, body, 0, unroll=True)


def _attn_a_lat(sink, q, k, v, k_ctx, v_ctx):
    B, n, _ = q.shape
    P = k_ctx.shape[1]
    tq = A_SUB * BLOCK
    seq = lambda w, m: pl.BlockSpec((None, m, w), lambda b, i: (b, 0, 0))
    return pl.pallas_call(
        _attn_a_lat_kernel,
        grid=(B, n // tq),
        in_specs=[pl.BlockSpec(memory_space=pltpu.SMEM),
                  pl.BlockSpec((None, tq, A_WIDTH), lambda b, i: (b, i, 0)),
                  seq(A_KV_WIDTH, n), seq(A_KV_WIDTH, n), seq(A_KV_WIDTH, P), seq(A_KV_WIDTH, P)],
        out_specs=pl.BlockSpec((None, tq, A_WIDTH), lambda b, i: (b, i, 0)),
        out_shape=jax.ShapeDtypeStruct((B, n, A_WIDTH), BF16),
        compiler_params=_cparams(("arbitrary", "arbitrary")),
        name="attn_a_lat",
    )(sink, q, k, v, k_ctx, v_ctx)


def _mla_step(q_ref, k, v, m_scr, acc_scr):
    reps = k.shape[0] // LANES
    for h in range(C_HEADS):
        hs = slice(h * C_HEAD_PAD, (h + 1) * C_HEAD_PAD)
        s = _dot_nt(q_ref[:, hs], k[:, hs])
        m_old = m_scr[h]
        m_new = jnp.maximum(m_old, jnp.max(s, axis=-1, keepdims=True))
        alpha = jnp.exp2(m_old - m_new)
        p = jnp.exp2(s - jnp.concatenate([m_new] * reps, axis=1))
        m_scr[h] = m_new
        acc_scr[h] = acc_scr[h] * alpha + _dot(p.astype(BF16), v[:, hs])


def _mla_kernel(n_chunks, tk, has_ctx, q_ref, k_ref, v_ref, *rest):
    if has_ctx:
        kc_ref, vc_ref, o_ref, m_scr, acc_scr = rest
    else:
        o_ref, m_scr, acc_scr = rest
    tq = q_ref.shape[0]
    m_scr[...] = jnp.full(m_scr.shape, -jnp.inf, F32)
    acc_scr[...] = jnp.zeros(acc_scr.shape, F32)

    def body(c, carry):
        ks = pl.multiple_of(c * tk, tk)
        _mla_step(q_ref, k_ref[pl.ds(ks, tk), :], v_ref[pl.ds(ks, tk), :], m_scr, acc_scr)
        return carry

    if has_ctx:
        _mla_step(q_ref, kc_ref[...], vc_ref[...], m_scr, acc_scr)
    lax.fori_loop(0, n_chunks, body, 0)

    low = lax.broadcasted_iota(jnp.int32, (tq, LANES), 1) < C_V
    for j in range(C_HEADS // 2):
        even, odd = acc_scr[2 * j], acc_scr[2 * j + 1]
        sums = pltpu.roll(jnp.where(low, odd, even), C_V, 1)
        o_ref[:, j * LANES:(j + 1) * LANES] = (jnp.where(low, even, odd) / sums).astype(BF16)


def _mla(q, k, v, k_ctx, v_ctx, tq, tk):
    B, n, _ = q.shape
    nk = k.shape[1]
    has_ctx = k_ctx is not None
    seq = lambda m: pl.BlockSpec((None, m, CW), lambda b, i: (b, 0, 0))
    in_specs = [pl.BlockSpec((None, tq, CW), lambda b, i: (b, i, 0)), seq(nk), seq(nk)]
    args = [q, k, v]
    if has_ctx:
        in_specs += [seq(k_ctx.shape[1])] * 2
        args += [k_ctx, v_ctx]
    return pl.pallas_call(
        functools.partial(_mla_kernel, nk // tk, tk, has_ctx),
        grid=(B, n // tq),
        in_specs=in_specs,
        out_specs=pl.BlockSpec((None, tq, C_WIDTH), lambda b, i: (b, i, 0)),
        out_shape=jax.ShapeDtypeStruct((B, n, C_WIDTH), BF16),
        scratch_shapes=[pltpu.VMEM((C_HEADS, tq, LANES), F32), pltpu.VMEM((C_HEADS, tq, LANES), F32)],
        compiler_params=_cparams(("arbitrary", "arbitrary")),
        name="mla_lat" if has_ctx else "mla_ctx",
    )(*args)


def _dft_tables(n):
    j = np.arange(n, dtype=np.int64)
    ang = ((j[:, None] * j[None, :]) % n) * (2.0 * np.pi / n)
    return np.cos(ang), np.sin(ang)


def _channel_dft():
    c, s = _dft_tables(B_GROUP_DIM)
    eye = np.eye(B_GROUPS)
    return _table_bf16(np.concatenate([np.kron(eye, c), -np.kron(eye, s)], axis=1))


def _table_bf16(t):
    return jnp.asarray(t, F32).astype(BF16)


def _fourier_ctx_kernel(scale, x_ref, fc_ref, fn_ref, o_ref):
    u = _dot(x_ref[...], fc_ref[...])
    ucat = jnp.concatenate([u[:, :B_WIDTH], u[:, B_WIDTH:]], axis=0).astype(BF16)
    o_ref[...] = (_dot(fn_ref[...], ucat) * scale).astype(BF16)


def _fourier_ctx(fb, fc):
    B, n, _ = fb.shape
    c, s = _dft_tables(n)
    fn = _table_bf16(np.concatenate([c, s], axis=1))
    scale = float((n * B_GROUP_DIM) ** -0.5)
    return pl.pallas_call(
        functools.partial(_fourier_ctx_kernel, scale),
        grid=(B,),
        in_specs=[pl.BlockSpec((None, n, B_WIDTH), lambda b: (b, 0, 0)),
                  pl.BlockSpec(fc.shape, lambda b: (0, 0)),
                  pl.BlockSpec(fn.shape, lambda b: (0, 0))],
        out_specs=pl.BlockSpec((None, n, B_WIDTH), lambda b: (b, 0, 0)),
        out_shape=jax.ShapeDtypeStruct((B, n, B_WIDTH), BF16),
        compiler_params=_cparams(("arbitrary",)),
        name="fourier_ctx",
    )(fb, fc, fn)


def _fourier_lat_kernel(scale, x_ref, fc_ref, g_ref, f_ref, o_ref, zr_scr, zi_scr):
    j = pl.program_id(1)
    nc = pl.num_programs(1) // 2

    @pl.when(j < nc)
    def _():
        for cc in range(FOURIER_COLS):
            xc = x_ref[:, cc, :].astype(BF16)
            u = _dot(xc, fc_ref[...]).astype(BF16)
            p = _dot(g_ref[cc], u)
            R = p.shape[0] // 2
            zshape = (R // FOURIER_COLS, FOURIER_COLS, B_WIDTH)
            zr_scr[j * FOURIER_COLS + cc] = (p[:R, :B_WIDTH] - p[R:, B_WIDTH:]).reshape(zshape)
            zi_scr[j * FOURIER_COLS + cc] = (p[:R, B_WIDTH:] + p[R:, :B_WIDTH]).reshape(zshape)

    @pl.when(j >= nc)
    def _():
        for kk in range(FOURIER_COLS):
            z = jnp.concatenate([zr_scr[:, j - nc, kk, :], zi_scr[:, j - nc, kk, :]], axis=0).astype(BF16)
            o_ref[:, kk, :] = _dot(f_ref[...], z) * scale


def _fourier_lat(fb, fc):
    B, n, _ = fb.shape
    R = n // GRID_W
    W = GRID_W * B_WIDTH
    kr = np.arange(R, dtype=np.int64)
    pos = GRID_W * np.arange(R, dtype=np.int64)[None, None, :] + np.arange(GRID_W, dtype=np.int64)[:, None, None]
    ang = ((kr[None, :, None] * pos) % n) * (2.0 * np.pi / n)
    g = _table_bf16(np.concatenate([np.cos(ang), -np.sin(ang)], axis=1))
    c64, s64 = _dft_tables(GRID_W)
    f2 = _table_bf16(np.concatenate([c64, s64], axis=1))
    assert R == GRID_W
    nc = GRID_W // FOURIER_COLS
    scale = float((n * B_GROUP_DIM) ** -0.5)
    blk = (None, GRID_W, FOURIER_COLS, B_WIDTH)
    zscr = pltpu.VMEM((GRID_W, R // FOURIER_COLS, FOURIER_COLS, B_WIDTH), F32)
    out = pl.pallas_call(
        functools.partial(_fourier_lat_kernel, scale),
        grid=(B, 2 * nc),
        in_specs=[pl.BlockSpec(blk, lambda b, j: (b, 0, jnp.minimum(j, nc - 1), 0)),
                  pl.BlockSpec(fc.shape, lambda b, j: (0, 0)),
                  pl.BlockSpec((FOURIER_COLS, 2 * R, R), lambda b, j: (jnp.minimum(j, nc - 1), 0, 0)),
                  pl.BlockSpec(f2.shape, lambda b, j: (0, 0))],
        out_specs=pl.BlockSpec(blk, lambda b, j: (b, 0, jnp.maximum(j - nc, 0), 0)),
        out_shape=jax.ShapeDtypeStruct((B, GRID_W, R, B_WIDTH), F32),
        scratch_shapes=[zscr, zscr],
        compiler_params=_cparams(("arbitrary", "arbitrary")),
        name="fourier_lat",
    )(fb.reshape(B, R, GRID_W, B_WIDTH), fc, g, f2)
    return out.reshape(B, n, B_WIDTH)


def _ffn_kernel(T, S, final, x_ref, xp_ref, xn_ref, a_ref, ap_ref, an_ref, b_ref, bp_ref, bn_ref, c_ref, cp_ref,
                cn_ref, mod_ref, g_ref, wout_ref, wug_ref, cw_ref, cb_ref, wd_ref, gfin_ref,
                o_ref, mix_scr, hs_scr, u0_scr, u1_scr, act_scr):
    i = pl.program_id(0)
    H = BF16_ROWS
    for r0, refs in ((0, (ap_ref, bp_ref, cp_ref)), (H, (a_ref, b_ref, c_ref)), (T + H, (an_ref, bn_ref, cn_ref))):
        rows = slice(r0, r0 + refs[0].shape[0])
        mix_scr[rows, 0:A_WIDTH] = refs[0][...]
        mix_scr[rows, A_WIDTH:A_WIDTH + B_WIDTH] = refs[1][...].astype(BF16)
        mix_scr[rows, A_WIDTH + B_WIDTH:] = refs[2][...]
    mix = _dot(mix_scr[...], wout_ref[...])
    gt1 = mod_ref[:, 2 * D_MODEL:3 * D_MODEL]
    sh2 = mod_ref[:, 3 * D_MODEL:4 * D_MODEL]
    sc2 = mod_ref[:, 4 * D_MODEL:5 * D_MODEL]

    def ffn_input(x, m):
        x1 = x + gt1 * m
        return x1, (_rms(x1, g_ref[...], D_MODEL) * (1.0 + sc2) + sh2).astype(BF16)

    at_start = (i * T) % S == 0
    at_end = ((i + 1) * T) % S == 0
    hp = ffn_input(xp_ref[...], mix[0:H])[1]
    hn = ffn_input(xn_ref[...], mix[T + H:T + 2 * H])[1]
    x1, hm = ffn_input(x_ref[...], mix[H:T + H])
    o_ref[...] = x1
    hs_scr[0:H] = jnp.where(at_start, jnp.zeros_like(hp), hp)
    hs_scr[H:T + H] = hm
    hs_scr[T + H:T + 2 * H] = jnp.where(at_end, jnp.zeros_like(hn), hn)
    if S < T:
        pos = lax.broadcasted_iota(jnp.int32, (T, 1), 0) % S
        has_prev = pos != 0
        has_next = pos != S - 1

    n_chunks = D_FF // F_CHUNK
    cols = lambda j: (pl.multiple_of(j * F_CHUNK, F_CHUNK), pl.multiple_of(j * F_CHUNK + D_FF, LANES))

    def up_proj(j, u_ref):
        for b, col in enumerate(cols(j)):
            u_ref[b] = _dot(hs_scr[...], wug_ref[:, pl.ds(col, F_CHUNK)])

    def conv(u_ref, b, col):
        u = u_ref[b]
        rows = u.shape[0]
        up = pltpu.roll(u, 1, 0)[H:T + H]
        un = pltpu.roll(u, rows - 1, 0)[H:T + H]
        if S < T:
            up = jnp.where(has_prev, up, 0.0)
            un = jnp.where(has_next, un, 0.0)
        cw = cw_ref[:, pl.ds(col, F_CHUNK)]
        return (up * cw[0:1, :] + u[H:T + H] * cw[1:2, :] + un * cw[2:3, :]
                + cb_ref[:, pl.ds(col, F_CHUNK)])

    def gate(j, u_ref):
        a, g = [conv(u_ref, b, col) for b, col in enumerate(cols(j))]
        act_scr[:, pl.ds(cols(j)[0], F_CHUNK)] = (g * jax.nn.sigmoid(g) * a).astype(BF16)

    def pair(k, carry):
        up_proj(2 * k + 1, u1_scr)
        gate(2 * k, u0_scr)
        up_proj(2 * k + 2, u0_scr)
        gate(2 * k + 1, u1_scr)
        return carry

    assert n_chunks % 2 == 1
    up_proj(0, u0_scr)
    lax.fori_loop(0, n_chunks // 2, pair, 0, unroll=True)
    gate(n_chunks - 1, u0_scr)
    gt2 = mod_ref[:, 5 * D_MODEL:6 * D_MODEL]
    x2 = o_ref[...] + gt2 * _dot(act_scr[...], wd_ref[...])
    if final:
        x2 = _rms(x2, gfin_ref[...], D_MODEL)
    o_ref[...] = x2


def _ffn(latent, final, layer, x, oa, ob, oc, mod_l, g_ffn, wout, wug, conv_w, conv_b, wdown, g_final):
    n = x.shape[0]
    T = T_FFN
    S = DEC_SEQ if latent else SEQ
    nt = n // T
    tpb = (DEC_SEQ // T) if latent else nt
    base = 1 if latent else 0
    hb = T // BF16_ROWS
    nhb = n // BF16_ROWS
    once = dict(pipeline_mode=pl.Buffered(1))
    whole = lambda a: pl.BlockSpec(a.shape, lambda i: (0,) * a.ndim, **once)
    of_layer = lambda a: pl.BlockSpec((None,) + a.shape[1:], lambda i: (layer,) + (0,) * (a.ndim - 1), **once)
    with_halos = lambda w: [
        pl.BlockSpec((T, w), lambda i: (i, 0)),
        pl.BlockSpec((BF16_ROWS, w), lambda i: (jnp.maximum(i * hb - 1, 0), 0)),
        pl.BlockSpec((BF16_ROWS, w), lambda i: (jnp.minimum((i + 1) * hb, nhb - 1), 0))]
    return pl.pallas_call(
        functools.partial(_ffn_kernel, T, S, final),
        grid=(nt,),
        in_specs=[spec for a in (x, oa, ob, oc) for spec in with_halos(a.shape[1])] + [
            pl.BlockSpec((None, 1, 6 * D_MODEL), lambda i: (base + i // tpb, 0, 0)),
            whole(g_ffn), whole(wout),
            of_layer(wug), of_layer(conv_w), of_layer(conv_b), of_layer(wdown), whole(g_final),
        ],
        out_specs=pl.BlockSpec((T, D_MODEL), lambda i: (i, 0)),
        out_shape=jax.ShapeDtypeStruct((n, D_MODEL), F32),
        scratch_shapes=[pltpu.VMEM((T + 2 * BF16_ROWS, D_MODEL), BF16),
                        pltpu.VMEM((T + 2 * BF16_ROWS, D_MODEL), BF16),
                        pltpu.VMEM((2, T + 2 * BF16_ROWS, F_CHUNK), F32),
                        pltpu.VMEM((2, T + 2 * BF16_ROWS, F_CHUNK), F32),
                        pltpu.VMEM((T, D_FF), BF16)],
        compiler_params=_cparams(("arbitrary",)),
        name="ffn_lat" if latent else "ffn_ctx",
    )(x, x, x, oa, oa, oa, ob, ob, ob, oc, oc, oc, mod_l, g_ffn, wout, wug, conv_w, conv_b, wdown, g_final)


def _rope_tables(n_tok, dim, lane0, width):
    rows = n_tok // GRID_W
    r = np.repeat(np.arange(rows), GRID_W).astype(np.float64)
    col = np.tile(np.arange(GRID_W), rows).astype(np.float64)
    quarter = dim // 4
    inv = ROPE_BASE ** (-np.arange(quarter, dtype=np.float64) / quarter)
    ang_r = r[:, None] * inv
    ang_c = col[:, None] * inv
    cos = np.concatenate([np.cos(ang_r)] * 2 + [np.cos(ang_c)] * 2, axis=1)
    sin = np.concatenate([-np.sin(ang_r), np.sin(ang_r), -np.sin(ang_c), np.sin(ang_c)], axis=1)
    reps = width // dim
    cos = np.concatenate([cos] * reps, axis=1)
    sin = np.concatenate([sin] * reps, axis=1)
    pad = ((0, 0), (lane0, LANES - lane0 - width))
    return jnp.asarray(np.pad(cos, pad, constant_values=1.0), F32), jnp.asarray(np.pad(sin, pad), F32)


def _layer_weights(w_in, g_cq, w_uq, w_ukv):
    z = lambda r, c: jnp.zeros((r, c), F32)
    o = np.cumsum([0, A_WIDTH, A_KV_WIDTH, A_KV_WIDTH, B_WIDTH, C_Q_LORA, C_KV_LORA, C_ROPE])
    win_p = jnp.concatenate([
        w_in[:, o[0]:o[1]].reshape(D_MODEL, A_HEADS, HEAD_DIM)[:, A_HEAD_ORDER, :].reshape(D_MODEL, A_WIDTH),
        w_in[:, o[1]:o[4]],
        w_in[:, o[4]:o[5]], z(D_MODEL, CQ_PAD - C_Q_LORA),
        w_in[:, o[5]:o[6]],
        z(D_MODEL, KR_LANE0), w_in[:, o[6]:o[7]], z(D_MODEL, LANES - KR_LANE0 - C_ROPE),
    ], axis=1).astype(BF16)
    gcq_p = jnp.pad(g_cq, (0, CQ_PAD - C_Q_LORA)).reshape(1, CQ_PAD)
    hq = C_NOPE + C_ROPE
    wuq_h = w_uq.reshape(C_Q_LORA, C_HEADS, hq)
    wuq_p = jnp.pad(wuq_h, ((0, CQ_PAD - C_Q_LORA), (0, 0), (0, C_HEAD_PAD - hq)))
    wuq_p = wuq_p.reshape(CQ_PAD, C_HEADS * C_HEAD_PAD).astype(BF16)
    wukv_h = w_ukv.reshape(C_KV_LORA, C_HEADS, C_NOPE + C_V)
    wk = jnp.pad(wukv_h[:, :, :C_NOPE], ((0, 0), (0, 0), (0, C_HEAD_PAD - C_NOPE)))
    zv = jnp.zeros((C_KV_LORA, C_HEAD_PAD - C_V), F32)
    wv = [wukv_h[:, h, C_NOPE:] for h in range(C_HEADS)]
    wv = jnp.concatenate([jnp.concatenate([zv, w] if h % 2 else [w, zv], axis=1) for h, w in enumerate(wv)], axis=1)
    wukv_p = jnp.concatenate([wk.reshape(C_KV_LORA, CW), wv], axis=1).astype(BF16)
    return win_p, gcq_p, wuq_p, wukv_p


def kernel(x_prompt, x_sample, cache_win_k, cache_win_v, cache_mla_ckv, cache_mla_krope, c, c_ctx,
           w_ada, b_ada, g_mix, w_in, sink, g_cq, w_uq, g_ckv, w_ukv, w_out, g_ffn, w_ug, conv_w,
           conv_b, w_down, g_final):
    n_ctx = BATCH * SEQ
    n_lat = DEC_BATCH * DEC_SEQ
    xp = x_prompt.reshape(n_ctx, D_MODEL)
    xs = x_sample.reshape(n_lat, D_MODEL)

    cvecs = jnp.concatenate([c_ctx[None, :], c, jnp.zeros((8 - 1 - DEC_BATCH, D_MODEL), F32)], axis=0)
    mod = _modulation(cvecs, w_ada, b_ada)

    cos_a, sin_a = _rope_tables(DEC_SEQ, HEAD_DIM, 0, LANES)
    cos_c, sin_c = _rope_tables(DEC_SEQ, C_ROPE, KR_LANE0, C_ROPE)
    tables = (cos_a, sin_a, cos_c, sin_c)
    fc = _channel_dft()
    g_final2 = g_final.reshape(1, D_MODEL)
    wug_all = w_ug.astype(BF16)
    wdown_all = w_down.astype(BF16)
    conv_b_all = conv_b.reshape(DEPTH, 1, 2 * D_FF)

    new_k, new_v, new_ckv, new_kr = [], [], [], []
    for l in range(DEPTH):
        win_p, gcq_p, wuq_p, wukv_p = _layer_weights(w_in[l], g_cq[l], w_uq[l], w_ukv[l])
        mod_l = mod[l].reshape(8, 1, 6 * D_MODEL)
        g_mix_l = g_mix[l].reshape(1, D_MODEL)
        g_ckv_l = g_ckv[l].reshape(1, C_KV_LORA)
        g_ffn_l = g_ffn[l].reshape(1, D_MODEL)
        sink_l = sink[l].reshape(1, A_HEADS)
        wout_a = w_out[l, :A_WIDTH].reshape(A_HEADS, HEAD_DIM, D_MODEL)[jnp.array(A_HEAD_ORDER)]
        wout_l = jnp.concatenate([wout_a.reshape(A_WIDTH, D_MODEL), w_out[l, A_WIDTH:]], axis=0).astype(BF16)
        final = l == DEPTH - 1

        (qa, ka, va, fb, qc, kc, vc, ska, sva, sckv, skr) = _projection(
            False, xp, mod_l, g_mix_l, win_p, gcq_p, wuq_p, g_ckv_l, wukv_p, None)
        r3 = lambda a: a.reshape(BATCH, SEQ, a.shape[-1])
        oa = _attn_a_ctx(sink_l, r3(qa), r3(ka), r3(va))
        ob = _fourier_ctx(r3(fb), fc)
        oc = _mla(r3(qc), r3(kc), r3(vc), None, None, SEQ, SEQ)
        f2 = lambda a: a.reshape(n_ctx, a.shape[-1])
        xp = _ffn(False, final, l, xp, f2(oa), f2(ob), f2(oc), mod_l, g_ffn_l, wout_l, wug_all, conv_w, conv_b_all,
                  wdown_all, g_final2)
        new_k.append(ska.reshape(BATCH, SEQ, A_KV_HEADS, HEAD_DIM))
        new_v.append(sva.reshape(BATCH, SEQ, A_KV_HEADS, HEAD_DIM))
        new_ckv.append(sckv.reshape(BATCH, SEQ, C_KV_LORA))
        new_kr.append(skr.reshape(BATCH, SEQ, C_ROPE))

        (qa, ka, va, fb, qc, kc, vc) = _projection(
            True, xs, mod_l, g_mix_l, win_p, gcq_p, wuq_p, g_ckv_l, wukv_p, tables)
        r3 = lambda a: a.reshape(DEC_BATCH, DEC_SEQ, a.shape[-1])
        kwin = cache_win_k[:, l].reshape(DEC_BATCH, PAST_LEN, A_KV_WIDTH).astype(BF16)
        vwin = cache_win_v[:, l].reshape(DEC_BATCH, PAST_LEN, A_KV_WIDTH).astype(BF16)
        oa = _attn_a_lat(sink_l, r3(qa), r3(ka), r3(va), kwin, vwin)
        ob = _fourier_lat(r3(fb), fc)
        kr_pad = jnp.pad(cache_mla_krope[:, l], ((0, 0), (0, 0), (KR_LANE0, LANES - KR_LANE0 - C_ROPE)))
        kc_ctx, vc_ctx = _kvcache(cache_mla_ckv[:, l], kr_pad, wukv_p)
        oc = _mla(r3(qc), r3(kc), r3(vc), kc_ctx, vc_ctx, TQ_MLA, TK_MLA)
        f2 = lambda a: a.reshape(n_lat, a.shape[-1])
        xs = _ffn(True, final, l, xs, f2(oa), f2(ob), f2(oc), mod_l, g_ffn_l, wout_l, wug_all, conv_w, conv_b_all,
                  wdown_all, g_final2)

    y_prompt = xp.reshape(BATCH, SEQ, D_MODEL)
    y_sample = xs.reshape(DEC_BATCH, DEC_SEQ, D_MODEL)
    return (y_prompt, y_sample, jnp.stack(new_k, axis=1), jnp.stack(new_v, axis=1),
            jnp.stack(new_ckv, axis=1), jnp.stack(new_kr, axis=1))
```

```python
import functools

import numpy as np
import jax
import jax.numpy as jnp
from jax import lax
from jax.experimental import pallas as pl
from jax.experimental.pallas import tpu as pltpu

F32 = jnp.float32
BF16 = jnp.bfloat16

D_MODEL = 1024
BATCH = 16
SEQ = 256
DEPTH = 2
DEC_BATCH = 4
DEC_SEQ = 4096
PAST_LEN = 256
GRID_W = 64
HEAD_DIM = 64
A_HEADS = 8
A_KV_HEADS = 2
A_GROUP = 4
A_WIDTH = 512
A_KV_WIDTH = 128
WINDOW = 128
BLOCK = 128
B_WIDTH = 256
B_GROUP_DIM = 64
B_GROUPS = 4
C_HEADS = 4
C_NOPE = 64
C_ROPE = 32
C_V = 64
C_Q_LORA = 192
C_KV_LORA = 128
C_WIDTH = 256
D_FF = 2816
ROPE_BASE = 10000.0
EPS = 1e-6
NEG = -1e30

LANES = 128
BF16_ROWS = 16
C_HEAD_PAD = 128
CW = C_HEADS * C_HEAD_PAD
CQ_PAD = 256
VMEM_LIMIT = 56 * 1024 * 1024

_QA0, _KA0, _VA0, _FB0, _CQ0, _CKV0, _KR0, _IN_PAD = 0, 512, 640, 768, 1024, 1280, 1408, 1536
KR_LANE0 = C_NOPE

T_PROJ = 1024
T_FFN = 1024
F_CHUNK = 256
TQ_MLA = 1024
TK_MLA = 2048
A_SUB = 8
A_HEAD_ORDER = [h for j in range(A_GROUP) for h in (j, A_GROUP + j)]
LOG2_E = float(np.log2(np.e))
FOURIER_COLS = 16


def _cparams(sem):
    return pltpu.CompilerParams(dimension_semantics=sem, vmem_limit_bytes=VMEM_LIMIT)


def _dot(a, b):
    return jnp.dot(a, b, preferred_element_type=F32)


def _dot_nt(a, b):
    return lax.dot_general(a, b, (((1,), (1,)), ((), ())), preferred_element_type=F32)


def _rms(x, g, n):
    ms = jnp.sum(x * x, axis=-1, keepdims=True) * (1.0 / n)
    return x * lax.rsqrt(ms + EPS) * g


def _split_bf16(a):
    hi = a.astype(BF16)
    return hi, (a - hi.astype(F32)).astype(BF16)


def _mod_kernel(c_ref, w_ref, b_ref, o_ref):
    cv = c_ref[...]
    s = cv * jax.nn.sigmoid(cv)
    s_hi, s_lo = _split_bf16(s)
    w_hi, w_lo = _split_bf16(w_ref[...])
    o_ref[...] = _dot(s_hi, w_hi) + (_dot(s_hi, w_lo) + _dot(s_lo, w_hi)) + b_ref[...]


def _modulation(cvecs, w_ada, b_ada):
    nj = 6
    return pl.pallas_call(
        _mod_kernel,
        grid=(DEPTH, nj),
        in_specs=[
            pl.BlockSpec((8, D_MODEL), lambda l, j: (0, 0)),
            pl.BlockSpec((None, D_MODEL, D_MODEL), lambda l, j: (l, 0, j)),
            pl.BlockSpec((None, 1, D_MODEL), lambda l, j: (l, 0, j)),
        ],
        out_specs=pl.BlockSpec((None, 8, D_MODEL), lambda l, j: (l, 0, j)),
        out_shape=jax.ShapeDtypeStruct((DEPTH, 8, 6 * D_MODEL), F32),
        compiler_params=_cparams(("arbitrary", "arbitrary")),
        name="modulation",
    )(cvecs, w_ada, b_ada.reshape(DEPTH, 1, 6 * D_MODEL))


def _rope_block(x, cos, sin, half):
    lane = lax.broadcasted_iota(jnp.int32, x.shape, 1)
    first = (lane % (2 * half)) < half
    partner = jnp.where(first, pltpu.roll(x, LANES - half, 1), pltpu.roll(x, half, 1))
    return x * cos + partner * sin


def _rope(x, cos, sin, half):
    blocks = [_rope_block(x[:, j:j + LANES], cos, sin, half) for j in range(0, x.shape[1], LANES)]
    return blocks[0] if len(blocks) == 1 else jnp.concatenate(blocks, axis=1)


def _proj_kernel(latent, x_ref, mod_ref, g_ref, win_ref, gcq_ref, wuq_ref, gckv_ref, wukv_ref, *rest):
    if latent:
        cosa_ref, sina_ref, cosc_ref, sinc_ref = rest[:4]
        qa_ref, ka_ref, va_ref, fb_ref, qc_ref, kc_ref, vc_ref = rest[4:]
    else:
        qa_ref, ka_ref, va_ref, fb_ref, qc_ref, kc_ref, vc_ref, ska_ref, sva_ref, sckv_ref, skr_ref = rest
    x = x_ref[...]
    y = _rms(x, g_ref[...], D_MODEL)
    sh1 = mod_ref[:, 0:D_MODEL]
    sc1 = mod_ref[:, D_MODEL:2 * D_MODEL]
    h = (y * (1.0 + sc1) + sh1).astype(BF16)

    qa = _dot(h, win_ref[:, _QA0:_KA0])
    kva = _dot(h, win_ref[:, _KA0:_FB0])
    ka, va = kva[:, :A_KV_WIDTH], kva[:, A_KV_WIDTH:]
    fb = _dot(h, win_ref[:, _FB0:_CQ0])
    cq = _dot(h, win_ref[:, _CQ0:_CKV0])
    ckr = _dot(h, win_ref[:, _CKV0:_IN_PAD])
    ckv, kr = ckr[:, :C_KV_LORA], ckr[:, C_KV_LORA:]

    if not latent:
        ska_ref[...] = ka
        sva_ref[...] = va
        skr_ref[...] = kr[:, KR_LANE0:KR_LANE0 + C_ROPE]
    else:
        qa = _rope(qa, cosa_ref[...], sina_ref[...], HEAD_DIM // 4)
        ka = _rope(ka, cosa_ref[...], sina_ref[...], HEAD_DIM // 4)
        kr = _rope(kr, cosc_ref[...], sinc_ref[...], C_ROPE // 4)
    qa_ref[...] = (qa * (LOG2_E * HEAD_DIM ** -0.5)).astype(BF16)
    ka_ref[...] = ka.T.astype(BF16)
    va_ref[...] = va.astype(BF16)
    fb_ref[...] = fb.astype(fb_ref.dtype)

    cqn = _rms(cq, gcq_ref[...], C_Q_LORA).astype(BF16)
    qc = _dot(cqn, wuq_ref[...])
    if latent:
        qc = _rope(qc, cosc_ref[...], sinc_ref[...], C_ROPE // 4)
    qc_ref[...] = (qc * (LOG2_E * (C_NOPE + C_ROPE) ** -0.5)).astype(BF16)

    ckvn = _rms(ckv, gckv_ref[...], C_KV_LORA)
    if not latent:
        sckv_ref[...] = ckvn
    _store_kv(_dot(ckvn.astype(BF16), wukv_ref[...]), kr, kc_ref, vc_ref)


def _store_kv(kv, kr, kc_ref, vc_ref):
    kc_ref[...] = (kv[:, :CW] + jnp.concatenate([kr] * C_HEADS, axis=1)).astype(BF16)
    lane = lax.broadcasted_iota(jnp.int32, (1, CW), 1)
    upper = lane % C_HEAD_PAD >= C_V
    odd = (lane // C_HEAD_PAD) % 2 == 1
    vc_ref[...] = (kv[:, CW:] + jnp.where(upper != odd, 1.0, 0.0).astype(F32)).astype(BF16)


def _projection(latent, x, mod_l, g_mix, win_p, gcq_p, wuq_p, gckv, wukv_p, tables):
    n = x.shape[0]
    T = T_PROJ
    nt = n // T
    tpb = (DEC_SEQ // T) if latent else nt
    base = 1 if latent else 0
    tok = lambda w: pl.BlockSpec((T, w), lambda i: (i, 0))
    full = lambda a: pl.BlockSpec(a.shape, lambda i: (0,) * a.ndim)
    in_specs = [
        tok(D_MODEL),
        pl.BlockSpec((None, 1, 6 * D_MODEL), lambda i: (base + i // tpb, 0, 0)),
        full(g_mix), full(win_p), full(gcq_p), full(wuq_p), full(gckv), full(wukv_p),
    ]
    args = [x, mod_l, g_mix, win_p, gcq_p, wuq_p, gckv, wukv_p]
    if latent:
        in_specs += [pl.BlockSpec((T, LANES), lambda i: (i % tpb, 0))] * 4
        args += list(tables)
    widths = [A_WIDTH, A_KV_WIDTH, A_KV_WIDTH, B_WIDTH, CW, CW, CW]
    out_specs = [tok(w) for w in widths]
    out_shape = [jax.ShapeDtypeStruct((n, w), BF16) for w in widths]
    out_specs[1] = pl.BlockSpec((A_KV_WIDTH, T), lambda i: (0, i))
    out_shape[1] = jax.ShapeDtypeStruct((A_KV_WIDTH, n), BF16)
    if latent:
        out_shape[3] = jax.ShapeDtypeStruct((n, B_WIDTH), F32)
    if not latent:
        sw = [A_KV_WIDTH, A_KV_WIDTH, C_KV_LORA, C_ROPE]
        out_specs += [tok(w) for w in sw]
        out_shape += [jax.ShapeDtypeStruct((n, w), F32) for w in sw]
    return pl.pallas_call(
        functools.partial(_proj_kernel, latent),
        grid=(nt,),
        in_specs=in_specs,
        out_specs=out_specs,
        out_shape=out_shape,
        compiler_params=_cparams(("arbitrary",)),
        name="proj_lat" if latent else "proj_ctx",
    )(*args)


def _kvcache_kernel(ckv_ref, kr_ref, wukv_ref, kc_ref, vc_ref):
    _store_kv(_dot(ckv_ref[...].astype(BF16), wukv_ref[...]), kr_ref[...], kc_ref, vc_ref)


def _kvcache(ckv, kr_pad, wukv_p):
    B, P, _ = ckv.shape
    return pl.pallas_call(
        _kvcache_kernel,
        grid=(B,),
        in_specs=[
            pl.BlockSpec((None, P, C_KV_LORA), lambda b: (b, 0, 0)),
            pl.BlockSpec((None, P, LANES), lambda b: (b, 0, 0)),
            pl.BlockSpec(wukv_p.shape, lambda b: (0, 0)),
        ],
        out_specs=[pl.BlockSpec((None, P, CW), lambda b: (b, 0, 0))] * 2,
        out_shape=[jax.ShapeDtypeStruct((B, P, CW), BF16)] * 2,
        compiler_params=_cparams(("arbitrary",)),
        name="kvcache",
    )(ckv, kr_pad, wukv_p)


def _gqa_group(g, q_blocks, keys, masks, values, sink_ref):
    tq = q_blocks[0].shape[0]
    lane = lax.broadcasted_iota(jnp.int32, (tq, LANES), 1)
    mine = (lane >= HEAD_DIM) if g else (lane < HEAD_DIM)
    qs = jnp.concatenate([jnp.where(mine, qb, jnp.zeros_like(qb)) for qb in q_blocks], axis=0)
    parts = []
    for k, mask in zip(keys, masks):
        s = _dot(qs, k)
        parts.append(s if mask is None else jnp.where(mask, s, NEG))
    s = parts[0] if len(parts) == 1 else jnp.concatenate(parts, axis=1)
    rowblk = lax.broadcasted_iota(jnp.int32, (A_GROUP * tq, 1), 0) // tq
    sk = jnp.full((A_GROUP * tq, 1), sink_ref[0, g * A_GROUP] * LOG2_E, F32)
    for j in range(1, A_GROUP):
        sk = jnp.where(rowblk == j, sink_ref[0, g * A_GROUP + j] * LOG2_E, sk)
    m = jnp.maximum(jnp.max(s, axis=-1, keepdims=True), sk)
    p = jnp.exp2(s - m)
    l = jnp.sum(p, axis=-1, keepdims=True) + jnp.exp2(sk - m)
    v = values[0] if len(values) == 1 else jnp.concatenate(values, axis=0)
    return _dot(p.astype(BF16), v) * (1.0 / l)


def _gqa_store(o_ref, rows, tq, o0, o1):
    low = lax.broadcasted_iota(jnp.int32, (tq, LANES), 1) < HEAD_DIM
    for j in range(A_GROUP):
        blk = jnp.where(low, o0[j * tq:(j + 1) * tq], o1[j * tq:(j + 1) * tq])
        o_ref[rows, j * LANES:(j + 1) * LANES] = blk.astype(BF16)


def _attn_a_ctx_kernel(sink_ref, q_ref, k_ref, v_ref, o_ref):
    n = q_ref.shape[0]
    q_blocks = [q_ref[:, j * LANES:(j + 1) * LANES] for j in range(A_GROUP)]
    outs = [_gqa_group(g, q_blocks, [k_ref[...]], [None], [v_ref[...]], sink_ref) for g in range(A_KV_HEADS)]
    _gqa_store(o_ref, slice(None), n, *outs)


def _attn_a_ctx(sink, q, k, v):
    B, n, _ = q.shape
    blk = lambda w: pl.BlockSpec((None, n, w), lambda b: (b, 0, 0))
    return pl.pallas_call(
        _attn_a_ctx_kernel,
        grid=(B,),
        in_specs=[pl.BlockSpec(memory_space=pltpu.SMEM), blk(A_WIDTH),
                  pl.BlockSpec((A_KV_WIDTH, n), lambda b: (0, b)), blk(A_KV_WIDTH)],
        out_specs=blk(A_WIDTH),
        out_shape=jax.ShapeDtypeStruct((B, n, A_WIDTH), BF16),
        compiler_params=_cparams(("arbitrary",)),
        name="attn_a_ctx",
    )(sink, q, k, v)


def _attn_a_lat_kernel(sink_ref, q_ref, k_ref, v_ref, kc_ref, vc_ref, o_ref):
    n = k_ref.shape[1]
    band = 3 * BLOCK
    kcx = kc_ref[...]
    vcx = vc_ref[...]

    def body(sb, carry):
        blk = pl.program_id(1) * A_SUB + sb
        rows = pl.ds(pl.multiple_of(sb * BLOCK, BLOCK), BLOCK)
        start = pl.multiple_of(jnp.clip((blk - 1) * BLOCK, 0, n - band), BLOCK)
        kb = k_ref[:, pl.ds(start, band)]
        vb = v_ref[pl.ds(start, band), :]
        qpos = blk * BLOCK + lax.broadcasted_iota(jnp.int32, (BLOCK, band), 0)
        kpos = start + lax.broadcasted_iota(jnp.int32, (BLOCK, band), 1)
        mask = jnp.concatenate([jnp.abs(qpos - kpos) <= WINDOW] * A_GROUP, axis=0)
        q_blocks = [q_ref[rows, j * LANES:(j + 1) * LANES] for j in range(A_GROUP)]
        outs = [_gqa_group(g, q_blocks, [kb, kcx], [mask, None], [vb, vcx], sink_ref)
                for g in range(A_KV_HEADS)]
        _gqa_store(o_ref, rows, BLOCK, *outs)
        return carry

    lax.fori_loop(0, A_SUB, body, 0, unroll=True)


def _attn_a_lat(sink, q, k, v, k_ctx, v_ctx):
    B, n, _ = q.shape
    P = v_ctx.shape[1]
    tq = A_SUB * BLOCK
    seq = lambda w, m: pl.BlockSpec((None, m, w), lambda b, i: (b, 0, 0))
    seq_t = lambda w, m: pl.BlockSpec((w, m), lambda b, i: (0, b))
    return pl.pallas_call(
        _attn_a_lat_kernel,
        grid=(B, n // tq),
        in_specs=[pl.BlockSpec(memory_space=pltpu.SMEM),
                  pl.BlockSpec((None, tq, A_WIDTH), lambda b, i: (b, i, 0)),
                  seq_t(A_KV_WIDTH, n), seq(A_KV_WIDTH, n), seq_t(A_KV_WIDTH, P), seq(A_KV_WIDTH, P)],
        out_specs=pl.BlockSpec((None, tq, A_WIDTH), lambda b, i: (b, i, 0)),
        out_shape=jax.ShapeDtypeStruct((B, n, A_WIDTH), BF16),
        compiler_params=_cparams(("arbitrary", "arbitrary")),
        name="attn_a_lat",
    )(sink, q, k, v, k_ctx, v_ctx)


def _mla_step(q_ref, k, v, m_scr, acc_scr):
    reps = k.shape[0] // LANES
    for h in range(C_HEADS):
        hs = slice(h * C_HEAD_PAD, (h + 1) * C_HEAD_PAD)
        s = _dot_nt(q_ref[:, hs], k[:, hs])
        m_old = m_scr[h]
        m_new = jnp.maximum(m_old, jnp.max(s, axis=-1, keepdims=True))
        alpha = jnp.exp2(m_old - m_new)
        p = jnp.exp2(s - jnp.concatenate([m_new] * reps, axis=1))
        m_scr[h] = m_new
        acc_scr[h] = acc_scr[h] * alpha + _dot(p.astype(BF16), v[:, hs])


def _mla_kernel(n_chunks, tk, has_ctx, q_ref, k_ref, v_ref, *rest):
    if has_ctx:
        kc_ref, vc_ref, o_ref, m_scr, acc_scr = rest
    else:
        o_ref, m_scr, acc_scr = rest
    tq = q_ref.shape[0]
    m_scr[...] = jnp.full(m_scr.shape, -jnp.inf, F32)
    acc_scr[...] = jnp.zeros(acc_scr.shape, F32)

    def body(c, carry):
        ks = pl.multiple_of(c * tk, tk)
        _mla_step(q_ref, k_ref[pl.ds(ks, tk), :], v_ref[pl.ds(ks, tk), :], m_scr, acc_scr)
        return carry

    if has_ctx:
        _mla_step(q_ref, kc_ref[...], vc_ref[...], m_scr, acc_scr)
    lax.fori_loop(0, n_chunks, body, 0)

    low = lax.broadcasted_iota(jnp.int32, (tq, LANES), 1) < C_V
    for j in range(C_HEADS // 2):
        even, odd = acc_scr[2 * j], acc_scr[2 * j + 1]
        sums = pltpu.roll(jnp.where(low, odd, even), C_V, 1)
        o_ref[:, j * LANES:(j + 1) * LANES] = (jnp.where(low, even, odd) / sums).astype(BF16)


def _mla(q, k, v, k_ctx, v_ctx, tq, tk):
    B, n, _ = q.shape
    nk = k.shape[1]
    has_ctx = k_ctx is not None
    seq = lambda m: pl.BlockSpec((None, m, CW), lambda b, i: (b, 0, 0))
    in_specs = [pl.BlockSpec((None, tq, CW), lambda b, i: (b, i, 0)), seq(nk), seq(nk)]
    args = [q, k, v]
    if has_ctx:
        in_specs += [seq(k_ctx.shape[1])] * 2
        args += [k_ctx, v_ctx]
    return pl.pallas_call(
        functools.partial(_mla_kernel, nk // tk, tk, has_ctx),
        grid=(B, n // tq),
        in_specs=in_specs,
        out_specs=pl.BlockSpec((None, tq, C_WIDTH), lambda b, i: (b, i, 0)),
        out_shape=jax.ShapeDtypeStruct((B, n, C_WIDTH), BF16),
        scratch_shapes=[pltpu.VMEM((C_HEADS, tq, LANES), F32), pltpu.VMEM((C_HEADS, tq, LANES), F32)],
        compiler_params=_cparams(("arbitrary", "arbitrary")),
        name="mla_lat" if has_ctx else "mla_ctx",
    )(*args)


def _dft_tables(n):
    j = np.arange(n, dtype=np.int64)
    ang = ((j[:, None] * j[None, :]) % n) * (2.0 * np.pi / n)
    return np.cos(ang), np.sin(ang)


def _channel_dft():
    c, s = _dft_tables(B_GROUP_DIM)
    eye = np.eye(B_GROUPS)
    return _table_bf16(np.concatenate([np.kron(eye, c), -np.kron(eye, s)], axis=1))


def _table_bf16(t):
    return jnp.asarray(t, F32).astype(BF16)


def _fourier_ctx_kernel(scale, x_ref, fc_ref, fn_ref, o_ref):
    u = _dot(x_ref[...], fc_ref[...])
    ucat = jnp.concatenate([u[:, :B_WIDTH], u[:, B_WIDTH:]], axis=0).astype(BF16)
    o_ref[...] = (_dot(fn_ref[...], ucat) * scale).astype(BF16)


def _fourier_ctx(fb, fc):
    B, n, _ = fb.shape
    c, s = _dft_tables(n)
    fn = _table_bf16(np.concatenate([c, s], axis=1))
    scale = float((n * B_GROUP_DIM) ** -0.5)
    return pl.pallas_call(
        functools.partial(_fourier_ctx_kernel, scale),
        grid=(B,),
        in_specs=[pl.BlockSpec((None, n, B_WIDTH), lambda b: (b, 0, 0)),
                  pl.BlockSpec(fc.shape, lambda b: (0, 0)),
                  pl.BlockSpec(fn.shape, lambda b: (0, 0))],
        out_specs=pl.BlockSpec((None, n, B_WIDTH), lambda b: (b, 0, 0)),
        out_shape=jax.ShapeDtypeStruct((B, n, B_WIDTH), BF16),
        compiler_params=_cparams(("arbitrary",)),
        name="fourier_ctx",
    )(fb, fc, fn)


def _fourier_lat_kernel(scale, x_ref, fc_ref, g_ref, f_ref, o_ref, zr_scr, zi_scr):
    j = pl.program_id(1)
    nc = pl.num_programs(1) // 2

    @pl.when(j < nc)
    def _():
        for cc in range(FOURIER_COLS):
            xc = x_ref[:, cc, :].astype(BF16)
            u = _dot(xc, fc_ref[...]).astype(BF16)
            p = _dot(g_ref[cc], u)
            R = p.shape[0] // 2
            zshape = (R // FOURIER_COLS, FOURIER_COLS, B_WIDTH)
            zr_scr[j * FOURIER_COLS + cc] = (p[:R, :B_WIDTH] - p[R:, B_WIDTH:]).reshape(zshape)
            zi_scr[j * FOURIER_COLS + cc] = (p[:R, B_WIDTH:] + p[R:, :B_WIDTH]).reshape(zshape)

    @pl.when(j >= nc)
    def _():
        for kk in range(FOURIER_COLS):
            z = jnp.concatenate([zr_scr[:, j - nc, kk, :], zi_scr[:, j - nc, kk, :]], axis=0).astype(BF16)
            o_ref[:, kk, :] = _dot(f_ref[...], z) * scale


def _fourier_lat(fb, fc):
    B, n, _ = fb.shape
    R = n // GRID_W
    W = GRID_W * B_WIDTH
    kr = np.arange(R, dtype=np.int64)
    pos = GRID_W * np.arange(R, dtype=np.int64)[None, None, :] + np.arange(GRID_W, dtype=np.int64)[:, None, None]
    ang = ((kr[None, :, None] * pos) % n) * (2.0 * np.pi / n)
    g = _table_bf16(np.concatenate([np.cos(ang), -np.sin(ang)], axis=1))
    c64, s64 = _dft_tables(GRID_W)
    f2 = _table_bf16(np.concatenate([c64, s64], axis=1))
    assert R == GRID_W
    nc = GRID_W // FOURIER_COLS
    scale = float((n * B_GROUP_DIM) ** -0.5)
    blk = (None, GRID_W, FOURIER_COLS, B_WIDTH)
    zscr = pltpu.VMEM((GRID_W, R // FOURIER_COLS, FOURIER_COLS, B_WIDTH), F32)
    out = pl.pallas_call(
        functools.partial(_fourier_lat_kernel, scale),
        grid=(B, 2 * nc),
        in_specs=[pl.BlockSpec(blk, lambda b, j: (b, 0, jnp.minimum(j, nc - 1), 0)),
                  pl.BlockSpec(fc.shape, lambda b, j: (0, 0)),
                  pl.BlockSpec((FOURIER_COLS, 2 * R, R), lambda b, j: (jnp.minimum(j, nc - 1), 0, 0)),
                  pl.BlockSpec(f2.shape, lambda b, j: (0, 0))],
        out_specs=pl.BlockSpec(blk, lambda b, j: (b, 0, jnp.maximum(j - nc, 0), 0)),
        out_shape=jax.ShapeDtypeStruct((B, GRID_W, R, B_WIDTH), F32),
        scratch_shapes=[zscr, zscr],
        compiler_params=_cparams(("arbitrary", "arbitrary")),
        name="fourier_lat",
    )(fb.reshape(B, R, GRID_W, B_WIDTH), fc, g, f2)
    return out.reshape(B, n, B_WIDTH)


def _ffn_kernel(T, S, final, x_ref, xp_ref, xn_ref, a_ref, ap_ref, an_ref, b_ref, bp_ref, bn_ref, c_ref, cp_ref,
                cn_ref, mod_ref, g_ref, wout_ref, wug_ref, cw_ref, cb_ref, wd_ref, gfin_ref,
                o_ref, mix_scr, hs_scr, u0_scr, u1_scr, act_scr):
    i = pl.program_id(0)
    H = BF16_ROWS
    for r0, refs in ((0, (ap_ref, bp_ref, cp_ref)), (H, (a_ref, b_ref, c_ref)), (T + H, (an_ref, bn_ref, cn_ref))):
        rows = slice(r0, r0 + refs[0].shape[0])
        mix_scr[rows, 0:A_WIDTH] = refs[0][...]
        mix_scr[rows, A_WIDTH:A_WIDTH + B_WIDTH] = refs[1][...].astype(BF16)
        mix_scr[rows, A_WIDTH + B_WIDTH:] = refs[2][...]
    mix = _dot(mix_scr[...], wout_ref[...])
    gt1 = mod_ref[:, 2 * D_MODEL:3 * D_MODEL]
    sh2 = mod_ref[:, 3 * D_MODEL:4 * D_MODEL]
    sc2 = mod_ref[:, 4 * D_MODEL:5 * D_MODEL]

    def ffn_input(x, m):
        x1 = x + gt1 * m
        return x1, (_rms(x1, g_ref[...], D_MODEL) * (1.0 + sc2) + sh2).astype(BF16)

    at_start = (i * T) % S == 0
    at_end = ((i + 1) * T) % S == 0
    hp = ffn_input(xp_ref[...], mix[0:H])[1]
    hn = ffn_input(xn_ref[...], mix[T + H:T + 2 * H])[1]
    x1, hm = ffn_input(x_ref[...], mix[H:T + H])
    o_ref[...] = x1
    hs_scr[0:H] = jnp.where(at_start, jnp.zeros_like(hp), hp)
    hs_scr[H:T + H] = hm
    hs_scr[T + H:T + 2 * H] = jnp.where(at_end, jnp.zeros_like(hn), hn)
    if S < T:
        pos = lax.broadcasted_iota(jnp.int32, (T, 1), 0) % S
        has_prev = pos != 0
        has_next = pos != S - 1

    n_chunks = D_FF // F_CHUNK
    cols = lambda j: (pl.multiple_of(j * F_CHUNK, F_CHUNK), pl.multiple_of(j * F_CHUNK + D_FF, LANES))

    def up_proj(j, u_ref):
        for b, col in enumerate(cols(j)):
            u_ref[b] = _dot(hs_scr[...], wug_ref[:, pl.ds(col, F_CHUNK)])

    def conv(u_ref, b, col):
        u = u_ref[b]
        rows = u.shape[0]
        up = pltpu.roll(u, 1, 0)[H:T + H]
        un = pltpu.roll(u, rows - 1, 0)[H:T + H]
        if S < T:
            up = jnp.where(has_prev, up, 0.0)
            un = jnp.where(has_next, un, 0.0)
        cw = cw_ref[:, pl.ds(col, F_CHUNK)]
        return (up * cw[0:1, :] + u[H:T + H] * cw[1:2, :] + un * cw[2:3, :]
                + cb_ref[:, pl.ds(col, F_CHUNK)])

    def gate(j, u_ref):
        a, g = [conv(u_ref, b, col) for b, col in enumerate(cols(j))]
        act_scr[:, pl.ds(cols(j)[0], F_CHUNK)] = (g * jax.nn.sigmoid(g) * a).astype(BF16)

    def pair(k, carry):
        up_proj(2 * k + 1, u1_scr)
        gate(2 * k, u0_scr)
        up_proj(2 * k + 2, u0_scr)
        gate(2 * k + 1, u1_scr)
        return carry

    assert n_chunks % 2 == 1
    up_proj(0, u0_scr)
    lax.fori_loop(0, n_chunks // 2, pair, 0, unroll=True)
    gate(n_chunks - 1, u0_scr)
    gt2 = mod_ref[:, 5 * D_MODEL:6 * D_MODEL]
    x2 = o_ref[...] + gt2 * _dot(act_scr[...], wd_ref[...])
    if final:
        x2 = _rms(x2, gfin_ref[...], D_MODEL)
    o_ref[...] = x2


def _ffn(latent, final, layer, x, oa, ob, oc, mod_l, g_ffn, wout, wug, conv_w, conv_b, wdown, g_final):
    n = x.shape[0]
    T = T_FFN
    S = DEC_SEQ if latent else SEQ
    nt = n // T
    tpb = (DEC_SEQ // T) if latent else nt
    base = 1 if latent else 0
    hb = T // BF16_ROWS
    nhb = n // BF16_ROWS
    once = dict(pipeline_mode=pl.Buffered(1))
    whole = lambda a: pl.BlockSpec(a.shape, lambda i: (0,) * a.ndim, **once)
    of_layer = lambda a: pl.BlockSpec((None,) + a.shape[1:], lambda i: (layer,) + (0,) * (a.ndim - 1), **once)
    with_halos = lambda w: [
        pl.BlockSpec((T, w), lambda i: (i, 0)),
        pl.BlockSpec((BF16_ROWS, w), lambda i: (jnp.maximum(i * hb - 1, 0), 0)),
        pl.BlockSpec((BF16_ROWS, w), lambda i: (jnp.minimum((i + 1) * hb, nhb - 1), 0))]
    return pl.pallas_call(
        functools.partial(_ffn_kernel, T, S, final),
        grid=(nt,),
        in_specs=[spec for a in (x, oa, ob, oc) for spec in with_halos(a.shape[1])] + [
            pl.BlockSpec((None, 1, 6 * D_MODEL), lambda i: (base + i // tpb, 0, 0)),
            whole(g_ffn), whole(wout),
            of_layer(wug), of_layer(conv_w), of_layer(conv_b), of_layer(wdown), whole(g_final),
        ],
        out_specs=pl.BlockSpec((T, D_MODEL), lambda i: (i, 0)),
        out_shape=jax.ShapeDtypeStruct((n, D_MODEL), F32),
        scratch_shapes=[pltpu.VMEM((T + 2 * BF16_ROWS, D_MODEL), BF16),
                        pltpu.VMEM((T + 2 * BF16_ROWS, D_MODEL), BF16),
                        pltpu.VMEM((2, T + 2 * BF16_ROWS, F_CHUNK), F32),
                        pltpu.VMEM((2, T + 2 * BF16_ROWS, F_CHUNK), F32),
                        pltpu.VMEM((T, D_FF), BF16)],
        compiler_params=_cparams(("arbitrary",)),
        name="ffn_lat" if latent else "ffn_ctx",
    )(x, x, x, oa, oa, oa, ob, ob, ob, oc, oc, oc, mod_l, g_ffn, wout, wug, conv_w, conv_b, wdown, g_final)


def _rope_tables(n_tok, dim, lane0, width):
    rows = n_tok // GRID_W
    r = np.repeat(np.arange(rows), GRID_W).astype(np.float64)
    col = np.tile(np.arange(GRID_W), rows).astype(np.float64)
    quarter = dim // 4
    inv = ROPE_BASE ** (-np.arange(quarter, dtype=np.float64) / quarter)
    ang_r = r[:, None] * inv
    ang_c = col[:, None] * inv
    cos = np.concatenate([np.cos(ang_r)] * 2 + [np.cos(ang_c)] * 2, axis=1)
    sin = np.concatenate([-np.sin(ang_r), np.sin(ang_r), -np.sin(ang_c), np.sin(ang_c)], axis=1)
    reps = width // dim
    cos = np.concatenate([cos] * reps, axis=1)
    sin = np.concatenate([sin] * reps, axis=1)
    pad = ((0, 0), (lane0, LANES - lane0 - width))
    return jnp.asarray(np.pad(cos, pad, constant_values=1.0), F32), jnp.asarray(np.pad(sin, pad), F32)


def _layer_weights(w_in, g_cq, w_uq, w_ukv):
    z = lambda r, c: jnp.zeros((r, c), F32)
    o = np.cumsum([0, A_WIDTH, A_KV_WIDTH, A_KV_WIDTH, B_WIDTH, C_Q_LORA, C_KV_LORA, C_ROPE])
    win_p = jnp.concatenate([
        w_in[:, o[0]:o[1]].reshape(D_MODEL, A_HEADS, HEAD_DIM)[:, A_HEAD_ORDER, :].reshape(D_MODEL, A_WIDTH),
        w_in[:, o[1]:o[4]],
        w_in[:, o[4]:o[5]], z(D_MODEL, CQ_PAD - C_Q_LORA),
        w_in[:, o[5]:o[6]],
        z(D_MODEL, KR_LANE0), w_in[:, o[6]:o[7]], z(D_MODEL, LANES - KR_LANE0 - C_ROPE),
    ], axis=1).astype(BF16)
    gcq_p = jnp.pad(g_cq, (0, CQ_PAD - C_Q_LORA)).reshape(1, CQ_PAD)
    hq = C_NOPE + C_ROPE
    wuq_h = w_uq.reshape(C_Q_LORA, C_HEADS, hq)
    wuq_p = jnp.pad(wuq_h, ((0, CQ_PAD - C_Q_LORA), (0, 0), (0, C_HEAD_PAD - hq)))
    wuq_p = wuq_p.reshape(CQ_PAD, C_HEADS * C_HEAD_PAD).astype(BF16)
    wukv_h = w_ukv.reshape(C_KV_LORA, C_HEADS, C_NOPE + C_V)
    wk = jnp.pad(wukv_h[:, :, :C_NOPE], ((0, 0), (0, 0), (0, C_HEAD_PAD - C_NOPE)))
    zv = jnp.zeros((C_KV_LORA, C_HEAD_PAD - C_V), F32)
    wv = [wukv_h[:, h, C_NOPE:] for h in range(C_HEADS)]
    wv = jnp.concatenate([jnp.concatenate([zv, w] if h % 2 else [w, zv], axis=1) for h, w in enumerate(wv)], axis=1)
    wukv_p = jnp.concatenate([wk.reshape(C_KV_LORA, CW), wv], axis=1).astype(BF16)
    return win_p, gcq_p, wuq_p, wukv_p


def kernel(x_prompt, x_sample, cache_win_k, cache_win_v, cache_mla_ckv, cache_mla_krope, c, c_ctx,
           w_ada, b_ada, g_mix, w_in, sink, g_cq, w_uq, g_ckv, w_ukv, w_out, g_ffn, w_ug, conv_w,
           conv_b, w_down, g_final):
    n_ctx = BATCH * SEQ
    n_lat = DEC_BATCH * DEC_SEQ
    xp = x_prompt.reshape(n_ctx, D_MODEL)
    xs = x_sample.reshape(n_lat, D_MODEL)

    cvecs = jnp.concatenate([c_ctx[None, :], c, jnp.zeros((8 - 1 - DEC_BATCH, D_MODEL), F32)], axis=0)
    mod = _modulation(cvecs, w_ada, b_ada)

    cos_a, sin_a = _rope_tables(DEC_SEQ, HEAD_DIM, 0, LANES)
    cos_c, sin_c = _rope_tables(DEC_SEQ, C_ROPE, KR_LANE0, C_ROPE)
    tables = (cos_a, sin_a, cos_c, sin_c)
    fc = _channel_dft()
    g_final2 = g_final.reshape(1, D_MODEL)
    wug_all = w_ug.astype(BF16)
    wdown_all = w_down.astype(BF16)
    conv_b_all = conv_b.reshape(DEPTH, 1, 2 * D_FF)

    new_k, new_v, new_ckv, new_kr = [], [], [], []
    for l in range(DEPTH):
        win_p, gcq_p, wuq_p, wukv_p = _layer_weights(w_in[l], g_cq[l], w_uq[l], w_ukv[l])
        mod_l = mod[l].reshape(8, 1, 6 * D_MODEL)
        g_mix_l = g_mix[l].reshape(1, D_MODEL)
        g_ckv_l = g_ckv[l].reshape(1, C_KV_LORA)
        g_ffn_l = g_ffn[l].reshape(1, D_MODEL)
        sink_l = sink[l].reshape(1, A_HEADS)
        wout_a = w_out[l, :A_WIDTH].reshape(A_HEADS, HEAD_DIM, D_MODEL)[jnp.array(A_HEAD_ORDER)]
        wout_l = jnp.concatenate([wout_a.reshape(A_WIDTH, D_MODEL), w_out[l, A_WIDTH:]], axis=0).astype(BF16)
        final = l == DEPTH - 1

        (qa, ka, va, fb, qc, kc, vc, ska, sva, sckv, skr) = _projection(
            False, xp, mod_l, g_mix_l, win_p, gcq_p, wuq_p, g_ckv_l, wukv_p, None)
        r3 = lambda a: a.reshape(BATCH, SEQ, a.shape[-1])
        oa = _attn_a_ctx(sink_l, r3(qa), ka, r3(va))
        ob = _fourier_ctx(r3(fb), fc)
        oc = _mla(r3(qc), r3(kc), r3(vc), None, None, SEQ, SEQ)
        f2 = lambda a: a.reshape(n_ctx, a.shape[-1])
        xp = _ffn(False, final, l, xp, f2(oa), f2(ob), f2(oc), mod_l, g_ffn_l, wout_l, wug_all, conv_w, conv_b_all,
                  wdown_all, g_final2)
        new_k.append(ska.reshape(BATCH, SEQ, A_KV_HEADS, HEAD_DIM))
        new_v.append(sva.reshape(BATCH, SEQ, A_KV_HEADS, HEAD_DIM))
        new_ckv.append(sckv.reshape(BATCH, SEQ, C_KV_LORA))
        new_kr.append(skr.reshape(BATCH, SEQ, C_ROPE))

        (qa, ka, va, fb, qc, kc, vc) = _projection(
            True, xs, mod_l, g_mix_l, win_p, gcq_p, wuq_p, g_ckv_l, wukv_p, tables)
        r3 = lambda a: a.reshape(DEC_BATCH, DEC_SEQ, a.shape[-1])
        kwin = cache_win_k[:, l].reshape(DEC_BATCH * PAST_LEN, A_KV_WIDTH).T.astype(BF16)
        vwin = cache_win_v[:, l].reshape(DEC_BATCH, PAST_LEN, A_KV_WIDTH).astype(BF16)
        oa = _attn_a_lat(sink_l, r3(qa), ka, r3(va), kwin, vwin)
        ob = _fourier_lat(r3(fb), fc)
        kr_pad = jnp.pad(cache_mla_krope[:, l], ((0, 0), (0, 0), (KR_LANE0, LANES - KR_LANE0 - C_ROPE)))
        kc_ctx, vc_ctx = _kvcache(cache_mla_ckv[:, l], kr_pad, wukv_p)
        oc = _mla(r3(qc), r3(kc), r3(vc), kc_ctx, vc_ctx, TQ_MLA, TK_MLA)
        f2 = lambda a: a.reshape(n_lat, a.shape[-1])
        xs = _ffn(True, final, l, xs, f2(oa), f2(ob), f2(oc), mod_l, g_ffn_l, wout_l, wug_all, conv_w, conv_b_all,
                  wdown_all, g_final2)

    y_prompt = xp.reshape(BATCH, SEQ, D_MODEL)
    y_sample = xs.reshape(DEC_BATCH, DEC_SEQ, D_MODEL)
    return (y_prompt, y_sample, jnp.stack(new_k, axis=1), jnp.stack(new_v, axis=1),
            jnp.stack(new_ckv, axis=1), jnp.stack(new_kr, axis=1))
```

```python
import functools

import numpy as np
import jax
import jax.numpy as jnp
from jax import lax
from jax.experimental import pallas as pl
from jax.experimental.pallas import tpu as pltpu

F32 = jnp.float32
BF16 = jnp.bfloat16

D_MODEL = 1024
BATCH = 16
SEQ = 256
DEPTH = 2
DEC_BATCH = 4
DEC_SEQ = 4096
PAST_LEN = 256
GRID_W = 64
HEAD_DIM = 64
A_HEADS = 8
A_KV_HEADS = 2
A_GROUP = 4
A_WIDTH = 512
A_KV_WIDTH = 128
WINDOW = 128
BLOCK = 128
B_WIDTH = 256
B_GROUP_DIM = 64
B_GROUPS = 4
C_HEADS = 4
C_NOPE = 64
C_ROPE = 32
C_V = 64
C_Q_LORA = 192
C_KV_LORA = 128
C_WIDTH = 256
D_FF = 2816
ROPE_BASE = 10000.0
EPS = 1e-6
NEG = -1e30

LANES = 128
BF16_ROWS = 16
C_HEAD_PAD = 128
CW = C_HEADS * C_HEAD_PAD
CQ_PAD = 256
VMEM_LIMIT = 56 * 1024 * 1024

_QA0, _KA0, _VA0, _FB0, _CQ0, _CKV0, _KR0, _IN_PAD = 0, 512, 640, 768, 1024, 1280, 1408, 1536
KR_LANE0 = C_NOPE

T_PROJ = 1024
T_FFN = 1024
F_CHUNK = 256
TQ_MLA = 1024
TK_MLA = 2048
A_SUB = 8
CTX_GROUP = 4
A_HEAD_ORDER = [h for j in range(A_GROUP) for h in (j, A_GROUP + j)]
LOG2_E = float(np.log2(np.e))
FOURIER_COLS = 16


def _cparams(sem):
    return pltpu.CompilerParams(dimension_semantics=sem, vmem_limit_bytes=VMEM_LIMIT)


def _dot(a, b):
    return jnp.dot(a, b, preferred_element_type=F32)


def _dot_nt(a, b):
    return lax.dot_general(a, b, (((1,), (1,)), ((), ())), preferred_element_type=F32)


def _rms(x, g, n):
    ms = jnp.sum(x * x, axis=-1, keepdims=True) * (1.0 / n)
    return x * lax.rsqrt(ms + EPS) * g


def _split_bf16(a):
    hi = a.astype(BF16)
    return hi, (a - hi.astype(F32)).astype(BF16)


def _mod_kernel(c_ref, w_ref, b_ref, o_ref):
    cv = c_ref[...]
    s = cv * jax.nn.sigmoid(cv)
    s_hi, s_lo = _split_bf16(s)
    w_hi, w_lo = _split_bf16(w_ref[...])
    o_ref[...] = _dot(s_hi, w_hi) + (_dot(s_hi, w_lo) + _dot(s_lo, w_hi)) + b_ref[...]


def _modulation(cvecs, w_ada, b_ada):
    nj = 6
    return pl.pallas_call(
        _mod_kernel,
        grid=(DEPTH, nj),
        in_specs=[
            pl.BlockSpec((8, D_MODEL), lambda l, j: (0, 0)),
            pl.BlockSpec((None, D_MODEL, D_MODEL), lambda l, j: (l, 0, j)),
            pl.BlockSpec((None, 1, D_MODEL), lambda l, j: (l, 0, j)),
        ],
        out_specs=pl.BlockSpec((None, 8, D_MODEL), lambda l, j: (l, 0, j)),
        out_shape=jax.ShapeDtypeStruct((DEPTH, 8, 6 * D_MODEL), F32),
        compiler_params=_cparams(("arbitrary", "arbitrary")),
        name="modulation",
    )(cvecs, w_ada, b_ada.reshape(DEPTH, 1, 6 * D_MODEL))


def _rope_block(x, cos, sin, half):
    lane = lax.broadcasted_iota(jnp.int32, x.shape, 1)
    first = (lane % (2 * half)) < half
    partner = jnp.where(first, pltpu.roll(x, LANES - half, 1), pltpu.roll(x, half, 1))
    return x * cos + partner * sin


def _rope(x, cos, sin, half):
    blocks = [_rope_block(x[:, j:j + LANES], cos, sin, half) for j in range(0, x.shape[1], LANES)]
    return blocks[0] if len(blocks) == 1 else jnp.concatenate(blocks, axis=1)


def _proj_kernel(latent, x_ref, mod_ref, g_ref, win_ref, gcq_ref, wuq_ref, gckv_ref, wukv_ref, *rest):
    if latent:
        cosa_ref, sina_ref, cosc_ref, sinc_ref = rest[:4]
        qa_ref, ka_ref, va_ref, fb_ref, qc_ref, kc_ref, vc_ref = rest[4:]
    else:
        qa_ref, ka_ref, va_ref, fb_ref, qc_ref, kc_ref, vc_ref, ska_ref, sva_ref, sckv_ref, skr_ref = rest
    x = x_ref[...]
    y = _rms(x, g_ref[...], D_MODEL)
    sh1 = mod_ref[:, 0:D_MODEL]
    sc1 = mod_ref[:, D_MODEL:2 * D_MODEL]
    h = (y * (1.0 + sc1) + sh1).astype(BF16)

    qa = _dot(h, win_ref[:, _QA0:_KA0])
    kva = _dot(h, win_ref[:, _KA0:_FB0])
    ka, va = kva[:, :A_KV_WIDTH], kva[:, A_KV_WIDTH:]
    fb = _dot(h, win_ref[:, _FB0:_CQ0])
    cq = _dot(h, win_ref[:, _CQ0:_CKV0])
    ckr = _dot(h, win_ref[:, _CKV0:_IN_PAD])
    ckv, kr = ckr[:, :C_KV_LORA], ckr[:, C_KV_LORA:]

    if not latent:
        ska_ref[...] = ka
        sva_ref[...] = va
        skr_ref[...] = kr[:, KR_LANE0:KR_LANE0 + C_ROPE]
    else:
        qa = _rope(qa, cosa_ref[...], sina_ref[...], HEAD_DIM // 4)
        ka = _rope(ka, cosa_ref[...], sina_ref[...], HEAD_DIM // 4)
        kr = _rope(kr, cosc_ref[...], sinc_ref[...], C_ROPE // 4)
    qa_ref[...] = (qa * (LOG2_E * HEAD_DIM ** -0.5)).astype(BF16)
    ka_ref[...] = ka.T.astype(BF16)
    va_ref[...] = va.astype(BF16)
    fb_ref[...] = fb.astype(fb_ref.dtype)

    cqn = _rms(cq, gcq_ref[...], C_Q_LORA).astype(BF16)
    qc = _dot(cqn, wuq_ref[...])
    if latent:
        qc = _rope(qc, cosc_ref[...], sinc_ref[...], C_ROPE // 4)
    qc_ref[...] = (qc * (LOG2_E * (C_NOPE + C_ROPE) ** -0.5)).astype(BF16)

    ckvn = _rms(ckv, gckv_ref[...], C_KV_LORA)
    if not latent:
        sckv_ref[...] = ckvn
    _store_kv(_dot(ckvn.astype(BF16), wukv_ref[...]), kr, kc_ref, vc_ref)


def _store_kv(kv, kr, kc_ref, vc_ref):
    kc_ref[...] = (kv[:, :CW] + jnp.concatenate([kr] * C_HEADS, axis=1)).astype(BF16)
    lane = lax.broadcasted_iota(jnp.int32, (1, CW), 1)
    upper = lane % C_HEAD_PAD >= C_V
    odd = (lane // C_HEAD_PAD) % 2 == 1
    vc_ref[...] = (kv[:, CW:] + jnp.where(upper != odd, 1.0, 0.0).astype(F32)).astype(BF16)


def _projection(latent, x, mod_l, g_mix, win_p, gcq_p, wuq_p, gckv, wukv_p, tables):
    n = x.shape[0]
    T = T_PROJ
    nt = n // T
    tpb = (DEC_SEQ // T) if latent else nt
    base = 1 if latent else 0
    tok = lambda w: pl.BlockSpec((T, w), lambda i: (i, 0))
    full = lambda a: pl.BlockSpec(a.shape, lambda i: (0,) * a.ndim)
    in_specs = [
        tok(D_MODEL),
        pl.BlockSpec((None, 1, 6 * D_MODEL), lambda i: (base + i // tpb, 0, 0)),
        full(g_mix), full(win_p), full(gcq_p), full(wuq_p), full(gckv), full(wukv_p),
    ]
    args = [x, mod_l, g_mix, win_p, gcq_p, wuq_p, gckv, wukv_p]
    if latent:
        in_specs += [pl.BlockSpec((T, LANES), lambda i: (i % tpb, 0))] * 4
        args += list(tables)
    widths = [A_WIDTH, A_KV_WIDTH, A_KV_WIDTH, B_WIDTH, CW, CW, CW]
    out_specs = [tok(w) for w in widths]
    out_shape = [jax.ShapeDtypeStruct((n, w), BF16) for w in widths]
    out_specs[1] = pl.BlockSpec((A_KV_WIDTH, T), lambda i: (0, i))
    out_shape[1] = jax.ShapeDtypeStruct((A_KV_WIDTH, n), BF16)
    if latent:
        out_shape[3] = jax.ShapeDtypeStruct((n, B_WIDTH), F32)
    if not latent:
        sw = [A_KV_WIDTH, A_KV_WIDTH, C_KV_LORA, C_ROPE]
        out_specs += [tok(w) for w in sw]
        out_shape += [jax.ShapeDtypeStruct((n, w), F32) for w in sw]
    return pl.pallas_call(
        functools.partial(_proj_kernel, latent),
        grid=(nt,),
        in_specs=in_specs,
        out_specs=out_specs,
        out_shape=out_shape,
        compiler_params=_cparams(("arbitrary",)),
        name="proj_lat" if latent else "proj_ctx",
    )(*args)


def _kvcache_kernel(ckv_ref, kr_ref, wukv_ref, kc_ref, vc_ref):
    _store_kv(_dot(ckv_ref[...].astype(BF16), wukv_ref[...]), kr_ref[...], kc_ref, vc_ref)


def _kvcache(ckv, kr_pad, wukv_p):
    B, P, _ = ckv.shape
    return pl.pallas_call(
        _kvcache_kernel,
        grid=(B,),
        in_specs=[
            pl.BlockSpec((None, P, C_KV_LORA), lambda b: (b, 0, 0)),
            pl.BlockSpec((None, P, LANES), lambda b: (b, 0, 0)),
            pl.BlockSpec(wukv_p.shape, lambda b: (0, 0)),
        ],
        out_specs=[pl.BlockSpec((None, P, CW), lambda b: (b, 0, 0))] * 2,
        out_shape=[jax.ShapeDtypeStruct((B, P, CW), BF16)] * 2,
        compiler_params=_cparams(("arbitrary",)),
        name="kvcache",
    )(ckv, kr_pad, wukv_p)


def _gqa_group(g, q_blocks, keys, masks, values, sink_ref):
    tq = q_blocks[0].shape[0]
    lane = lax.broadcasted_iota(jnp.int32, (tq, LANES), 1)
    mine = (lane >= HEAD_DIM) if g else (lane < HEAD_DIM)
    qs = jnp.concatenate([jnp.where(mine, qb, jnp.zeros_like(qb)) for qb in q_blocks], axis=0)
    parts = []
    for k, mask in zip(keys, masks):
        s = _dot(qs, k)
        parts.append(s if mask is None else jnp.where(mask, s, NEG))
    s = parts[0] if len(parts) == 1 else jnp.concatenate(parts, axis=1)
    rowblk = lax.broadcasted_iota(jnp.int32, (A_GROUP * tq, 1), 0) // tq
    sk = jnp.full((A_GROUP * tq, 1), sink_ref[0, g * A_GROUP] * LOG2_E, F32)
    for j in range(1, A_GROUP):
        sk = jnp.where(rowblk == j, sink_ref[0, g * A_GROUP + j] * LOG2_E, sk)
    m = jnp.maximum(jnp.max(s, axis=-1, keepdims=True), sk)
    p = jnp.exp2(s - m)
    l = jnp.sum(p, axis=-1, keepdims=True) + jnp.exp2(sk - m)
    v = values[0] if len(values) == 1 else jnp.concatenate(values, axis=0)
    return _dot(p.astype(BF16), v) * (1.0 / l)


def _gqa_store(o_ref, rows, tq, o0, o1):
    low = lax.broadcasted_iota(jnp.int32, (tq, LANES), 1) < HEAD_DIM
    for j in range(A_GROUP):
        blk = jnp.where(low, o0[j * tq:(j + 1) * tq], o1[j * tq:(j + 1) * tq])
        o_ref[rows, j * LANES:(j + 1) * LANES] = blk.astype(BF16)


def _attn_a_ctx_kernel(sink_ref, q_ref, k_ref, v_ref, o_ref):
    n = q_ref.shape[1]
    for e in range(CTX_GROUP):
        q_blocks = [q_ref[e, :, j * LANES:(j + 1) * LANES] for j in range(A_GROUP)]
        k = k_ref[:, e * n:(e + 1) * n]
        outs = [_gqa_group(g, q_blocks, [k], [None], [v_ref[e]], sink_ref) for g in range(A_KV_HEADS)]
        _gqa_store(o_ref.at[e], slice(None), n, *outs)


def _attn_a_ctx(sink, q, k, v):
    B, n, _ = q.shape
    G = CTX_GROUP
    blk = lambda w: pl.BlockSpec((G, n, w), lambda b: (b, 0, 0))
    return pl.pallas_call(
        _attn_a_ctx_kernel,
        grid=(B // G,),
        in_specs=[pl.BlockSpec(memory_space=pltpu.SMEM), blk(A_WIDTH),
                  pl.BlockSpec((A_KV_WIDTH, G * n), lambda b: (0, b)), blk(A_KV_WIDTH)],
        out_specs=blk(A_WIDTH),
        out_shape=jax.ShapeDtypeStruct((B, n, A_WIDTH), BF16),
        compiler_params=_cparams(("arbitrary",)),
        name="attn_a_ctx",
    )(sink, q, k, v)


def _attn_a_lat_kernel(sink_ref, q_ref, k_ref, v_ref, kc_ref, vc_ref, o_ref):
    n = k_ref.shape[1]
    band = 3 * BLOCK
    kcx = kc_ref[...]
    vcx = vc_ref[...]

    def body(sb, carry):
        blk = pl.program_id(1) * A_SUB + sb
        rows = pl.ds(pl.multiple_of(sb * BLOCK, BLOCK), BLOCK)
        start = pl.multiple_of(jnp.clip((blk - 1) * BLOCK, 0, n - band), BLOCK)
        kb = k_ref[:, pl.ds(start, band)]
        vb = v_ref[pl.ds(start, band), :]
        qpos = blk * BLOCK + lax.broadcasted_iota(jnp.int32, (BLOCK, band), 0)
        kpos = start + lax.broadcasted_iota(jnp.int32, (BLOCK, band), 1)
        mask = jnp.concatenate([jnp.abs(qpos - kpos) <= WINDOW] * A_GROUP, axis=0)
        q_blocks = [q_ref[rows, j * LANES:(j + 1) * LANES] for j in range(A_GROUP)]
        outs = [_gqa_group(g, q_blocks, [kb, kcx], [mask, None], [vb, vcx], sink_ref)
                for g in range(A_KV_HEADS)]
        _gqa_store(o_ref, rows, BLOCK, *outs)
        return carry

    lax.fori_loop(0, A_SUB, body, 0, unroll=True)


def _attn_a_lat(sink, q, k, v, k_ctx, v_ctx):
    B, n, _ = q.shape
    P = v_ctx.shape[1]
    tq = A_SUB * BLOCK
    seq = lambda w, m: pl.BlockSpec((None, m, w), lambda b, i: (b, 0, 0))
    seq_t = lambda w, m: pl.BlockSpec((w, m), lambda b, i: (0, b))
    return pl.pallas_call(
        _attn_a_lat_kernel,
        grid=(B, n // tq),
        in_specs=[pl.BlockSpec(memory_space=pltpu.SMEM),
                  pl.BlockSpec((None, tq, A_WIDTH), lambda b, i: (b, i, 0)),
                  seq_t(A_KV_WIDTH, n), seq(A_KV_WIDTH, n), seq_t(A_KV_WIDTH, P), seq(A_KV_WIDTH, P)],
        out_specs=pl.BlockSpec((None, tq, A_WIDTH), lambda b, i: (b, i, 0)),
        out_shape=jax.ShapeDtypeStruct((B, n, A_WIDTH), BF16),
        compiler_params=_cparams(("arbitrary", "arbitrary")),
        name="attn_a_lat",
    )(sink, q, k, v, k_ctx, v_ctx)


def _mla_step(q_ref, k, v, m_scr, acc_scr, first):
    reps = k.shape[0] // LANES
    for h in range(C_HEADS):
        hs = slice(h * C_HEAD_PAD, (h + 1) * C_HEAD_PAD)
        s = _dot_nt(q_ref[:, hs], k[:, hs])
        m_new = jnp.broadcast_to(jnp.max(s, axis=-1, keepdims=True), (s.shape[0], LANES))
        if not first:
            m_old = m_scr[h]
            m_new = jnp.maximum(m_old, m_new)
        p = jnp.exp2(s - jnp.concatenate([m_new] * reps, axis=1))
        m_scr[h] = m_new
        pv = _dot(p.astype(BF16), v[:, hs])
        acc_scr[h] = pv if first else acc_scr[h] * jnp.exp2(m_old - m_new) + pv


def _mla_tile(n_chunks, tk, q_ref, k_ref, v_ref, kc_ref, vc_ref, o_ref, m_scr, acc_scr):
    tq = q_ref.shape[0]

    def body(c, carry):
        ks = pl.multiple_of(c * tk, tk)
        _mla_step(q_ref, k_ref[pl.ds(ks, tk), :], v_ref[pl.ds(ks, tk), :], m_scr, acc_scr, False)
        return carry

    if kc_ref is not None:
        _mla_step(q_ref, kc_ref[...], vc_ref[...], m_scr, acc_scr, True)
        lax.fori_loop(0, n_chunks, body, 0)
    else:
        _mla_step(q_ref, k_ref[0:tk, :], v_ref[0:tk, :], m_scr, acc_scr, True)
        lax.fori_loop(1, n_chunks, body, 0)

    low = lax.broadcasted_iota(jnp.int32, (tq, LANES), 1) < C_V
    for j in range(C_HEADS // 2):
        even, odd = acc_scr[2 * j], acc_scr[2 * j + 1]
        sums = pltpu.roll(jnp.where(low, odd, even), C_V, 1)
        o_ref[:, j * LANES:(j + 1) * LANES] = (jnp.where(low, even, odd) / sums).astype(BF16)


def _mla_kernel(n_chunks, tk, has_ctx, group, q_ref, k_ref, v_ref, *rest):
    if has_ctx:
        kc_ref, vc_ref, o_ref, m_scr, acc_scr = rest
    else:
        (o_ref, m_scr, acc_scr), kc_ref, vc_ref = rest, None, None
    if group is None:
        _mla_tile(n_chunks, tk, q_ref, k_ref, v_ref, kc_ref, vc_ref, o_ref, m_scr, acc_scr)
    else:
        for e in range(group):
            _mla_tile(n_chunks, tk, q_ref.at[e], k_ref.at[e], v_ref.at[e], None, None, o_ref.at[e], m_scr, acc_scr)


def _mla(q, k, v, k_ctx, v_ctx, tq, tk, group=None):
    B, n, _ = q.shape
    nk = k.shape[1]
    has_ctx = k_ctx is not None
    lead = None if group is None else group
    nb = B if group is None else B // group
    seq = lambda m: pl.BlockSpec((lead, m, CW), lambda b, i: (b, 0, 0))
    in_specs = [pl.BlockSpec((lead, tq, CW), lambda b, i: (b, i, 0)), seq(nk), seq(nk)]
    args = [q, k, v]
    if has_ctx:
        in_specs += [seq(k_ctx.shape[1])] * 2
        args += [k_ctx, v_ctx]
    return pl.pallas_call(
        functools.partial(_mla_kernel, nk // tk, tk, has_ctx, group),
        grid=(nb, n // tq),
        in_specs=in_specs,
        out_specs=pl.BlockSpec((lead, tq, C_WIDTH), lambda b, i: (b, i, 0)),
        out_shape=jax.ShapeDtypeStruct((B, n, C_WIDTH), BF16),
        scratch_shapes=[pltpu.VMEM((C_HEADS, tq, LANES), F32), pltpu.VMEM((C_HEADS, tq, LANES), F32)],
        compiler_params=_cparams(("arbitrary", "arbitrary")),
        name="mla_lat" if has_ctx else "mla_ctx",
    )(*args)


def _dft_tables(n):
    j = np.arange(n, dtype=np.int64)
    ang = ((j[:, None] * j[None, :]) % n) * (2.0 * np.pi / n)
    return np.cos(ang), np.sin(ang)


def _channel_dft():
    c, s = _dft_tables(B_GROUP_DIM)
    eye = np.eye(B_GROUPS)
    return _table_bf16(np.concatenate([np.kron(eye, c), -np.kron(eye, s)], axis=1))


def _table_bf16(t):
    return jnp.asarray(t, F32).astype(BF16)


def _fourier_ctx_kernel(scale, x_ref, fc_ref, fn_ref, o_ref):
    for e in range(CTX_GROUP):
        u = _dot(x_ref[e], fc_ref[...])
        ucat = jnp.concatenate([u[:, :B_WIDTH], u[:, B_WIDTH:]], axis=0).astype(BF16)
        o_ref[e] = (_dot(fn_ref[...], ucat) * scale).astype(BF16)


def _fourier_ctx(fb, fc):
    B, n, _ = fb.shape
    c, s = _dft_tables(n)
    fn = _table_bf16(np.concatenate([c, s], axis=1))
    scale = float((n * B_GROUP_DIM) ** -0.5)
    return pl.pallas_call(
        functools.partial(_fourier_ctx_kernel, scale),
        grid=(B // CTX_GROUP,),
        in_specs=[pl.BlockSpec((CTX_GROUP, n, B_WIDTH), lambda b: (b, 0, 0)),
                  pl.BlockSpec(fc.shape, lambda b: (0, 0)),
                  pl.BlockSpec(fn.shape, lambda b: (0, 0))],
        out_specs=pl.BlockSpec((CTX_GROUP, n, B_WIDTH), lambda b: (b, 0, 0)),
        out_shape=jax.ShapeDtypeStruct((B, n, B_WIDTH), BF16),
        compiler_params=_cparams(("arbitrary",)),
        name="fourier_ctx",
    )(fb, fc, fn)


def _fourier_lat_kernel(scale, x_ref, fc_ref, g_ref, f_ref, o_ref, zr_scr, zi_scr):
    j = pl.program_id(1)
    nc = pl.num_programs(1) // 2

    @pl.when(j < nc)
    def _():
        for cc in range(FOURIER_COLS):
            xc = x_ref[:, cc, :].astype(BF16)
            u = _dot(xc, fc_ref[...]).astype(BF16)
            p = _dot(g_ref[cc], u)
            R = p.shape[0] // 2
            zshape = (R // FOURIER_COLS, FOURIER_COLS, B_WIDTH)
            zr_scr[j * FOURIER_COLS + cc] = (p[:R, :B_WIDTH] - p[R:, B_WIDTH:]).reshape(zshape)
            zi_scr[j * FOURIER_COLS + cc] = (p[:R, B_WIDTH:] + p[R:, :B_WIDTH]).reshape(zshape)

    @pl.when(j >= nc)
    def _():
        for kk in range(FOURIER_COLS):
            z = jnp.concatenate([zr_scr[:, j - nc, kk, :], zi_scr[:, j - nc, kk, :]], axis=0).astype(BF16)
            o_ref[:, kk, :] = _dot(f_ref[...], z) * scale


def _fourier_lat(fb, fc):
    B, n, _ = fb.shape
    R = n // GRID_W
    W = GRID_W * B_WIDTH
    kr = np.arange(R, dtype=np.int64)
    pos = GRID_W * np.arange(R, dtype=np.int64)[None, None, :] + np.arange(GRID_W, dtype=np.int64)[:, None, None]
    ang = ((kr[None, :, None] * pos) % n) * (2.0 * np.pi / n)
    g = _table_bf16(np.concatenate([np.cos(ang), -np.sin(ang)], axis=1))
    c64, s64 = _dft_tables(GRID_W)
    f2 = _table_bf16(np.concatenate([c64, s64], axis=1))
    assert R == GRID_W
    nc = GRID_W // FOURIER_COLS
    scale = float((n * B_GROUP_DIM) ** -0.5)
    blk = (None, GRID_W, FOURIER_COLS, B_WIDTH)
    zscr = pltpu.VMEM((GRID_W, R // FOURIER_COLS, FOURIER_COLS, B_WIDTH), F32)
    out = pl.pallas_call(
        functools.partial(_fourier_lat_kernel, scale),
        grid=(B, 2 * nc),
        in_specs=[pl.BlockSpec(blk, lambda b, j: (b, 0, jnp.minimum(j, nc - 1), 0)),
                  pl.BlockSpec(fc.shape, lambda b, j: (0, 0)),
                  pl.BlockSpec((FOURIER_COLS, 2 * R, R), lambda b, j: (jnp.minimum(j, nc - 1), 0, 0)),
                  pl.BlockSpec(f2.shape, lambda b, j: (0, 0))],
        out_specs=pl.BlockSpec(blk, lambda b, j: (b, 0, jnp.maximum(j - nc, 0), 0)),
        out_shape=jax.ShapeDtypeStruct((B, GRID_W, R, B_WIDTH), F32),
        scratch_shapes=[zscr, zscr],
        compiler_params=_cparams(("arbitrary", "arbitrary")),
        name="fourier_lat",
    )(fb.reshape(B, R, GRID_W, B_WIDTH), fc, g, f2)
    return out.reshape(B, n, B_WIDTH)


def _ffn_kernel(T, S, final, x_ref, xp_ref, xn_ref, a_ref, ap_ref, an_ref, b_ref, bp_ref, bn_ref, c_ref, cp_ref,
                cn_ref, mod_ref, g_ref, wout_ref, wug_ref, cw_ref, cb_ref, wd_ref, gfin_ref,
                o_ref, mix_scr, hs_scr, u0_scr, u1_scr, act_scr):
    i = pl.program_id(0)
    H = BF16_ROWS
    for r0, refs in ((0, (ap_ref, bp_ref, cp_ref)), (H, (a_ref, b_ref, c_ref)), (T + H, (an_ref, bn_ref, cn_ref))):
        rows = slice(r0, r0 + refs[0].shape[0])
        mix_scr[rows, 0:A_WIDTH] = refs[0][...]
        mix_scr[rows, A_WIDTH:A_WIDTH + B_WIDTH] = refs[1][...].astype(BF16)
        mix_scr[rows, A_WIDTH + B_WIDTH:] = refs[2][...]
    mix = _dot(mix_scr[...], wout_ref[...])
    gt1 = mod_ref[:, 2 * D_MODEL:3 * D_MODEL]
    sh2 = mod_ref[:, 3 * D_MODEL:4 * D_MODEL]
    sc2 = mod_ref[:, 4 * D_MODEL:5 * D_MODEL]

    def ffn_input(x, m):
        x1 = x + gt1 * m
        return x1, (_rms(x1, g_ref[...], D_MODEL) * (1.0 + sc2) + sh2).astype(BF16)

    at_start = (i * T) % S == 0
    at_end = ((i + 1) * T) % S == 0
    hp = ffn_input(xp_ref[...], mix[0:H])[1]
    hn = ffn_input(xn_ref[...], mix[T + H:T + 2 * H])[1]
    x1, hm = ffn_input(x_ref[...], mix[H:T + H])
    o_ref[...] = x1
    hs_scr[0:H] = jnp.where(at_start, jnp.zeros_like(hp), hp)
    hs_scr[H:T + H] = hm
    hs_scr[T + H:T + 2 * H] = jnp.where(at_end, jnp.zeros_like(hn), hn)
    if S < T:
        pos = lax.broadcasted_iota(jnp.int32, (T, 1), 0) % S
        has_prev = pos != 0
        has_next = pos != S - 1

    n_chunks = D_FF // F_CHUNK
    cols = lambda j: (pl.multiple_of(j * F_CHUNK, F_CHUNK), pl.multiple_of(j * F_CHUNK + D_FF, LANES))

    def up_proj(j, u_ref):
        for b, col in enumerate(cols(j)):
            u_ref[b] = _dot(hs_scr[...], wug_ref[:, pl.ds(col, F_CHUNK)])

    def conv(u_ref, b, col):
        u = u_ref[b]
        rows = u.shape[0]
        up = pltpu.roll(u, 1, 0)[H:T + H]
        un = pltpu.roll(u, rows - 1, 0)[H:T + H]
        if S < T:
            up = jnp.where(has_prev, up, 0.0)
            un = jnp.where(has_next, un, 0.0)
        cw = cw_ref[:, pl.ds(col, F_CHUNK)]
        return (up * cw[0:1, :] + u[H:T + H] * cw[1:2, :] + un * cw[2:3, :]
                + cb_ref[:, pl.ds(col, F_CHUNK)])

    def gate(j, u_ref):
        a, g = [conv(u_ref, b, col) for b, col in enumerate(cols(j))]
        act_scr[:, pl.ds(cols(j)[0], F_CHUNK)] = (g * jax.nn.sigmoid(g) * a).astype(BF16)

    def pair(k, carry):
        up_proj(2 * k + 1, u1_scr)
        gate(2 * k, u0_scr)
        up_proj(2 * k + 2, u0_scr)
        gate(2 * k + 1, u1_scr)
        return carry

    assert n_chunks % 2 == 1
    up_proj(0, u0_scr)
    lax.fori_loop(0, n_chunks // 2, pair, 0, unroll=True)
    gate(n_chunks - 1, u0_scr)
    gt2 = mod_ref[:, 5 * D_MODEL:6 * D_MODEL]
    x2 = o_ref[...] + gt2 * _dot(act_scr[...], wd_ref[...])
    if final:
        x2 = _rms(x2, gfin_ref[...], D_MODEL)
    o_ref[...] = x2


def _ffn(latent, final, layer, x, oa, ob, oc, mod_l, g_ffn, wout, wug, conv_w, conv_b, wdown, g_final):
    n = x.shape[0]
    T = T_FFN
    S = DEC_SEQ if latent else SEQ
    nt = n // T
    tpb = (DEC_SEQ // T) if latent else nt
    base = 1 if latent else 0
    hb = T // BF16_ROWS
    nhb = n // BF16_ROWS
    once = dict(pipeline_mode=pl.Buffered(1))
    whole = lambda a: pl.BlockSpec(a.shape, lambda i: (0,) * a.ndim, **once)
    of_layer = lambda a: pl.BlockSpec((None,) + a.shape[1:], lambda i: (layer,) + (0,) * (a.ndim - 1), **once)
    with_halos = lambda w: [
        pl.BlockSpec((T, w), lambda i: (i, 0)),
        pl.BlockSpec((BF16_ROWS, w), lambda i: (jnp.maximum(i * hb - 1, 0), 0)),
        pl.BlockSpec((BF16_ROWS, w), lambda i: (jnp.minimum((i + 1) * hb, nhb - 1), 0))]
    return pl.pallas_call(
        functools.partial(_ffn_kernel, T, S, final),
        grid=(nt,),
        in_specs=[spec for a in (x, oa, ob, oc) for spec in with_halos(a.shape[1])] + [
            pl.BlockSpec((None, 1, 6 * D_MODEL), lambda i: (base + i // tpb, 0, 0)),
            whole(g_ffn), whole(wout),
            of_layer(wug), of_layer(conv_w), of_layer(conv_b), of_layer(wdown), whole(g_final),
        ],
        out_specs=pl.BlockSpec((T, D_MODEL), lambda i: (i, 0)),
        out_shape=jax.ShapeDtypeStruct((n, D_MODEL), F32),
        scratch_shapes=[pltpu.VMEM((T + 2 * BF16_ROWS, D_MODEL), BF16),
                        pltpu.VMEM((T + 2 * BF16_ROWS, D_MODEL), BF16),
                        pltpu.VMEM((2, T + 2 * BF16_ROWS, F_CHUNK), F32),
                        pltpu.VMEM((2, T + 2 * BF16_ROWS, F_CHUNK), F32),
                        pltpu.VMEM((T, D_FF), BF16)],
        compiler_params=_cparams(("arbitrary",)),
        name="ffn_lat" if latent else "ffn_ctx",
    )(x, x, x, oa, oa, oa, ob, ob, ob, oc, oc, oc, mod_l, g_ffn, wout, wug, conv_w, conv_b, wdown, g_final)


def _rope_tables(n_tok, dim, lane0, width):
    rows = n_tok // GRID_W
    r = np.repeat(np.arange(rows), GRID_W).astype(np.float64)
    col = np.tile(np.arange(GRID_W), rows).astype(np.float64)
    quarter = dim // 4
    inv = ROPE_BASE ** (-np.arange(quarter, dtype=np.float64) / quarter)
    ang_r = r[:, None] * inv
    ang_c = col[:, None] * inv
    cos = np.concatenate([np.cos(ang_r)] * 2 + [np.cos(ang_c)] * 2, axis=1)
    sin = np.concatenate([-np.sin(ang_r), np.sin(ang_r), -np.sin(ang_c), np.sin(ang_c)], axis=1)
    reps = width // dim
    cos = np.concatenate([cos] * reps, axis=1)
    sin = np.concatenate([sin] * reps, axis=1)
    pad = ((0, 0), (lane0, LANES - lane0 - width))
    return jnp.asarray(np.pad(cos, pad, constant_values=1.0), F32), jnp.asarray(np.pad(sin, pad), F32)


def _layer_weights(w_in, g_cq, w_uq, w_ukv):
    z = lambda r, c: jnp.zeros((r, c), F32)
    o = np.cumsum([0, A_WIDTH, A_KV_WIDTH, A_KV_WIDTH, B_WIDTH, C_Q_LORA, C_KV_LORA, C_ROPE])
    win_p = jnp.concatenate([
        w_in[:, o[0]:o[1]].reshape(D_MODEL, A_HEADS, HEAD_DIM)[:, A_HEAD_ORDER, :].reshape(D_MODEL, A_WIDTH),
        w_in[:, o[1]:o[4]],
        w_in[:, o[4]:o[5]], z(D_MODEL, CQ_PAD - C_Q_LORA),
        w_in[:, o[5]:o[6]],
        z(D_MODEL, KR_LANE0), w_in[:, o[6]:o[7]], z(D_MODEL, LANES - KR_LANE0 - C_ROPE),
    ], axis=1).astype(BF16)
    gcq_p = jnp.pad(g_cq, (0, CQ_PAD - C_Q_LORA)).reshape(1, CQ_PAD)
    hq = C_NOPE + C_ROPE
    wuq_h = w_uq.reshape(C_Q_LORA, C_HEADS, hq)
    wuq_p = jnp.pad(wuq_h, ((0, CQ_PAD - C_Q_LORA), (0, 0), (0, C_HEAD_PAD - hq)))
    wuq_p = wuq_p.reshape(CQ_PAD, C_HEADS * C_HEAD_PAD).astype(BF16)
    wukv_h = w_ukv.reshape(C_KV_LORA, C_HEADS, C_NOPE + C_V)
    wk = jnp.pad(wukv_h[:, :, :C_NOPE], ((0, 0), (0, 0), (0, C_HEAD_PAD - C_NOPE)))
    zv = jnp.zeros((C_KV_LORA, C_HEAD_PAD - C_V), F32)
    wv = [wukv_h[:, h, C_NOPE:] for h in range(C_HEADS)]
    wv = jnp.concatenate([jnp.concatenate([zv, w] if h % 2 else [w, zv], axis=1) for h, w in enumerate(wv)], axis=1)
    wukv_p = jnp.concatenate([wk.reshape(C_KV_LORA, CW), wv], axis=1).astype(BF16)
    return win_p, gcq_p, wuq_p, wukv_p


def kernel(x_prompt, x_sample, cache_win_k, cache_win_v, cache_mla_ckv, cache_mla_krope, c, c_ctx,
           w_ada, b_ada, g_mix, w_in, sink, g_cq, w_uq, g_ckv, w_ukv, w_out, g_ffn, w_ug, conv_w,
           conv_b, w_down, g_final):
    n_ctx = BATCH * SEQ
    n_lat = DEC_BATCH * DEC_SEQ
    xp = x_prompt.reshape(n_ctx, D_MODEL)
    xs = x_sample.reshape(n_lat, D_MODEL)

    cvecs = jnp.concatenate([c_ctx[None, :], c, jnp.zeros((8 - 1 - DEC_BATCH, D_MODEL), F32)], axis=0)
    mod = _modulation(cvecs, w_ada, b_ada)

    cos_a, sin_a = _rope_tables(DEC_SEQ, HEAD_DIM, 0, LANES)
    cos_c, sin_c = _rope_tables(DEC_SEQ, C_ROPE, KR_LANE0, C_ROPE)
    tables = (cos_a, sin_a, cos_c, sin_c)
    fc = _channel_dft()
    g_final2 = g_final.reshape(1, D_MODEL)
    wug_all = w_ug.astype(BF16)
    wdown_all = w_down.astype(BF16)
    conv_b_all = conv_b.reshape(DEPTH, 1, 2 * D_FF)

    new_k, new_v, new_ckv, new_kr = [], [], [], []
    for l in range(DEPTH):
        win_p, gcq_p, wuq_p, wukv_p = _layer_weights(w_in[l], g_cq[l], w_uq[l], w_ukv[l])
        mod_l = mod[l].reshape(8, 1, 6 * D_MODEL)
        g_mix_l = g_mix[l].reshape(1, D_MODEL)
        g_ckv_l = g_ckv[l].reshape(1, C_KV_LORA)
        g_ffn_l = g_ffn[l].reshape(1, D_MODEL)
        sink_l = sink[l].reshape(1, A_HEADS)
        wout_a = w_out[l, :A_WIDTH].reshape(A_HEADS, HEAD_DIM, D_MODEL)[jnp.array(A_HEAD_ORDER)]
        wout_l = jnp.concatenate([wout_a.reshape(A_WIDTH, D_MODEL), w_out[l, A_WIDTH:]], axis=0).astype(BF16)
        final = l == DEPTH - 1

        (qa, ka, va, fb, qc, kc, vc, ska, sva, sckv, skr) = _projection(
            False, xp, mod_l, g_mix_l, win_p, gcq_p, wuq_p, g_ckv_l, wukv_p, None)
        r3 = lambda a: a.reshape(BATCH, SEQ, a.shape[-1])
        oa = _attn_a_ctx(sink_l, r3(qa), ka, r3(va))
        ob = _fourier_ctx(r3(fb), fc)
        oc = _mla(r3(qc), r3(kc), r3(vc), None, None, SEQ, SEQ, group=CTX_GROUP)
        f2 = lambda a: a.reshape(n_ctx, a.shape[-1])
        xp = _ffn(False, final, l, xp, f2(oa), f2(ob), f2(oc), mod_l, g_ffn_l, wout_l, wug_all, conv_w, conv_b_all,
                  wdown_all, g_final2)
        new_k.append(ska.reshape(BATCH, SEQ, A_KV_HEADS, HEAD_DIM))
        new_v.append(sva.reshape(BATCH, SEQ, A_KV_HEADS, HEAD_DIM))
        new_ckv.append(sckv.reshape(BATCH, SEQ, C_KV_LORA))
        new_kr.append(skr.reshape(BATCH, SEQ, C_ROPE))

        (qa, ka, va, fb, qc, kc, vc) = _projection(
            True, xs, mod_l, g_mix_l, win_p, gcq_p, wuq_p, g_ckv_l, wukv_p, tables)
        r3 = lambda a: a.reshape(DEC_BATCH, DEC_SEQ, a.shape[-1])
        kwin = cache_win_k[:, l].reshape(DEC_BATCH * PAST_LEN, A_KV_WIDTH).T.astype(BF16)
        vwin = cache_win_v[:, l].reshape(DEC_BATCH, PAST_LEN, A_KV_WIDTH).astype(BF16)
        oa = _attn_a_lat(sink_l, r3(qa), ka, r3(va), kwin, vwin)
        ob = _fourier_lat(r3(fb), fc)
        kr_pad = jnp.pad(cache_mla_krope[:, l], ((0, 0), (0, 0), (KR_LANE0, LANES - KR_LANE0 - C_ROPE)))
        kc_ctx, vc_ctx = _kvcache(cache_mla_ckv[:, l], kr_pad, wukv_p)
        oc = _mla(r3(qc), r3(kc), r3(vc), kc_ctx, vc_ctx, TQ_MLA, TK_MLA)
        f2 = lambda a: a.reshape(n_lat, a.shape[-1])
        xs = _ffn(True, final, l, xs, f2(oa), f2(ob), f2(oc), mod_l, g_ffn_l, wout_l, wug_all, conv_w, conv_b_all,
                  wdown_all, g_final2)

    y_prompt = xp.reshape(BATCH, SEQ, D_MODEL)
    y_sample = xs.reshape(DEC_BATCH, DEC_SEQ, D_MODEL)
    return (y_prompt, y_sample, jnp.stack(new_k, axis=1), jnp.stack(new_v, axis=1),
            jnp.stack(new_ckv, axis=1), jnp.stack(new_kr, axis=1))
```

```python
import functools

import numpy as np
import jax
import jax.numpy as jnp
from jax import lax
from jax.experimental import pallas as pl
from jax.experimental.pallas import tpu as pltpu

F32 = jnp.float32
BF16 = jnp.bfloat16

D_MODEL = 1024
BATCH = 16
SEQ = 256
DEPTH = 2
DEC_BATCH = 4
DEC_SEQ = 4096
PAST_LEN = 256
GRID_W = 64
HEAD_DIM = 64
A_HEADS = 8
A_KV_HEADS = 2
A_GROUP = 4
A_WIDTH = 512
A_KV_WIDTH = 128
WINDOW = 128
BLOCK = 128
B_WIDTH = 256
B_GROUP_DIM = 64
B_GROUPS = 4
C_HEADS = 4
C_NOPE = 64
C_ROPE = 32
C_V = 64
C_Q_LORA = 192
C_KV_LORA = 128
C_WIDTH = 256
D_FF = 2816
ROPE_BASE = 10000.0
EPS = 1e-6
NEG = -1e30

LANES = 128
BF16_ROWS = 16
C_HEAD_PAD = 128
CW = C_HEADS * C_HEAD_PAD
CQ_PAD = 256
VMEM_LIMIT = 56 * 1024 * 1024

_QA0, _KA0, _VA0, _FB0, _CQ0, _CKV0, _KR0, _IN_PAD = 0, 512, 640, 768, 1024, 1280, 1408, 1536
_q8 = C_ROPE // 4
C_NOPE_LANES = list(range(0, 48)) + list(range(64, 80))
C_ROPE_LANES = (list(range(48, 48 + _q8)) + list(range(112, 112 + _q8))
                + list(range(48 + _q8, 48 + 2 * _q8)) + list(range(112 + _q8, 112 + 2 * _q8)))
C_STATE_LANES = list(range(80, 80 + C_ROPE))

T_PROJ = 1024
T_FFN = 1024
F_CHUNK = 256
TQ_MLA = 1024
TK_MLA = 2048
A_SUB = 8
CTX_GROUP = 4
A_HEAD_ORDER = [h for j in range(A_GROUP) for h in (j, A_GROUP + j)]
LOG2_E = float(np.log2(np.e))
FOURIER_COLS = 32


def _cparams(sem):
    return pltpu.CompilerParams(dimension_semantics=sem, vmem_limit_bytes=VMEM_LIMIT)


def _dot(a, b):
    return jnp.dot(a, b, preferred_element_type=F32)


def _dot_nt(a, b):
    return lax.dot_general(a, b, (((1,), (1,)), ((), ())), preferred_element_type=F32)


def _rms(x, g, n):
    ms = jnp.sum(x * x, axis=-1, keepdims=True) * (1.0 / n)
    return x * lax.rsqrt(ms + EPS) * g


def _split_bf16(a):
    hi = a.astype(BF16)
    return hi, (a - hi.astype(F32)).astype(BF16)


def _mod_kernel(c_ref, w_ref, b_ref, o_ref):
    cv = c_ref[...]
    s = cv * jax.nn.sigmoid(cv)
    s_hi, s_lo = _split_bf16(s)
    w_hi, w_lo = _split_bf16(w_ref[...])
    o_ref[...] = _dot(s_hi, w_hi) + (_dot(s_hi, w_lo) + _dot(s_lo, w_hi)) + b_ref[...]


def _modulation(cvecs, w_ada, b_ada):
    nj = 6
    return pl.pallas_call(
        _mod_kernel,
        grid=(DEPTH, nj),
        in_specs=[
            pl.BlockSpec((8, D_MODEL), lambda l, j: (0, 0)),
            pl.BlockSpec((None, D_MODEL, D_MODEL), lambda l, j: (l, 0, j)),
            pl.BlockSpec((None, 1, D_MODEL), lambda l, j: (l, 0, j)),
        ],
        out_specs=pl.BlockSpec((None, 8, D_MODEL), lambda l, j: (l, 0, j)),
        out_shape=jax.ShapeDtypeStruct((DEPTH, 8, 6 * D_MODEL), F32),
        compiler_params=_cparams(("arbitrary", "arbitrary")),
        name="modulation",
    )(cvecs, w_ada, b_ada.reshape(DEPTH, 1, 6 * D_MODEL))


def _rope_block(x, cos, sin, half):
    if 2 * half == LANES:
        partner = pltpu.roll(x, half, 1)
    else:
        lane = lax.broadcasted_iota(jnp.int32, x.shape, 1)
        first = (lane % (2 * half)) < half
        partner = jnp.where(first, pltpu.roll(x, LANES - half, 1), pltpu.roll(x, half, 1))
    return x * cos + partner * sin


def _rope(x, cos, sin, half):
    blocks = [_rope_block(x[:, j:j + LANES], cos, sin, half) for j in range(0, x.shape[1], LANES)]
    return blocks[0] if len(blocks) == 1 else jnp.concatenate(blocks, axis=1)


def _proj_kernel(latent, x_ref, mod_ref, g_ref, win_ref, gcq_ref, wuq_ref, gckv_ref, wukv_ref, *rest):
    if latent:
        cosa_ref, sina_ref, cosc_ref, sinc_ref = rest[:4]
        qa_ref, ka_ref, va_ref, fb_ref, qc_ref, kc_ref, vc_ref = rest[4:]
    else:
        qa_ref, ka_ref, va_ref, fb_ref, qc_ref, kc_ref, vc_ref, ska_ref, sva_ref, sckv_ref, skr_ref = rest
    x = x_ref[...]
    y = _rms(x, g_ref[...], D_MODEL)
    sh1 = mod_ref[:, 0:D_MODEL]
    sc1 = mod_ref[:, D_MODEL:2 * D_MODEL]
    h = (y * (1.0 + sc1) + sh1).astype(BF16)

    qa = _dot(h, win_ref[:, _QA0:_KA0])
    kva = _dot(h, win_ref[:, _KA0:_FB0])
    ka, va = kva[:, :A_KV_WIDTH], kva[:, A_KV_WIDTH:]
    fb = _dot(h, win_ref[:, _FB0:_CQ0])
    cq = _dot(h, win_ref[:, _CQ0:_CKV0])
    ckr = _dot(h, win_ref[:, _CKV0:_IN_PAD])
    ckv, kr = ckr[:, :C_KV_LORA], ckr[:, C_KV_LORA:]

    if not latent:
        ska_ref[...] = ka
        sva_ref[...] = va
        skr_ref[...] = kr[:, C_STATE_LANES[0]:C_STATE_LANES[-1] + 1]
    else:
        qa = _rope(qa, cosa_ref[...], sina_ref[...], HEAD_DIM // 4)
        ka = _rope(ka, cosa_ref[...], sina_ref[...], HEAD_DIM // 4)
        kr = _rope(kr, cosc_ref[...], sinc_ref[...], LANES // 2)
    qa_ref[...] = (qa * (LOG2_E * HEAD_DIM ** -0.5)).astype(BF16)
    ka_ref[...] = ka.T.astype(BF16)
    va_ref[...] = va.astype(BF16)
    fb_ref[...] = fb.astype(fb_ref.dtype)

    cqn = _rms(cq, gcq_ref[...], C_Q_LORA).astype(BF16)
    qc = _dot(cqn, wuq_ref[...])
    if latent:
        qc = _rope(qc, cosc_ref[...], sinc_ref[...], LANES // 2)
    qc_ref[...] = (qc * (LOG2_E * (C_NOPE + C_ROPE) ** -0.5)).astype(BF16)

    ckvn = _rms(ckv, gckv_ref[...], C_KV_LORA)
    if not latent:
        sckv_ref[...] = ckvn
    _store_kv(_dot(ckvn.astype(BF16), wukv_ref[...]), kr, kc_ref, vc_ref)


def _store_kv(kv, kr, kc_ref, vc_ref):
    kc_ref[...] = (kv[:, :CW] + jnp.concatenate([kr] * C_HEADS, axis=1)).astype(BF16)
    lane = lax.broadcasted_iota(jnp.int32, (1, CW), 1)
    upper = lane % C_HEAD_PAD >= C_V
    odd = (lane // C_HEAD_PAD) % 2 == 1
    vc_ref[...] = (kv[:, CW:] + jnp.where(upper != odd, 1.0, 0.0).astype(F32)).astype(BF16)


def _projection(latent, x, mod_l, g_mix, win_p, gcq_p, wuq_p, gckv, wukv_p, tables):
    n = x.shape[0]
    T = T_PROJ
    nt = n // T
    tpb = (DEC_SEQ // T) if latent else nt
    base = 1 if latent else 0
    tok = lambda w: pl.BlockSpec((T, w), lambda i: (i, 0))
    full = lambda a: pl.BlockSpec(a.shape, lambda i: (0,) * a.ndim)
    in_specs = [
        tok(D_MODEL),
        pl.BlockSpec((None, 1, 6 * D_MODEL), lambda i: (base + i // tpb, 0, 0)),
        full(g_mix), full(win_p), full(gcq_p), full(wuq_p), full(gckv), full(wukv_p),
    ]
    args = [x, mod_l, g_mix, win_p, gcq_p, wuq_p, gckv, wukv_p]
    if latent:
        in_specs += [pl.BlockSpec((T, LANES), lambda i: (i % tpb, 0))] * 4
        args += list(tables)
    widths = [A_WIDTH, A_KV_WIDTH, A_KV_WIDTH, B_WIDTH, CW, CW, CW]
    out_specs = [tok(w) for w in widths]
    out_shape = [jax.ShapeDtypeStruct((n, w), BF16) for w in widths]
    out_specs[1] = pl.BlockSpec((A_KV_WIDTH, T), lambda i: (0, i))
    out_shape[1] = jax.ShapeDtypeStruct((A_KV_WIDTH, n), BF16)
    if latent:
        out_shape[3] = jax.ShapeDtypeStruct((n, B_WIDTH), F32)
    if not latent:
        sw = [A_KV_WIDTH, A_KV_WIDTH, C_KV_LORA, C_ROPE]
        out_specs += [tok(w) for w in sw]
        out_shape += [jax.ShapeDtypeStruct((n, w), F32) for w in sw]
    return pl.pallas_call(
        functools.partial(_proj_kernel, latent),
        grid=(nt,),
        in_specs=in_specs,
        out_specs=out_specs,
        out_shape=out_shape,
        compiler_params=_cparams(("arbitrary",)),
        name="proj_lat" if latent else "proj_ctx",
    )(*args)


def _kvcache_kernel(ckv_ref, kr_ref, wukv_ref, kc_ref, vc_ref):
    _store_kv(_dot(ckv_ref[...].astype(BF16), wukv_ref[...]), kr_ref[...], kc_ref, vc_ref)


def _kvcache(ckv, kr_pad, wukv_p):
    B, P, _ = ckv.shape
    return pl.pallas_call(
        _kvcache_kernel,
        grid=(B,),
        in_specs=[
            pl.BlockSpec((None, P, C_KV_LORA), lambda b: (b, 0, 0)),
            pl.BlockSpec((None, P, LANES), lambda b: (b, 0, 0)),
            pl.BlockSpec(wukv_p.shape, lambda b: (0, 0)),
        ],
        out_specs=[pl.BlockSpec((None, P, CW), lambda b: (b, 0, 0))] * 2,
        out_shape=[jax.ShapeDtypeStruct((B, P, CW), BF16)] * 2,
        compiler_params=_cparams(("arbitrary",)),
        name="kvcache",
    )(ckv, kr_pad, wukv_p)


def _gqa_group(g, q_blocks, keys, masks, values, sink_ref):
    tq = q_blocks[0].shape[0]
    lane = lax.broadcasted_iota(jnp.int32, (tq, LANES), 1)
    mine = (lane >= HEAD_DIM) if g else (lane < HEAD_DIM)
    qs = jnp.concatenate([jnp.where(mine, qb, jnp.zeros_like(qb)) for qb in q_blocks], axis=0)
    parts = []
    for k, mask in zip(keys, masks):
        s = _dot(qs, k)
        parts.append(s if mask is None else jnp.where(mask, s, NEG))
    s = parts[0] if len(parts) == 1 else jnp.concatenate(parts, axis=1)
    rowblk = lax.broadcasted_iota(jnp.int32, (A_GROUP * tq, 1), 0) // tq
    sk = jnp.full((A_GROUP * tq, 1), sink_ref[0, g * A_GROUP] * LOG2_E, F32)
    for j in range(1, A_GROUP):
        sk = jnp.where(rowblk == j, sink_ref[0, g * A_GROUP + j] * LOG2_E, sk)
    m = jnp.maximum(jnp.max(s, axis=-1, keepdims=True), sk)
    p = jnp.exp2(s - m)
    l = jnp.sum(p, axis=-1, keepdims=True) + jnp.exp2(sk - m)
    v = values[0] if len(values) == 1 else jnp.concatenate(values, axis=0)
    return _dot(p.astype(BF16), v) * (1.0 / l)


def _gqa_store(o_ref, rows, tq, o0, o1):
    low = lax.broadcasted_iota(jnp.int32, (tq, LANES), 1) < HEAD_DIM
    for j in range(A_GROUP):
        blk = jnp.where(low, o0[j * tq:(j + 1) * tq], o1[j * tq:(j + 1) * tq])
        o_ref[rows, j * LANES:(j + 1) * LANES] = blk.astype(BF16)


def _attn_a_ctx_kernel(sink_ref, q_ref, k_ref, v_ref, o_ref):
    n = q_ref.shape[1]
    for e in range(CTX_GROUP):
        q_blocks = [q_ref[e, :, j * LANES:(j + 1) * LANES] for j in range(A_GROUP)]
        k = k_ref[:, e * n:(e + 1) * n]
        outs = [_gqa_group(g, q_blocks, [k], [None], [v_ref[e]], sink_ref) for g in range(A_KV_HEADS)]
        _gqa_store(o_ref.at[e], slice(None), n, *outs)


def _attn_a_ctx(sink, q, k, v):
    B, n, _ = q.shape
    G = CTX_GROUP
    blk = lambda w: pl.BlockSpec((G, n, w), lambda b: (b, 0, 0))
    return pl.pallas_call(
        _attn_a_ctx_kernel,
        grid=(B // G,),
        in_specs=[pl.BlockSpec(memory_space=pltpu.SMEM), blk(A_WIDTH),
                  pl.BlockSpec((A_KV_WIDTH, G * n), lambda b: (0, b)), blk(A_KV_WIDTH)],
        out_specs=blk(A_WIDTH),
        out_shape=jax.ShapeDtypeStruct((B, n, A_WIDTH), BF16),
        compiler_params=_cparams(("arbitrary",)),
        name="attn_a_ctx",
    )(sink, q, k, v)


def _attn_a_lat_kernel(sink_ref, q_ref, k_ref, v_ref, kc_ref, vc_ref, o_ref):
    n = k_ref.shape[1]
    band = 3 * BLOCK
    kcx = kc_ref[...]
    vcx = vc_ref[...]

    def body(sb, carry):
        blk = pl.program_id(1) * A_SUB + sb
        rows = pl.ds(pl.multiple_of(sb * BLOCK, BLOCK), BLOCK)
        start = pl.multiple_of(jnp.clip((blk - 1) * BLOCK, 0, n - band), BLOCK)
        kb = k_ref[:, pl.ds(start, band)]
        vb = v_ref[pl.ds(start, band), :]
        qpos = blk * BLOCK + lax.broadcasted_iota(jnp.int32, (BLOCK, band), 0)
        kpos = start + lax.broadcasted_iota(jnp.int32, (BLOCK, band), 1)
        mask = jnp.concatenate([jnp.abs(qpos - kpos) <= WINDOW] * A_GROUP, axis=0)
        q_blocks = [q_ref[rows, j * LANES:(j + 1) * LANES] for j in range(A_GROUP)]
        outs = [_gqa_group(g, q_blocks, [kb, kcx], [mask, None], [vb, vcx], sink_ref)
                for g in range(A_KV_HEADS)]
        _gqa_store(o_ref, rows, BLOCK, *outs)
        return carry

    lax.fori_loop(0, A_SUB, body, 0, unroll=True)


def _attn_a_lat(sink, q, k, v, k_ctx, v_ctx):
    B, n, _ = q.shape
    P = v_ctx.shape[1]
    tq = A_SUB * BLOCK
    seq = lambda w, m: pl.BlockSpec((None, m, w), lambda b, i: (b, 0, 0))
    seq_t = lambda w, m: pl.BlockSpec((w, m), lambda b, i: (0, b))
    return pl.pallas_call(
        _attn_a_lat_kernel,
        grid=(B, n // tq),
        in_specs=[pl.BlockSpec(memory_space=pltpu.SMEM),
                  pl.BlockSpec((None, tq, A_WIDTH), lambda b, i: (b, i, 0)),
                  seq_t(A_KV_WIDTH, n), seq(A_KV_WIDTH, n), seq_t(A_KV_WIDTH, P), seq(A_KV_WIDTH, P)],
        out_specs=pl.BlockSpec((None, tq, A_WIDTH), lambda b, i: (b, i, 0)),
        out_shape=jax.ShapeDtypeStruct((B, n, A_WIDTH), BF16),
        compiler_params=_cparams(("arbitrary", "arbitrary")),
        name="attn_a_lat",
    )(sink, q, k, v, k_ctx, v_ctx)


def _mla_step(q_ref, k, v, m_scr, acc_scr, first):
    reps = k.shape[0] // LANES
    for h in range(C_HEADS):
        hs = slice(h * C_HEAD_PAD, (h + 1) * C_HEAD_PAD)
        s = _dot_nt(q_ref[:, hs], k[:, hs])
        m_new = jnp.broadcast_to(jnp.max(s, axis=-1, keepdims=True), (s.shape[0], LANES))
        if not first:
            m_old = m_scr[h]
            m_new = jnp.maximum(m_old, m_new)
        p = jnp.exp2(s - jnp.concatenate([m_new] * reps, axis=1))
        m_scr[h] = m_new
        pv = _dot(p.astype(BF16), v[:, hs])
        acc_scr[h] = pv if first else acc_scr[h] * jnp.exp2(m_old - m_new) + pv


def _mla_tile(n_chunks, tk, q_ref, k_ref, v_ref, kc_ref, vc_ref, o_ref, m_scr, acc_scr):
    tq = q_ref.shape[0]

    def body(c, carry):
        ks = pl.multiple_of(c * tk, tk)
        _mla_step(q_ref, k_ref[pl.ds(ks, tk), :], v_ref[pl.ds(ks, tk), :], m_scr, acc_scr, False)
        return carry

    if kc_ref is not None:
        _mla_step(q_ref, kc_ref[...], vc_ref[...], m_scr, acc_scr, True)
        lax.fori_loop(0, n_chunks, body, 0)
    else:
        _mla_step(q_ref, k_ref[0:tk, :], v_ref[0:tk, :], m_scr, acc_scr, True)
        lax.fori_loop(1, n_chunks, body, 0)

    low = lax.broadcasted_iota(jnp.int32, (tq, LANES), 1) < C_V
    for j in range(C_HEADS // 2):
        even, odd = acc_scr[2 * j], acc_scr[2 * j + 1]
        sums = pltpu.roll(jnp.where(low, odd, even), C_V, 1)
        o_ref[:, j * LANES:(j + 1) * LANES] = (jnp.where(low, even, odd) / sums).astype(BF16)


def _mla_kernel(n_chunks, tk, has_ctx, group, q_ref, k_ref, v_ref, *rest):
    if has_ctx:
        kc_ref, vc_ref, o_ref, m_scr, acc_scr = rest
    else:
        (o_ref, m_scr, acc_scr), kc_ref, vc_ref = rest, None, None
    if group is None:
        _mla_tile(n_chunks, tk, q_ref, k_ref, v_ref, kc_ref, vc_ref, o_ref, m_scr, acc_scr)
    else:
        for e in range(group):
            _mla_tile(n_chunks, tk, q_ref.at[e], k_ref.at[e], v_ref.at[e], None, None, o_ref.at[e], m_scr, acc_scr)


def _mla(q, k, v, k_ctx, v_ctx, tq, tk, group=None):
    B, n, _ = q.shape
    nk = k.shape[1]
    has_ctx = k_ctx is not None
    lead = None if group is None else group
    nb = B if group is None else B // group
    seq = lambda m: pl.BlockSpec((lead, m, CW), lambda b, i: (b, 0, 0))
    in_specs = [pl.BlockSpec((lead, tq, CW), lambda b, i: (b, i, 0)), seq(nk), seq(nk)]
    args = [q, k, v]
    if has_ctx:
        in_specs += [seq(k_ctx.shape[1])] * 2
        args += [k_ctx, v_ctx]
    return pl.pallas_call(
        functools.partial(_mla_kernel, nk // tk, tk, has_ctx, group),
        grid=(nb, n // tq),
        in_specs=in_specs,
        out_specs=pl.BlockSpec((lead, tq, C_WIDTH), lambda b, i: (b, i, 0)),
        out_shape=jax.ShapeDtypeStruct((B, n, C_WIDTH), BF16),
        scratch_shapes=[pltpu.VMEM((C_HEADS, tq, LANES), F32), pltpu.VMEM((C_HEADS, tq, LANES), F32)],
        compiler_params=_cparams(("arbitrary", "arbitrary")),
        name="mla_lat" if has_ctx else "mla_ctx",
    )(*args)


def _dft_tables(n):
    j = np.arange(n, dtype=np.int64)
    ang = ((j[:, None] * j[None, :]) % n) * (2.0 * np.pi / n)
    return np.cos(ang), np.sin(ang)


def _channel_dft():
    c, s = _dft_tables(B_GROUP_DIM)
    eye = np.eye(B_GROUPS)
    return _table_bf16(np.concatenate([np.kron(eye, c), -np.kron(eye, s)], axis=1))


def _table_bf16(t):
    return jnp.asarray(t, F32).astype(BF16)


def _fourier_ctx_kernel(scale, x_ref, fc_ref, fn_ref, o_ref):
    for e in range(CTX_GROUP):
        u = _dot(x_ref[e], fc_ref[...])
        ucat = jnp.concatenate([u[:, :B_WIDTH], u[:, B_WIDTH:]], axis=0).astype(BF16)
        o_ref[e] = (_dot(fn_ref[...], ucat) * scale).astype(BF16)


def _fourier_ctx(fb, fc):
    B, n, _ = fb.shape
    c, s = _dft_tables(n)
    fn = _table_bf16(np.concatenate([c, s], axis=1))
    scale = float((n * B_GROUP_DIM) ** -0.5)
    return pl.pallas_call(
        functools.partial(_fourier_ctx_kernel, scale),
        grid=(B // CTX_GROUP,),
        in_specs=[pl.BlockSpec((CTX_GROUP, n, B_WIDTH), lambda b: (b, 0, 0)),
                  pl.BlockSpec(fc.shape, lambda b: (0, 0)),
                  pl.BlockSpec(fn.shape, lambda b: (0, 0))],
        out_specs=pl.BlockSpec((CTX_GROUP, n, B_WIDTH), lambda b: (b, 0, 0)),
        out_shape=jax.ShapeDtypeStruct((B, n, B_WIDTH), BF16),
        compiler_params=_cparams(("arbitrary",)),
        name="fourier_ctx",
    )(fb, fc, fn)


def _fourier_lat_kernel(scale, x_ref, fc_ref, g_ref, f_ref, o_ref, zr_scr, zi_scr):
    j = pl.program_id(1)
    nc = pl.num_programs(1) // 2

    @pl.when(j < nc)
    def _():
        for cc in range(FOURIER_COLS):
            xc = x_ref[:, cc, :].astype(BF16)
            u = _dot(xc, fc_ref[...]).astype(BF16)
            p = _dot(g_ref[cc], u)
            R = p.shape[0] // 2
            zshape = (R // FOURIER_COLS, FOURIER_COLS, B_WIDTH)
            zr_scr[j * FOURIER_COLS + cc] = (p[:R, :B_WIDTH] - p[R:, B_WIDTH:]).reshape(zshape)
            zi_scr[j * FOURIER_COLS + cc] = (p[:R, B_WIDTH:] + p[R:, :B_WIDTH]).reshape(zshape)

    @pl.when(j >= nc)
    def _():
        for kk in range(FOURIER_COLS):
            z = jnp.concatenate([zr_scr[:, j - nc, kk, :], zi_scr[:, j - nc, kk, :]], axis=0).astype(BF16)
            o_ref[:, kk, :] = _dot(f_ref[...], z) * scale


def _fourier_lat(fb, fc):
    B, n, _ = fb.shape
    R = n // GRID_W
    W = GRID_W * B_WIDTH
    kr = np.arange(R, dtype=np.int64)
    pos = GRID_W * np.arange(R, dtype=np.int64)[None, None, :] + np.arange(GRID_W, dtype=np.int64)[:, None, None]
    ang = ((kr[None, :, None] * pos) % n) * (2.0 * np.pi / n)
    g = _table_bf16(np.concatenate([np.cos(ang), -np.sin(ang)], axis=1))
    c64, s64 = _dft_tables(GRID_W)
    f2 = _table_bf16(np.concatenate([c64, s64], axis=1))
    assert R == GRID_W
    nc = GRID_W // FOURIER_COLS
    scale = float((n * B_GROUP_DIM) ** -0.5)
    blk = (None, GRID_W, FOURIER_COLS, B_WIDTH)
    zscr = pltpu.VMEM((GRID_W, R // FOURIER_COLS, FOURIER_COLS, B_WIDTH), F32)
    out = pl.pallas_call(
        functools.partial(_fourier_lat_kernel, scale),
        grid=(B, 2 * nc),
        in_specs=[pl.BlockSpec(blk, lambda b, j: (b, 0, jnp.minimum(j, nc - 1), 0)),
                  pl.BlockSpec(fc.shape, lambda b, j: (0, 0)),
                  pl.BlockSpec((FOURIER_COLS, 2 * R, R), lambda b, j: (jnp.minimum(j, nc - 1), 0, 0)),
                  pl.BlockSpec(f2.shape, lambda b, j: (0, 0))],
        out_specs=pl.BlockSpec(blk, lambda b, j: (b, 0, jnp.maximum(j - nc, 0), 0)),
        out_shape=jax.ShapeDtypeStruct((B, GRID_W, R, B_WIDTH), F32),
        scratch_shapes=[zscr, zscr],
        compiler_params=_cparams(("arbitrary", "arbitrary")),
        name="fourier_lat",
    )(fb.reshape(B, R, GRID_W, B_WIDTH), fc, g, f2)
    return out.reshape(B, n, B_WIDTH)


def _ffn_kernel(T, S, final, x_ref, xp_ref, xn_ref, a_ref, ap_ref, an_ref, b_ref, bp_ref, bn_ref, c_ref, cp_ref,
                cn_ref, mod_ref, g_ref, wout_ref, wug_ref, cw_ref, cb_ref, wd_ref, gfin_ref,
                o_ref, mix_scr, hs_scr, u0_scr, u1_scr, act_scr):
    i = pl.program_id(0)
    H = BF16_ROWS
    for r0, refs in ((0, (ap_ref, bp_ref, cp_ref)), (H, (a_ref, b_ref, c_ref)), (T + H, (an_ref, bn_ref, cn_ref))):
        rows = slice(r0, r0 + refs[0].shape[0])
        mix_scr[rows, 0:A_WIDTH] = refs[0][...]
        mix_scr[rows, A_WIDTH:A_WIDTH + B_WIDTH] = refs[1][...].astype(BF16)
        mix_scr[rows, A_WIDTH + B_WIDTH:] = refs[2][...]
    mix = _dot(mix_scr[...], wout_ref[...])
    gt1 = mod_ref[:, 2 * D_MODEL:3 * D_MODEL]
    sh2 = mod_ref[:, 3 * D_MODEL:4 * D_MODEL]
    sc2 = mod_ref[:, 4 * D_MODEL:5 * D_MODEL]

    def ffn_input(x, m):
        x1 = x + gt1 * m
        return x1, (_rms(x1, g_ref[...], D_MODEL) * (1.0 + sc2) + sh2).astype(BF16)

    at_start = (i * T) % S == 0
    at_end = ((i + 1) * T) % S == 0
    hp = ffn_input(xp_ref[...], mix[0:H])[1]
    hn = ffn_input(xn_ref[...], mix[T + H:T + 2 * H])[1]
    x1, hm = ffn_input(x_ref[...], mix[H:T + H])
    o_ref[...] = x1
    hs_scr[0:H] = jnp.where(at_start, jnp.zeros_like(hp), hp)
    hs_scr[H:T + H] = hm
    hs_scr[T + H:T + 2 * H] = jnp.where(at_end, jnp.zeros_like(hn), hn)
    if S < T:
        pos = lax.broadcasted_iota(jnp.int32, (T, 1), 0) % S
        has_prev = pos != 0
        has_next = pos != S - 1

    n_chunks = D_FF // F_CHUNK
    cols = lambda j: (pl.multiple_of(j * F_CHUNK, F_CHUNK), pl.multiple_of(j * F_CHUNK + D_FF, LANES))

    def up_proj(j, u_ref):
        for b, col in enumerate(cols(j)):
            u_ref[b] = _dot(hs_scr[...], wug_ref[:, pl.ds(col, F_CHUNK)])

    def conv(u_ref, b, col):
        u = u_ref[b]
        rows = u.shape[0]
        up = pltpu.roll(u, 1, 0)[H:T + H]
        un = pltpu.roll(u, rows - 1, 0)[H:T + H]
        if S < T:
            up = jnp.where(has_prev, up, 0.0)
            un = jnp.where(has_next, un, 0.0)
        cw = cw_ref[:, pl.ds(col, F_CHUNK)]
        return (up * cw[0:1, :] + u[H:T + H] * cw[1:2, :] + un * cw[2:3, :]
                + cb_ref[:, pl.ds(col, F_CHUNK)])

    def gate(j, u_ref):
        a, g = [conv(u_ref, b, col) for b, col in enumerate(cols(j))]
        act_scr[:, pl.ds(cols(j)[0], F_CHUNK)] = (g * jax.nn.sigmoid(g) * a).astype(BF16)

    def pair(k, carry):
        up_proj(2 * k + 1, u1_scr)
        gate(2 * k, u0_scr)
        up_proj(2 * k + 2, u0_scr)
        gate(2 * k + 1, u1_scr)
        return carry

    assert n_chunks % 2 == 1
    up_proj(0, u0_scr)
    lax.fori_loop(0, n_chunks // 2, pair, 0, unroll=True)
    gate(n_chunks - 1, u0_scr)
    gt2 = mod_ref[:, 5 * D_MODEL:6 * D_MODEL]
    x2 = o_ref[...] + gt2 * _dot(act_scr[...], wd_ref[...])
    if final:
        x2 = _rms(x2, gfin_ref[...], D_MODEL)
    o_ref[...] = x2


def _ffn(latent, final, layer, x, oa, ob, oc, mod_l, g_ffn, wout, wug, conv_w, conv_b, wdown, g_final):
    n = x.shape[0]
    T = T_FFN
    S = DEC_SEQ if latent else SEQ
    nt = n // T
    tpb = (DEC_SEQ // T) if latent else nt
    base = 1 if latent else 0
    hb = T // BF16_ROWS
    nhb = n // BF16_ROWS
    once = dict(pipeline_mode=pl.Buffered(1))
    whole = lambda a: pl.BlockSpec(a.shape, lambda i: (0,) * a.ndim, **once)
    of_layer = lambda a: pl.BlockSpec((None,) + a.shape[1:], lambda i: (layer,) + (0,) * (a.ndim - 1), **once)
    with_halos = lambda w: [
        pl.BlockSpec((T, w), lambda i: (i, 0)),
        pl.BlockSpec((BF16_ROWS, w), lambda i: (jnp.maximum(i * hb - 1, 0), 0)),
        pl.BlockSpec((BF16_ROWS, w), lambda i: (jnp.minimum((i + 1) * hb, nhb - 1), 0))]
    return pl.pallas_call(
        functools.partial(_ffn_kernel, T, S, final),
        grid=(nt,),
        in_specs=[spec for a in (x, oa, ob, oc) for spec in with_halos(a.shape[1])] + [
            pl.BlockSpec((None, 1, 6 * D_MODEL), lambda i: (base + i // tpb, 0, 0)),
            whole(g_ffn), whole(wout),
            of_layer(wug), of_layer(conv_w), of_layer(conv_b), of_layer(wdown), whole(g_final),
        ],
        out_specs=pl.BlockSpec((T, D_MODEL), lambda i: (i, 0)),
        out_shape=jax.ShapeDtypeStruct((n, D_MODEL), F32),
        scratch_shapes=[pltpu.VMEM((T + 2 * BF16_ROWS, D_MODEL), BF16),
                        pltpu.VMEM((T + 2 * BF16_ROWS, D_MODEL), BF16),
                        pltpu.VMEM((2, T + 2 * BF16_ROWS, F_CHUNK), F32),
                        pltpu.VMEM((2, T + 2 * BF16_ROWS, F_CHUNK), F32),
                        pltpu.VMEM((T, D_FF), BF16)],
        compiler_params=_cparams(("arbitrary",)),
        name="ffn_lat" if latent else "ffn_ctx",
    )(x, x, x, oa, oa, oa, ob, ob, ob, oc, oc, oc, mod_l, g_ffn, wout, wug, conv_w, conv_b, wdown, g_final)


def _rope_tables(n_tok, dim, lane0, width, lanes=None):
    rows = n_tok // GRID_W
    r = np.repeat(np.arange(rows), GRID_W).astype(np.float64)
    col = np.tile(np.arange(GRID_W), rows).astype(np.float64)
    quarter = dim // 4
    inv = ROPE_BASE ** (-np.arange(quarter, dtype=np.float64) / quarter)
    ang_r = r[:, None] * inv
    ang_c = col[:, None] * inv
    cos = np.concatenate([np.cos(ang_r)] * 2 + [np.cos(ang_c)] * 2, axis=1)
    sin = np.concatenate([-np.sin(ang_r), np.sin(ang_r), -np.sin(ang_c), np.sin(ang_c)], axis=1)
    if lanes is not None:
        cos_b, sin_b = np.ones((n_tok, LANES)), np.zeros((n_tok, LANES))
        cos_b[:, lanes], sin_b[:, lanes] = cos, sin
        return jnp.asarray(cos_b, F32), jnp.asarray(sin_b, F32)
    reps = width // dim
    cos = np.concatenate([cos] * reps, axis=1)
    sin = np.concatenate([sin] * reps, axis=1)
    pad = ((0, 0), (lane0, LANES - lane0 - width))
    return jnp.asarray(np.pad(cos, pad, constant_values=1.0), F32), jnp.asarray(np.pad(sin, pad), F32)


def _place(a, lanes, width=LANES):
    src = np.full((width,), a.shape[-1], np.int32)
    src[np.asarray(lanes)] = np.arange(len(lanes))
    ext = jnp.concatenate([a, jnp.zeros(a.shape[:-1] + (1,), a.dtype)], axis=-1)
    return ext[..., src]


def _layer_weights(w_in, g_cq, w_uq, w_ukv):
    z = lambda r, c: jnp.zeros((r, c), F32)
    o = np.cumsum([0, A_WIDTH, A_KV_WIDTH, A_KV_WIDTH, B_WIDTH, C_Q_LORA, C_KV_LORA, C_ROPE])
    win_p = jnp.concatenate([
        w_in[:, o[0]:o[1]].reshape(D_MODEL, A_HEADS, HEAD_DIM)[:, A_HEAD_ORDER, :].reshape(D_MODEL, A_WIDTH),
        w_in[:, o[1]:o[4]],
        w_in[:, o[4]:o[5]], z(D_MODEL, CQ_PAD - C_Q_LORA),
        w_in[:, o[5]:o[6]],
        _place(w_in[:, o[6]:o[7]], C_ROPE_LANES) + _place(w_in[:, o[6]:o[7]], C_STATE_LANES),
    ], axis=1).astype(BF16)
    gcq_p = jnp.pad(g_cq, (0, CQ_PAD - C_Q_LORA)).reshape(1, CQ_PAD)
    hq = C_NOPE + C_ROPE
    wuq_h = w_uq.reshape(C_Q_LORA, C_HEADS, hq)
    wuq_p = _place(wuq_h, C_NOPE_LANES + C_ROPE_LANES)
    wuq_p = jnp.pad(wuq_p, ((0, CQ_PAD - C_Q_LORA), (0, 0), (0, 0)))
    wuq_p = wuq_p.reshape(CQ_PAD, C_HEADS * C_HEAD_PAD).astype(BF16)
    wukv_h = w_ukv.reshape(C_KV_LORA, C_HEADS, C_NOPE + C_V)
    wk = _place(wukv_h[:, :, :C_NOPE], C_NOPE_LANES)
    zv = jnp.zeros((C_KV_LORA, C_HEAD_PAD - C_V), F32)
    wv = [wukv_h[:, h, C_NOPE:] for h in range(C_HEADS)]
    wv = jnp.concatenate([jnp.concatenate([zv, w] if h % 2 else [w, zv], axis=1) for h, w in enumerate(wv)], axis=1)
    wukv_p = jnp.concatenate([wk.reshape(C_KV_LORA, CW), wv], axis=1).astype(BF16)
    return win_p, gcq_p, wuq_p, wukv_p


def kernel(x_prompt, x_sample, cache_win_k, cache_win_v, cache_mla_ckv, cache_mla_krope, c, c_ctx,
           w_ada, b_ada, g_mix, w_in, sink, g_cq, w_uq, g_ckv, w_ukv, w_out, g_ffn, w_ug, conv_w,
           conv_b, w_down, g_final):
    n_ctx = BATCH * SEQ
    n_lat = DEC_BATCH * DEC_SEQ
    xp = x_prompt.reshape(n_ctx, D_MODEL)
    xs = x_sample.reshape(n_lat, D_MODEL)

    cvecs = jnp.concatenate([c_ctx[None, :], c, jnp.zeros((8 - 1 - DEC_BATCH, D_MODEL), F32)], axis=0)
    mod = _modulation(cvecs, w_ada, b_ada)

    cos_a, sin_a = _rope_tables(DEC_SEQ, HEAD_DIM, 0, LANES)
    cos_c, sin_c = _rope_tables(DEC_SEQ, C_ROPE, 0, C_ROPE, lanes=C_ROPE_LANES)
    tables = (cos_a, sin_a, cos_c, sin_c)
    fc = _channel_dft()
    g_final2 = g_final.reshape(1, D_MODEL)
    wug_all = w_ug.astype(BF16)
    wdown_all = w_down.astype(BF16)
    conv_b_all = conv_b.reshape(DEPTH, 1, 2 * D_FF)

    new_k, new_v, new_ckv, new_kr = [], [], [], []
    for l in range(DEPTH):
        win_p, gcq_p, wuq_p, wukv_p = _layer_weights(w_in[l], g_cq[l], w_uq[l], w_ukv[l])
        mod_l = mod[l].reshape(8, 1, 6 * D_MODEL)
        g_mix_l = g_mix[l].reshape(1, D_MODEL)
        g_ckv_l = g_ckv[l].reshape(1, C_KV_LORA)
        g_ffn_l = g_ffn[l].reshape(1, D_MODEL)
        sink_l = sink[l].reshape(1, A_HEADS)
        wout_a = w_out[l, :A_WIDTH].reshape(A_HEADS, HEAD_DIM, D_MODEL)[jnp.array(A_HEAD_ORDER)]
        wout_l = jnp.concatenate([wout_a.reshape(A_WIDTH, D_MODEL), w_out[l, A_WIDTH:]], axis=0).astype(BF16)
        final = l == DEPTH - 1

        (qa, ka, va, fb, qc, kc, vc, ska, sva, sckv, skr) = _projection(
            False, xp, mod_l, g_mix_l, win_p, gcq_p, wuq_p, g_ckv_l, wukv_p, None)
        r3 = lambda a: a.reshape(BATCH, SEQ, a.shape[-1])
        oa = _attn_a_ctx(sink_l, r3(qa), ka, r3(va))
        ob = _fourier_ctx(r3(fb), fc)
        oc = _mla(r3(qc), r3(kc), r3(vc), None, None, SEQ, SEQ, group=CTX_GROUP)
        f2 = lambda a: a.reshape(n_ctx, a.shape[-1])
        xp = _ffn(False, final, l, xp, f2(oa), f2(ob), f2(oc), mod_l, g_ffn_l, wout_l, wug_all, conv_w, conv_b_all,
                  wdown_all, g_final2)
        new_k.append(ska.reshape(BATCH, SEQ, A_KV_HEADS, HEAD_DIM))
        new_v.append(sva.reshape(BATCH, SEQ, A_KV_HEADS, HEAD_DIM))
        new_ckv.append(sckv.reshape(BATCH, SEQ, C_KV_LORA))
        new_kr.append(skr.reshape(BATCH, SEQ, C_ROPE))

        (qa, ka, va, fb, qc, kc, vc) = _projection(
            True, xs, mod_l, g_mix_l, win_p, gcq_p, wuq_p, g_ckv_l, wukv_p, tables)
        r3 = lambda a: a.reshape(DEC_BATCH, DEC_SEQ, a.shape[-1])
        kwin = cache_win_k[:, l].reshape(DEC_BATCH * PAST_LEN, A_KV_WIDTH).T.astype(BF16)
        vwin = cache_win_v[:, l].reshape(DEC_BATCH, PAST_LEN, A_KV_WIDTH).astype(BF16)
        oa = _attn_a_lat(sink_l, r3(qa), ka, r3(va), kwin, vwin)
        ob = _fourier_lat(r3(fb), fc)
        kr_pad = _place(cache_mla_krope[:, l], C_ROPE_LANES)
        kc_ctx, vc_ctx = _kvcache(cache_mla_ckv[:, l], kr_pad, wukv_p)
        oc = _mla(r3(qc), r3(kc), r3(vc), kc_ctx, vc_ctx, TQ_MLA, TK_MLA)
        f2 = lambda a: a.reshape(n_lat, a.shape[-1])
        xs = _ffn(True, final, l, xs, f2(oa), f2(ob), f2(oc), mod_l, g_ffn_l, wout_l, wug_all, conv_w, conv_b_all,
                  wdown_all, g_final2)

    y_prompt = xp.reshape(BATCH, SEQ, D_MODEL)
    y_sample = xs.reshape(DEC_BATCH, DEC_SEQ, D_MODEL)
    return (y_prompt, y_sample, jnp.stack(new_k, axis=1), jnp.stack(new_v, axis=1),
            jnp.stack(new_ckv, axis=1), jnp.stack(new_kr, axis=1))
```

```python
import functools

import numpy as np
import jax
import jax.numpy as jnp
from jax import lax
from jax.experimental import pallas as pl
from jax.experimental.pallas import tpu as pltpu

F32 = jnp.float32
BF16 = jnp.bfloat16

D_MODEL = 1024
BATCH = 16
SEQ = 256
DEPTH = 2
DEC_BATCH = 4
DEC_SEQ = 4096
PAST_LEN = 256
GRID_W = 64
HEAD_DIM = 64
A_HEADS = 8
A_KV_HEADS = 2
A_GROUP = 4
A_WIDTH = 512
A_KV_WIDTH = 128
WINDOW = 128
BLOCK = 128
B_WIDTH = 256
B_GROUP_DIM = 64
B_GROUPS = 4
C_HEADS = 4
C_NOPE = 64
C_ROPE = 32
C_V = 64
C_Q_LORA = 192
C_KV_LORA = 128
C_WIDTH = 256
D_FF = 2816
ROPE_BASE = 10000.0
EPS = 1e-6
NEG = -1e30

LANES = 128
BF16_ROWS = 16
C_HEAD_PAD = 128
CW = C_HEADS * C_HEAD_PAD
CQ_PAD = 256
VMEM_LIMIT = 56 * 1024 * 1024

_QA0, _KA0, _VA0, _FB0, _CQ0, _CKV0, _KR0, _IN_PAD = 0, 512, 640, 768, 1024, 1280, 1408, 1536
_q8 = C_ROPE // 4
C_NOPE_LANES = list(range(0, 48)) + list(range(64, 80))
C_ROPE_LANES = (list(range(48, 48 + _q8)) + list(range(112, 112 + _q8))
                + list(range(48 + _q8, 48 + 2 * _q8)) + list(range(112 + _q8, 112 + 2 * _q8)))
C_STATE_LANES = list(range(80, 80 + C_ROPE))

T_PROJ = 1024
PROJ_SPLIT = 2
T_FFN = 1024
F_CHUNK = 256
TQ_MLA = 1024
TK_MLA = 2048
A_SUB = 8
CTX_GROUP = 4
A_HEAD_ORDER = [h for j in range(A_GROUP) for h in (j, A_GROUP + j)]
LOG2_E = float(np.log2(np.e))
FOURIER_COLS = 32


def _cparams(sem):
    return pltpu.CompilerParams(dimension_semantics=sem, vmem_limit_bytes=VMEM_LIMIT)


def _dot(a, b):
    return jnp.dot(a, b, preferred_element_type=F32)


def _dot_nt(a, b):
    return lax.dot_general(a, b, (((1,), (1,)), ((), ())), preferred_element_type=F32)


def _rms(x, g, n):
    ms = jnp.sum(x * x, axis=-1, keepdims=True) * (1.0 / n)
    return x * lax.rsqrt(ms + EPS) * g


def _split_bf16(a):
    hi = a.astype(BF16)
    return hi, (a - hi.astype(F32)).astype(BF16)


def _mod_kernel(c_ref, w_ref, b_ref, o_ref):
    cv = c_ref[...]
    s = cv * jax.nn.sigmoid(cv)
    s_hi, s_lo = _split_bf16(s)
    w_hi, w_lo = _split_bf16(w_ref[...])
    o_ref[...] = _dot(s_hi, w_hi) + (_dot(s_hi, w_lo) + _dot(s_lo, w_hi)) + b_ref[...]


def _modulation(cvecs, w_ada, b_ada):
    nj = 6
    return pl.pallas_call(
        _mod_kernel,
        grid=(DEPTH, nj),
        in_specs=[
            pl.BlockSpec((8, D_MODEL), lambda l, j: (0, 0)),
            pl.BlockSpec((None, D_MODEL, D_MODEL), lambda l, j: (l, 0, j)),
            pl.BlockSpec((None, 1, D_MODEL), lambda l, j: (l, 0, j)),
        ],
        out_specs=pl.BlockSpec((None, 8, D_MODEL), lambda l, j: (l, 0, j)),
        out_shape=jax.ShapeDtypeStruct((DEPTH, 8, 6 * D_MODEL), F32),
        compiler_params=_cparams(("arbitrary", "arbitrary")),
        name="modulation",
    )(cvecs, w_ada, b_ada.reshape(DEPTH, 1, 6 * D_MODEL))


def _rope_block(x, cos, sin, half):
    if 2 * half == LANES:
        partner = pltpu.roll(x, half, 1)
    else:
        lane = lax.broadcasted_iota(jnp.int32, x.shape, 1)
        first = (lane % (2 * half)) < half
        partner = jnp.where(first, pltpu.roll(x, LANES - half, 1), pltpu.roll(x, half, 1))
    return x * cos + partner * sin


def _rope(x, cos, sin, half):
    blocks = [_rope_block(x[:, j:j + LANES], cos, sin, half) for j in range(0, x.shape[1], LANES)]
    return blocks[0] if len(blocks) == 1 else jnp.concatenate(blocks, axis=1)


def _proj_kernel(latent, x_ref, mod_ref, g_ref, win_ref, gcq_ref, wuq_ref, gckv_ref, wukv_ref, *rest):
    n_tab = 4 if latent else 0
    tables, outs = rest[:n_tab], rest[n_tab:]
    split = PROJ_SPLIT if latent else 1
    rows = x_ref.shape[0] // split
    for r in range(split):
        rs = pl.ds(r * rows, rows)
        outs_r = [o.at[:, rs] if i == 1 else o.at[rs] for i, o in enumerate(outs)]
        _proj_rows(latent, x_ref.at[rs], mod_ref, g_ref, win_ref, gcq_ref, wuq_ref, gckv_ref, wukv_ref,
                   *[t.at[rs] for t in tables], *outs_r)


def _proj_rows(latent, x_ref, mod_ref, g_ref, win_ref, gcq_ref, wuq_ref, gckv_ref, wukv_ref, *rest):
    if latent:
        cosa_ref, sina_ref, cosc_ref, sinc_ref = rest[:4]
        qa_ref, ka_ref, va_ref, fb_ref, qc_ref, kc_ref, vc_ref = rest[4:]
    else:
        qa_ref, ka_ref, va_ref, fb_ref, qc_ref, kc_ref, vc_ref, ska_ref, sva_ref, sckv_ref, skr_ref = rest
    x = x_ref[...]
    y = _rms(x, g_ref[...], D_MODEL)
    sh1 = mod_ref[:, 0:D_MODEL]
    sc1 = mod_ref[:, D_MODEL:2 * D_MODEL]
    h = (y * (1.0 + sc1) + sh1).astype(BF16)

    qa = _dot(h, win_ref[:, _QA0:_KA0])
    kva = _dot(h, win_ref[:, _KA0:_FB0])
    ka, va = kva[:, :A_KV_WIDTH], kva[:, A_KV_WIDTH:]
    fb = _dot(h, win_ref[:, _FB0:_CQ0])
    cq = _dot(h, win_ref[:, _CQ0:_CKV0])
    ckr = _dot(h, win_ref[:, _CKV0:_IN_PAD])
    ckv, kr = ckr[:, :C_KV_LORA], ckr[:, C_KV_LORA:]

    if not latent:
        ska_ref[...] = ka
        sva_ref[...] = va
        skr_ref[...] = kr[:, C_STATE_LANES[0]:C_STATE_LANES[-1] + 1]
    else:
        qa = _rope(qa, cosa_ref[...], sina_ref[...], HEAD_DIM // 4)
        ka = _rope(ka, cosa_ref[...], sina_ref[...], HEAD_DIM // 4)
        kr = _rope(kr, cosc_ref[...], sinc_ref[...], LANES // 2)
    qa_ref[...] = (qa * (LOG2_E * HEAD_DIM ** -0.5)).astype(BF16)
    ka_ref[...] = ka.T.astype(BF16)
    va_ref[...] = va.astype(BF16)
    fb_ref[...] = fb.astype(fb_ref.dtype)

    cqn = _rms(cq, gcq_ref[...], C_Q_LORA).astype(BF16)
    qc = _dot(cqn, wuq_ref[...])
    if latent:
        qc = _rope(qc, cosc_ref[...], sinc_ref[...], LANES // 2)
    qc_ref[...] = (qc * (LOG2_E * (C_NOPE + C_ROPE) ** -0.5)).astype(BF16)

    ckvn = _rms(ckv, gckv_ref[...], C_KV_LORA)
    if not latent:
        sckv_ref[...] = ckvn
    _store_kv(_dot(ckvn.astype(BF16), wukv_ref[...]), kr, kc_ref, vc_ref)


def _store_kv(kv, kr, kc_ref, vc_ref):
    kc_ref[...] = (kv[:, :CW] + jnp.concatenate([kr] * C_HEADS, axis=1)).astype(BF16)
    lane = lax.broadcasted_iota(jnp.int32, (1, CW), 1)
    upper = lane % C_HEAD_PAD >= C_V
    odd = (lane // C_HEAD_PAD) % 2 == 1
    vc_ref[...] = (kv[:, CW:] + jnp.where(upper != odd, 1.0, 0.0).astype(F32)).astype(BF16)


def _projection(latent, x, mod_l, g_mix, win_p, gcq_p, wuq_p, gckv, wukv_p, tables):
    n = x.shape[0]
    T = T_PROJ
    nt = n // T
    tpb = (DEC_SEQ // T) if latent else nt
    base = 1 if latent else 0
    tok = lambda w: pl.BlockSpec((T, w), lambda i: (i, 0))
    full = lambda a: pl.BlockSpec(a.shape, lambda i: (0,) * a.ndim)
    in_specs = [
        tok(D_MODEL),
        pl.BlockSpec((None, 1, 6 * D_MODEL), lambda i: (base + i // tpb, 0, 0)),
        full(g_mix), full(win_p), full(gcq_p), full(wuq_p), full(gckv), full(wukv_p),
    ]
    args = [x, mod_l, g_mix, win_p, gcq_p, wuq_p, gckv, wukv_p]
    if latent:
        in_specs += [pl.BlockSpec((T, LANES), lambda i: (i % tpb, 0))] * 4
        args += list(tables)
    widths = [A_WIDTH, A_KV_WIDTH, A_KV_WIDTH, B_WIDTH, CW, CW, CW]
    out_specs = [tok(w) for w in widths]
    out_shape = [jax.ShapeDtypeStruct((n, w), BF16) for w in widths]
    out_specs[1] = pl.BlockSpec((A_KV_WIDTH, T), lambda i: (0, i))
    out_shape[1] = jax.ShapeDtypeStruct((A_KV_WIDTH, n), BF16)
    if latent:
        out_shape[3] = jax.ShapeDtypeStruct((n, B_WIDTH), F32)
    if not latent:
        sw = [A_KV_WIDTH, A_KV_WIDTH, C_KV_LORA, C_ROPE]
        out_specs += [tok(w) for w in sw]
        out_shape += [jax.ShapeDtypeStruct((n, w), F32) for w in sw]
    return pl.pallas_call(
        functools.partial(_proj_kernel, latent),
        grid=(nt,),
        in_specs=in_specs,
        out_specs=out_specs,
        out_shape=out_shape,
        compiler_params=_cparams(("arbitrary",)),
        name="proj_lat" if latent else "proj_ctx",
    )(*args)


def _kvcache_kernel(ckv_ref, kr_ref, wukv_ref, kc_ref, vc_ref):
    _store_kv(_dot(ckv_ref[...].astype(BF16), wukv_ref[...]), kr_ref[...], kc_ref, vc_ref)


def _kvcache(ckv, kr_pad, wukv_p):
    B, P, _ = ckv.shape
    return pl.pallas_call(
        _kvcache_kernel,
        grid=(B,),
        in_specs=[
            pl.BlockSpec((None, P, C_KV_LORA), lambda b: (b, 0, 0)),
            pl.BlockSpec((None, P, LANES), lambda b: (b, 0, 0)),
            pl.BlockSpec(wukv_p.shape, lambda b: (0, 0)),
        ],
        out_specs=[pl.BlockSpec((None, P, CW), lambda b: (b, 0, 0))] * 2,
        out_shape=[jax.ShapeDtypeStruct((B, P, CW), BF16)] * 2,
        compiler_params=_cparams(("arbitrary",)),
        name="kvcache",
    )(ckv, kr_pad, wukv_p)


def _gqa_group(g, q_blocks, keys, masks, values, sink_ref):
    tq = q_blocks[0].shape[0]
    lane = lax.broadcasted_iota(jnp.int32, (tq, LANES), 1)
    mine = (lane >= HEAD_DIM) if g else (lane < HEAD_DIM)
    qs = jnp.concatenate([jnp.where(mine, qb, jnp.zeros_like(qb)) for qb in q_blocks], axis=0)
    parts = []
    for k, mask in zip(keys, masks):
        s = _dot(qs, k)
        parts.append(s if mask is None else jnp.where(mask, s, NEG))
    s = parts[0] if len(parts) == 1 else jnp.concatenate(parts, axis=1)
    rowblk = lax.broadcasted_iota(jnp.int32, (A_GROUP * tq, 1), 0) // tq
    sk = jnp.full((A_GROUP * tq, 1), sink_ref[0, g * A_GROUP] * LOG2_E, F32)
    for j in range(1, A_GROUP):
        sk = jnp.where(rowblk == j, sink_ref[0, g * A_GROUP + j] * LOG2_E, sk)
    m = jnp.maximum(jnp.max(s, axis=-1, keepdims=True), sk)
    p = jnp.exp2(s - m)
    l = jnp.sum(p, axis=-1, keepdims=True) + jnp.exp2(sk - m)
    v = values[0] if len(values) == 1 else jnp.concatenate(values, axis=0)
    return _dot(p.astype(BF16), v) * (1.0 / l)


def _gqa_store(o_ref, rows, tq, o0, o1):
    low = lax.broadcasted_iota(jnp.int32, (tq, LANES), 1) < HEAD_DIM
    for j in range(A_GROUP):
        blk = jnp.where(low, o0[j * tq:(j + 1) * tq], o1[j * tq:(j + 1) * tq])
        o_ref[rows, j * LANES:(j + 1) * LANES] = blk.astype(BF16)


def _attn_a_ctx_kernel(sink_ref, q_ref, k_ref, v_ref, o_ref):
    n = q_ref.shape[1]
    for e in range(CTX_GROUP):
        q_blocks = [q_ref[e, :, j * LANES:(j + 1) * LANES] for j in range(A_GROUP)]
        k = k_ref[:, e * n:(e + 1) * n]
        outs = [_gqa_group(g, q_blocks, [k], [None], [v_ref[e]], sink_ref) for g in range(A_KV_HEADS)]
        _gqa_store(o_ref.at[e], slice(None), n, *outs)


def _attn_a_ctx(sink, q, k, v):
    B, n, _ = q.shape
    G = CTX_GROUP
    blk = lambda w: pl.BlockSpec((G, n, w), lambda b: (b, 0, 0))
    return pl.pallas_call(
        _attn_a_ctx_kernel,
        grid=(B // G,),
        in_specs=[pl.BlockSpec(memory_space=pltpu.SMEM), blk(A_WIDTH),
                  pl.BlockSpec((A_KV_WIDTH, G * n), lambda b: (0, b)), blk(A_KV_WIDTH)],
        out_specs=blk(A_WIDTH),
        out_shape=jax.ShapeDtypeStruct((B, n, A_WIDTH), BF16),
        compiler_params=_cparams(("arbitrary",)),
        name="attn_a_ctx",
    )(sink, q, k, v)


def _attn_a_lat_kernel(sink_ref, q_ref, k_ref, v_ref, kc_ref, vc_ref, o_ref):
    n = k_ref.shape[1]
    band = 3 * BLOCK
    kcx = kc_ref[...]
    vcx = vc_ref[...]

    def body(sb, carry):
        blk = pl.program_id(1) * A_SUB + sb
        rows = pl.ds(pl.multiple_of(sb * BLOCK, BLOCK), BLOCK)
        start = pl.multiple_of(jnp.clip((blk - 1) * BLOCK, 0, n - band), BLOCK)
        kb = k_ref[:, pl.ds(start, band)]
        vb = v_ref[pl.ds(start, band), :]
        qpos = blk * BLOCK + lax.broadcasted_iota(jnp.int32, (BLOCK, band), 0)
        kpos = start + lax.broadcasted_iota(jnp.int32, (BLOCK, band), 1)
        mask = jnp.concatenate([jnp.abs(qpos - kpos) <= WINDOW] * A_GROUP, axis=0)
        q_blocks = [q_ref[rows, j * LANES:(j + 1) * LANES] for j in range(A_GROUP)]
        outs = [_gqa_group(g, q_blocks, [kb, kcx], [mask, None], [vb, vcx], sink_ref)
                for g in range(A_KV_HEADS)]
        _gqa_store(o_ref, rows, BLOCK, *outs)
        return carry

    lax.fori_loop(0, A_SUB, body, 0, unroll=True)


def _attn_a_lat(sink, q, k, v, k_ctx, v_ctx):
    B, n, _ = q.shape
    P = v_ctx.shape[1]
    tq = A_SUB * BLOCK
    seq = lambda w, m: pl.BlockSpec((None, m, w), lambda b, i: (b, 0, 0))
    seq_t = lambda w, m: pl.BlockSpec((w, m), lambda b, i: (0, b))
    return pl.pallas_call(
        _attn_a_lat_kernel,
        grid=(B, n // tq),
        in_specs=[pl.BlockSpec(memory_space=pltpu.SMEM),
                  pl.BlockSpec((None, tq, A_WIDTH), lambda b, i: (b, i, 0)),
                  seq_t(A_KV_WIDTH, n), seq(A_KV_WIDTH, n), seq_t(A_KV_WIDTH, P), seq(A_KV_WIDTH, P)],
        out_specs=pl.BlockSpec((None, tq, A_WIDTH), lambda b, i: (b, i, 0)),
        out_shape=jax.ShapeDtypeStruct((B, n, A_WIDTH), BF16),
        compiler_params=_cparams(("arbitrary", "arbitrary")),
        name="attn_a_lat",
    )(sink, q, k, v, k_ctx, v_ctx)


def _mla_step(q_ref, k, v, m_scr, acc_scr, first):
    reps = k.shape[0] // LANES
    for h in range(C_HEADS):
        hs = slice(h * C_HEAD_PAD, (h + 1) * C_HEAD_PAD)
        s = _dot_nt(q_ref[:, hs], k[:, hs])
        m_new = jnp.broadcast_to(jnp.max(s, axis=-1, keepdims=True), (s.shape[0], LANES))
        if not first:
            m_old = m_scr[h]
            m_new = jnp.maximum(m_old, m_new)
        p = jnp.exp2(s - jnp.concatenate([m_new] * reps, axis=1))
        m_scr[h] = m_new
        pv = _dot(p.astype(BF16), v[:, hs])
        acc_scr[h] = pv if first else acc_scr[h] * jnp.exp2(m_old - m_new) + pv


def _mla_tile(n_chunks, tk, q_ref, k_ref, v_ref, kc_ref, vc_ref, o_ref, m_scr, acc_scr):
    tq = q_ref.shape[0]

    def body(c, carry):
        ks = pl.multiple_of(c * tk, tk)
        _mla_step(q_ref, k_ref[pl.ds(ks, tk), :], v_ref[pl.ds(ks, tk), :], m_scr, acc_scr, False)
        return carry

    if kc_ref is not None:
        _mla_step(q_ref, kc_ref[...], vc_ref[...], m_scr, acc_scr, True)
        lax.fori_loop(0, n_chunks, body, 0)
    else:
        _mla_step(q_ref, k_ref[0:tk, :], v_ref[0:tk, :], m_scr, acc_scr, True)
        lax.fori_loop(1, n_chunks, body, 0)

    low = lax.broadcasted_iota(jnp.int32, (tq, LANES), 1) < C_V
    for j in range(C_HEADS // 2):
        even, odd = acc_scr[2 * j], acc_scr[2 * j + 1]
        sums = pltpu.roll(jnp.where(low, odd, even), C_V, 1)
        o_ref[:, j * LANES:(j + 1) * LANES] = (jnp.where(low, even, odd) / sums).astype(BF16)


def _mla_kernel(n_chunks, tk, has_ctx, group, q_ref, k_ref, v_ref, *rest):
    if has_ctx:
        kc_ref, vc_ref, o_ref, m_scr, acc_scr = rest
    else:
        (o_ref, m_scr, acc_scr), kc_ref, vc_ref = rest, None, None
    if group is None:
        _mla_tile(n_chunks, tk, q_ref, k_ref, v_ref, kc_ref, vc_ref, o_ref, m_scr, acc_scr)
    else:
        for e in range(group):
            _mla_tile(n_chunks, tk, q_ref.at[e], k_ref.at[e], v_ref.at[e], None, None, o_ref.at[e], m_scr, acc_scr)


def _mla(q, k, v, k_ctx, v_ctx, tq, tk, group=None):
    B, n, _ = q.shape
    nk = k.shape[1]
    has_ctx = k_ctx is not None
    lead = None if group is None else group
    nb = B if group is None else B // group
    seq = lambda m: pl.BlockSpec((lead, m, CW), lambda b, i: (b, 0, 0))
    in_specs = [pl.BlockSpec((lead, tq, CW), lambda b, i: (b, i, 0)), seq(nk), seq(nk)]
    args = [q, k, v]
    if has_ctx:
        in_specs += [seq(k_ctx.shape[1])] * 2
        args += [k_ctx, v_ctx]
    return pl.pallas_call(
        functools.partial(_mla_kernel, nk // tk, tk, has_ctx, group),
        grid=(nb, n // tq),
        in_specs=in_specs,
        out_specs=pl.BlockSpec((lead, tq, C_WIDTH), lambda b, i: (b, i, 0)),
        out_shape=jax.ShapeDtypeStruct((B, n, C_WIDTH), BF16),
        scratch_shapes=[pltpu.VMEM((C_HEADS, tq, LANES), F32), pltpu.VMEM((C_HEADS, tq, LANES), F32)],
        compiler_params=_cparams(("arbitrary", "arbitrary")),
        name="mla_lat" if has_ctx else "mla_ctx",
    )(*args)


def _dft_tables(n):
    j = np.arange(n, dtype=np.int64)
    ang = ((j[:, None] * j[None, :]) % n) * (2.0 * np.pi / n)
    return np.cos(ang), np.sin(ang)


def _channel_dft():
    c, s = _dft_tables(B_GROUP_DIM)
    eye = np.eye(B_GROUPS)
    return _table_bf16(np.concatenate([np.kron(eye, c), -np.kron(eye, s)], axis=1))


def _table_bf16(t):
    return jnp.asarray(t, F32).astype(BF16)


def _fourier_ctx_kernel(scale, x_ref, fc_ref, fn_ref, o_ref):
    for e in range(CTX_GROUP):
        u = _dot(x_ref[e], fc_ref[...])
        ucat = jnp.concatenate([u[:, :B_WIDTH], u[:, B_WIDTH:]], axis=0).astype(BF16)
        o_ref[e] = (_dot(fn_ref[...], ucat) * scale).astype(BF16)


def _fourier_ctx(fb, fc):
    B, n, _ = fb.shape
    c, s = _dft_tables(n)
    fn = _table_bf16(np.concatenate([c, s], axis=1))
    scale = float((n * B_GROUP_DIM) ** -0.5)
    return pl.pallas_call(
        functools.partial(_fourier_ctx_kernel, scale),
        grid=(B // CTX_GROUP,),
        in_specs=[pl.BlockSpec((CTX_GROUP, n, B_WIDTH), lambda b: (b, 0, 0)),
                  pl.BlockSpec(fc.shape, lambda b: (0, 0)),
                  pl.BlockSpec(fn.shape, lambda b: (0, 0))],
        out_specs=pl.BlockSpec((CTX_GROUP, n, B_WIDTH), lambda b: (b, 0, 0)),
        out_shape=jax.ShapeDtypeStruct((B, n, B_WIDTH), BF16),
        compiler_params=_cparams(("arbitrary",)),
        name="fourier_ctx",
    )(fb, fc, fn)


def _fourier_lat_kernel(scale, x_ref, fc_ref, g_ref, f_ref, o_ref, zr_scr, zi_scr):
    j = pl.program_id(1)
    nc = pl.num_programs(1) // 2

    @pl.when(j < nc)
    def _():
        for cc in range(FOURIER_COLS):
            xc = x_ref[:, cc, :].astype(BF16)
            u = _dot(xc, fc_ref[...]).astype(BF16)
            p = _dot(g_ref[cc], u)
            R = p.shape[0] // 2
            zshape = (R // FOURIER_COLS, FOURIER_COLS, B_WIDTH)
            zr_scr[j * FOURIER_COLS + cc] = (p[:R, :B_WIDTH] - p[R:, B_WIDTH:]).reshape(zshape)
            zi_scr[j * FOURIER_COLS + cc] = (p[:R, B_WIDTH:] + p[R:, :B_WIDTH]).reshape(zshape)

    @pl.when(j >= nc)
    def _():
        for kk in range(FOURIER_COLS):
            z = jnp.concatenate([zr_scr[:, j - nc, kk, :], zi_scr[:, j - nc, kk, :]], axis=0).astype(BF16)
            o_ref[:, kk, :] = _dot(f_ref[...], z) * scale


def _fourier_lat(fb, fc):
    B, n, _ = fb.shape
    R = n // GRID_W
    W = GRID_W * B_WIDTH
    kr = np.arange(R, dtype=np.int64)
    pos = GRID_W * np.arange(R, dtype=np.int64)[None, None, :] + np.arange(GRID_W, dtype=np.int64)[:, None, None]
    ang = ((kr[None, :, None] * pos) % n) * (2.0 * np.pi / n)
    g = _table_bf16(np.concatenate([np.cos(ang), -np.sin(ang)], axis=1))
    c64, s64 = _dft_tables(GRID_W)
    f2 = _table_bf16(np.concatenate([c64, s64], axis=1))
    assert R == GRID_W
    nc = GRID_W // FOURIER_COLS
    scale = float((n * B_GROUP_DIM) ** -0.5)
    blk = (None, GRID_W, FOURIER_COLS, B_WIDTH)
    zscr = pltpu.VMEM((GRID_W, R // FOURIER_COLS, FOURIER_COLS, B_WIDTH), F32)
    out = pl.pallas_call(
        functools.partial(_fourier_lat_kernel, scale),
        grid=(B, 2 * nc),
        in_specs=[pl.BlockSpec(blk, lambda b, j: (b, 0, jnp.minimum(j, nc - 1), 0)),
                  pl.BlockSpec(fc.shape, lambda b, j: (0, 0)),
                  pl.BlockSpec((FOURIER_COLS, 2 * R, R), lambda b, j: (jnp.minimum(j, nc - 1), 0, 0)),
                  pl.BlockSpec(f2.shape, lambda b, j: (0, 0))],
        out_specs=pl.BlockSpec(blk, lambda b, j: (b, 0, jnp.maximum(j - nc, 0), 0)),
        out_shape=jax.ShapeDtypeStruct((B, GRID_W, R, B_WIDTH), F32),
        scratch_shapes=[zscr, zscr],
        compiler_params=_cparams(("arbitrary", "arbitrary")),
        name="fourier_lat",
    )(fb.reshape(B, R, GRID_W, B_WIDTH), fc, g, f2)
    return out.reshape(B, n, B_WIDTH)


def _ffn_kernel(T, S, final, x_ref, xp_ref, xn_ref, a_ref, ap_ref, an_ref, b_ref, bp_ref, bn_ref, c_ref, cp_ref,
                cn_ref, mod_ref, g_ref, wout_ref, wug_ref, cw_ref, cb_ref, wd_ref, gfin_ref,
                o_ref, mix_scr, hs_scr, u0_scr, u1_scr, act_scr):
    i = pl.program_id(0)
    H = BF16_ROWS
    for r0, refs in ((0, (ap_ref, bp_ref, cp_ref)), (H, (a_ref, b_ref, c_ref)), (T + H, (an_ref, bn_ref, cn_ref))):
        rows = slice(r0, r0 + refs[0].shape[0])
        mix_scr[rows, 0:A_WIDTH] = refs[0][...]
        mix_scr[rows, A_WIDTH:A_WIDTH + B_WIDTH] = refs[1][...].astype(BF16)
        mix_scr[rows, A_WIDTH + B_WIDTH:] = refs[2][...]
    mix = _dot(mix_scr[...], wout_ref[...])
    gt1 = mod_ref[:, 2 * D_MODEL:3 * D_MODEL]
    sh2 = mod_ref[:, 3 * D_MODEL:4 * D_MODEL]
    sc2 = mod_ref[:, 4 * D_MODEL:5 * D_MODEL]

    def ffn_input(x, m):
        x1 = x + gt1 * m
        return x1, (_rms(x1, g_ref[...], D_MODEL) * (1.0 + sc2) + sh2).astype(BF16)

    at_start = (i * T) % S == 0
    at_end = ((i + 1) * T) % S == 0
    hp = ffn_input(xp_ref[...], mix[0:H])[1]
    hn = ffn_input(xn_ref[...], mix[T + H:T + 2 * H])[1]
    x1, hm = ffn_input(x_ref[...], mix[H:T + H])
    o_ref[...] = x1
    hs_scr[0:H] = jnp.where(at_start, jnp.zeros_like(hp), hp)
    hs_scr[H:T + H] = hm
    hs_scr[T + H:T + 2 * H] = jnp.where(at_end, jnp.zeros_like(hn), hn)
    if S < T:
        pos = lax.broadcasted_iota(jnp.int32, (T, 1), 0) % S
        has_prev = pos != 0
        has_next = pos != S - 1

    n_chunks = D_FF // F_CHUNK
    cols = lambda j: (pl.multiple_of(j * F_CHUNK, F_CHUNK), pl.multiple_of(j * F_CHUNK + D_FF, LANES))

    def up_proj(j, u_ref):
        for b, col in enumerate(cols(j)):
            u_ref[b] = _dot(hs_scr[...], wug_ref[:, pl.ds(col, F_CHUNK)])

    def conv(u_ref, b, col):
        u = u_ref[b]
        rows = u.shape[0]
        up = pltpu.roll(u, 1, 0)[H:T + H]
        un = pltpu.roll(u, rows - 1, 0)[H:T + H]
        if S < T:
            up = jnp.where(has_prev, up, 0.0)
            un = jnp.where(has_next, un, 0.0)
        cw = cw_ref[:, pl.ds(col, F_CHUNK)]
        return (up * cw[0:1, :] + u[H:T + H] * cw[1:2, :] + un * cw[2:3, :]
                + cb_ref[:, pl.ds(col, F_CHUNK)])

    def gate(j, u_ref):
        a, g = [conv(u_ref, b, col) for b, col in enumerate(cols(j))]
        act_scr[:, pl.ds(cols(j)[0], F_CHUNK)] = (g * jax.nn.sigmoid(g) * a).astype(BF16)

    def pair(k, carry):
        up_proj(2 * k + 1, u1_scr)
        gate(2 * k, u0_scr)
        up_proj(2 * k + 2, u0_scr)
        gate(2 * k + 1, u1_scr)
        return carry

    assert n_chunks % 2 == 1
    up_proj(0, u0_scr)
    lax.fori_loop(0, n_chunks // 2, pair, 0, unroll=True)
    gate(n_chunks - 1, u0_scr)
    gt2 = mod_ref[:, 5 * D_MODEL:6 * D_MODEL]
    x2 = o_ref[...] + gt2 * _dot(act_scr[...], wd_ref[...])
    if final:
        x2 = _rms(x2, gfin_ref[...], D_MODEL)
    o_ref[...] = x2


def _ffn(latent, final, layer, x, oa, ob, oc, mod_l, g_ffn, wout, wug, conv_w, conv_b, wdown, g_final):
    n = x.shape[0]
    T = T_FFN
    S = DEC_SEQ if latent else SEQ
    nt = n // T
    tpb = (DEC_SEQ // T) if latent else nt
    base = 1 if latent else 0
    hb = T // BF16_ROWS
    nhb = n // BF16_ROWS
    once = dict(pipeline_mode=pl.Buffered(1))
    whole = lambda a: pl.BlockSpec(a.shape, lambda i: (0,) * a.ndim, **once)
    of_layer = lambda a: pl.BlockSpec((None,) + a.shape[1:], lambda i: (layer,) + (0,) * (a.ndim - 1), **once)
    with_halos = lambda w: [
        pl.BlockSpec((T, w), lambda i: (i, 0)),
        pl.BlockSpec((BF16_ROWS, w), lambda i: (jnp.maximum(i * hb - 1, 0), 0)),
        pl.BlockSpec((BF16_ROWS, w), lambda i: (jnp.minimum((i + 1) * hb, nhb - 1), 0))]
    return pl.pallas_call(
        functools.partial(_ffn_kernel, T, S, final),
        grid=(nt,),
        in_specs=[spec for a in (x, oa, ob, oc) for spec in with_halos(a.shape[1])] + [
            pl.BlockSpec((None, 1, 6 * D_MODEL), lambda i: (base + i // tpb, 0, 0)),
            whole(g_ffn), whole(wout),
            of_layer(wug), of_layer(conv_w), of_layer(conv_b), of_layer(wdown), whole(g_final),
        ],
        out_specs=pl.BlockSpec((T, D_MODEL), lambda i: (i, 0)),
        out_shape=jax.ShapeDtypeStruct((n, D_MODEL), F32),
        scratch_shapes=[pltpu.VMEM((T + 2 * BF16_ROWS, D_MODEL), BF16),
                        pltpu.VMEM((T + 2 * BF16_ROWS, D_MODEL), BF16),
                        pltpu.VMEM((2, T + 2 * BF16_ROWS, F_CHUNK), F32),
                        pltpu.VMEM((2, T + 2 * BF16_ROWS, F_CHUNK), F32),
                        pltpu.VMEM((T, D_FF), BF16)],
        compiler_params=_cparams(("arbitrary",)),
        name="ffn_lat" if latent else "ffn_ctx",
    )(x, x, x, oa, oa, oa, ob, ob, ob, oc, oc, oc, mod_l, g_ffn, wout, wug, conv_w, conv_b, wdown, g_final)


def _rope_tables(n_tok, dim, lane0, width, lanes=None):
    rows = n_tok // GRID_W
    r = np.repeat(np.arange(rows), GRID_W).astype(np.float64)
    col = np.tile(np.arange(GRID_W), rows).astype(np.float64)
    quarter = dim // 4
    inv = ROPE_BASE ** (-np.arange(quarter, dtype=np.float64) / quarter)
    ang_r = r[:, None] * inv
    ang_c = col[:, None] * inv
    cos = np.concatenate([np.cos(ang_r)] * 2 + [np.cos(ang_c)] * 2, axis=1)
    sin = np.concatenate([-np.sin(ang_r), np.sin(ang_r), -np.sin(ang_c), np.sin(ang_c)], axis=1)
    if lanes is not None:
        cos_b, sin_b = np.ones((n_tok, LANES)), np.zeros((n_tok, LANES))
        cos_b[:, lanes], sin_b[:, lanes] = cos, sin
        return jnp.asarray(cos_b, F32), jnp.asarray(sin_b, F32)
    reps = width // dim
    cos = np.concatenate([cos] * reps, axis=1)
    sin = np.concatenate([sin] * reps, axis=1)
    pad = ((0, 0), (lane0, LANES - lane0 - width))
    return jnp.asarray(np.pad(cos, pad, constant_values=1.0), F32), jnp.asarray(np.pad(sin, pad), F32)


def _place(a, lanes, width=LANES):
    src = np.full((width,), a.shape[-1], np.int32)
    src[np.asarray(lanes)] = np.arange(len(lanes))
    ext = jnp.concatenate([a, jnp.zeros(a.shape[:-1] + (1,), a.dtype)], axis=-1)
    return ext[..., src]


def _layer_weights(w_in, g_cq, w_uq, w_ukv):
    z = lambda r, c: jnp.zeros((r, c), F32)
    o = np.cumsum([0, A_WIDTH, A_KV_WIDTH, A_KV_WIDTH, B_WIDTH, C_Q_LORA, C_KV_LORA, C_ROPE])
    win_p = jnp.concatenate([
        w_in[:, o[0]:o[1]].reshape(D_MODEL, A_HEADS, HEAD_DIM)[:, A_HEAD_ORDER, :].reshape(D_MODEL, A_WIDTH),
        w_in[:, o[1]:o[4]],
        w_in[:, o[4]:o[5]], z(D_MODEL, CQ_PAD - C_Q_LORA),
        w_in[:, o[5]:o[6]],
        _place(w_in[:, o[6]:o[7]], C_ROPE_LANES) + _place(w_in[:, o[6]:o[7]], C_STATE_LANES),
    ], axis=1).astype(BF16)
    gcq_p = jnp.pad(g_cq, (0, CQ_PAD - C_Q_LORA)).reshape(1, CQ_PAD)
    hq = C_NOPE + C_ROPE
    wuq_h = w_uq.reshape(C_Q_LORA, C_HEADS, hq)
    wuq_p = _place(wuq_h, C_NOPE_LANES + C_ROPE_LANES)
    wuq_p = jnp.pad(wuq_p, ((0, CQ_PAD - C_Q_LORA), (0, 0), (0, 0)))
    wuq_p = wuq_p.reshape(CQ_PAD, C_HEADS * C_HEAD_PAD).astype(BF16)
    wukv_h = w_ukv.reshape(C_KV_LORA, C_HEADS, C_NOPE + C_V)
    wk = _place(wukv_h[:, :, :C_NOPE], C_NOPE_LANES)
    zv = jnp.zeros((C_KV_LORA, C_HEAD_PAD - C_V), F32)
    wv = [wukv_h[:, h, C_NOPE:] for h in range(C_HEADS)]
    wv = jnp.concatenate([jnp.concatenate([zv, w] if h % 2 else [w, zv], axis=1) for h, w in enumerate(wv)], axis=1)
    wukv_p = jnp.concatenate([wk.reshape(C_KV_LORA, CW), wv], axis=1).astype(BF16)
    return win_p, gcq_p, wuq_p, wukv_p


def kernel(x_prompt, x_sample, cache_win_k, cache_win_v, cache_mla_ckv, cache_mla_krope, c, c_ctx,
           w_ada, b_ada, g_mix, w_in, sink, g_cq, w_uq, g_ckv, w_ukv, w_out, g_ffn, w_ug, conv_w,
           conv_b, w_down, g_final):
    n_ctx = BATCH * SEQ
    n_lat = DEC_BATCH * DEC_SEQ
    xp = x_prompt.reshape(n_ctx, D_MODEL)
    xs = x_sample.reshape(n_lat, D_MODEL)

    cvecs = jnp.concatenate([c_ctx[None, :], c, jnp.zeros((8 - 1 - DEC_BATCH, D_MODEL), F32)], axis=0)
    mod = _modulation(cvecs, w_ada, b_ada)

    cos_a, sin_a = _rope_tables(DEC_SEQ, HEAD_DIM, 0, LANES)
    cos_c, sin_c = _rope_tables(DEC_SEQ, C_ROPE, 0, C_ROPE, lanes=C_ROPE_LANES)
    tables = (cos_a, sin_a, cos_c, sin_c)
    fc = _channel_dft()
    g_final2 = g_final.reshape(1, D_MODEL)
    wug_all = w_ug.astype(BF16)
    wdown_all = w_down.astype(BF16)
    conv_b_all = conv_b.reshape(DEPTH, 1, 2 * D_FF)

    new_k, new_v, new_ckv, new_kr = [], [], [], []
    for l in range(DEPTH):
        win_p, gcq_p, wuq_p, wukv_p = _layer_weights(w_in[l], g_cq[l], w_uq[l], w_ukv[l])
        mod_l = mod[l].reshape(8, 1, 6 * D_MODEL)
        g_mix_l = g_mix[l].reshape(1, D_MODEL)
        g_ckv_l = g_ckv[l].reshape(1, C_KV_LORA)
        g_ffn_l = g_ffn[l].reshape(1, D_MODEL)
        sink_l = sink[l].reshape(1, A_HEADS)
        wout_a = w_out[l, :A_WIDTH].reshape(A_HEADS, HEAD_DIM, D_MODEL)[jnp.array(A_HEAD_ORDER)]
        wout_l = jnp.concatenate([wout_a.reshape(A_WIDTH, D_MODEL), w_out[l, A_WIDTH:]], axis=0).astype(BF16)
        final = l == DEPTH - 1

        (qa, ka, va, fb, qc, kc, vc, ska, sva, sckv, skr) = _projection(
            False, xp, mod_l, g_mix_l, win_p, gcq_p, wuq_p, g_ckv_l, wukv_p, None)
        r3 = lambda a: a.reshape(BATCH, SEQ, a.shape[-1])
        oa = _attn_a_ctx(sink_l, r3(qa), ka, r3(va))
        ob = _fourier_ctx(r3(fb), fc)
        oc = _mla(r3(qc), r3(kc), r3(vc), None, None, SEQ, SEQ, group=CTX_GROUP)
        f2 = lambda a: a.reshape(n_ctx, a.shape[-1])
        xp = _ffn(False, final, l, xp, f2(oa), f2(ob), f2(oc), mod_l, g_ffn_l, wout_l, wug_all, conv_w, conv_b_all,
                  wdown_all, g_final2)
        new_k.append(ska.reshape(BATCH, SEQ, A_KV_HEADS, HEAD_DIM))
        new_v.append(sva.reshape(BATCH, SEQ, A_KV_HEADS, HEAD_DIM))
        new_ckv.append(sckv.reshape(BATCH, SEQ, C_KV_LORA))
        new_kr.append(skr.reshape(BATCH, SEQ, C_ROPE))

        (qa, ka, va, fb, qc, kc, vc) = _projection(
            True, xs, mod_l, g_mix_l, win_p, gcq_p, wuq_p, g_ckv_l, wukv_p, tables)
        r3 = lambda a: a.reshape(DEC_BATCH, DEC_SEQ, a.shape[-1])
        kwin = cache_win_k[:, l].reshape(DEC_BATCH * PAST_LEN, A_KV_WIDTH).T.astype(BF16)
        vwin = cache_win_v[:, l].reshape(DEC_BATCH, PAST_LEN, A_KV_WIDTH).astype(BF16)
        oa = _attn_a_lat(sink_l, r3(qa), ka, r3(va), kwin, vwin)
        ob = _fourier_lat(r3(fb), fc)
        kr_pad = _place(cache_mla_krope[:, l], C_ROPE_LANES)
        kc_ctx, vc_ctx = _kvcache(cache_mla_ckv[:, l], kr_pad, wukv_p)
        oc = _mla(r3(qc), r3(kc), r3(vc), kc_ctx, vc_ctx, TQ_MLA, TK_MLA)
        f2 = lambda a: a.reshape(n_lat, a.shape[-1])
        xs = _ffn(True, final, l, xs, f2(oa), f2(ob), f2(oc), mod_l, g_ffn_l, wout_l, wug_all, conv_w, conv_b_all,
                  wdown_all, g_final2)

    y_prompt = xp.reshape(BATCH, SEQ, D_MODEL)
    y_sample = xs.reshape(DEC_BATCH, DEC_SEQ, D_MODEL)
    return (y_prompt, y_sample, jnp.stack(new_k, axis=1), jnp.stack(new_v, axis=1),
            jnp.stack(new_ckv, axis=1), jnp.stack(new_kr, axis=1))
```

```python
import functools

import numpy as np
import jax
import jax.numpy as jnp
from jax import lax
from jax.experimental import pallas as pl
from jax.experimental.pallas import tpu as pltpu

F32 = jnp.float32
BF16 = jnp.bfloat16

D_MODEL = 1024
BATCH = 16
SEQ = 256
DEPTH = 2
DEC_BATCH = 4
DEC_SEQ = 4096
PAST_LEN = 256
GRID_W = 64
HEAD_DIM = 64
A_HEADS = 8
A_KV_HEADS = 2
A_GROUP = 4
A_WIDTH = 512
A_KV_WIDTH = 128
WINDOW = 128
BLOCK = 128
B_WIDTH = 256
B_GROUP_DIM = 64
B_GROUPS = 4
C_HEADS = 4
C_NOPE = 64
C_ROPE = 32
C_V = 64
C_Q_LORA = 192
C_KV_LORA = 128
C_WIDTH = 256
D_FF = 2816
ROPE_BASE = 10000.0
EPS = 1e-6
NEG = -1e30

LANES = 128
BF16_ROWS = 16
C_HEAD_PAD = 128
CW = C_HEADS * C_HEAD_PAD
CQ_PAD = 256
VMEM_LIMIT = 56 * 1024 * 1024

_QA0, _KA0, _VA0, _FB0, _CQ0, _CKV0, _KR0, _IN_PAD = 0, 512, 640, 768, 1024, 1280, 1408, 1536
_q8 = C_ROPE // 4
C_NOPE_LANES = list(range(0, 48)) + list(range(64, 80))
C_ROPE_LANES = (list(range(48, 48 + _q8)) + list(range(112, 112 + _q8))
                + list(range(48 + _q8, 48 + 2 * _q8)) + list(range(112 + _q8, 112 + 2 * _q8)))
C_STATE_LANES = list(range(80, 80 + C_ROPE))

T_PROJ = 1024
PROJ_SPLIT = 2
T_FFN = 1024
F_CHUNK = 256
TQ_MLA = 1024
TK_MLA = 2048
A_SUB = 8
CTX_GROUP = 4
A_HEAD_ORDER = [h for j in range(A_GROUP) for h in (j, A_GROUP + j)]
LOG2_E = float(np.log2(np.e))
FOURIER_COLS = 32


def _cparams(sem):
    return pltpu.CompilerParams(dimension_semantics=sem, vmem_limit_bytes=VMEM_LIMIT)


def _dot(a, b):
    return jnp.dot(a, b, preferred_element_type=F32)


def _dot_nt(a, b):
    return lax.dot_general(a, b, (((1,), (1,)), ((), ())), preferred_element_type=F32)


def _rms(x, g, n):
    ms = jnp.sum(x * x, axis=-1, keepdims=True) * (1.0 / n)
    return x * lax.rsqrt(ms + EPS) * g


def _split_bf16(a):
    hi = a.astype(BF16)
    return hi, (a - hi.astype(F32)).astype(BF16)


def _mod_kernel(c_ref, w_ref, b_ref, o_ref):
    cv = c_ref[...]
    s = cv * jax.nn.sigmoid(cv)
    s_hi, s_lo = _split_bf16(s)
    w_hi, w_lo = _split_bf16(w_ref[...])
    o_ref[...] = _dot(s_hi, w_hi) + (_dot(s_hi, w_lo) + _dot(s_lo, w_hi)) + b_ref[...]


def _modulation(cvecs, w_ada, b_ada):
    nj = 3
    wb = 6 * D_MODEL // nj
    return pl.pallas_call(
        _mod_kernel,
        grid=(DEPTH, nj),
        in_specs=[
            pl.BlockSpec((8, D_MODEL), lambda l, j: (0, 0)),
            pl.BlockSpec((None, D_MODEL, wb), lambda l, j: (l, 0, j)),
            pl.BlockSpec((None, 1, wb), lambda l, j: (l, 0, j)),
        ],
        out_specs=pl.BlockSpec((None, 8, wb), lambda l, j: (l, 0, j)),
        out_shape=jax.ShapeDtypeStruct((DEPTH, 8, 6 * D_MODEL), F32),
        compiler_params=_cparams(("arbitrary", "arbitrary")),
        name="modulation",
    )(cvecs, w_ada, b_ada.reshape(DEPTH, 1, 6 * D_MODEL))


def _rope_block(x, cos, sin, half):
    if 2 * half == LANES:
        partner = pltpu.roll(x, half, 1)
    else:
        lane = lax.broadcasted_iota(jnp.int32, x.shape, 1)
        first = (lane % (2 * half)) < half
        partner = jnp.where(first, pltpu.roll(x, LANES - half, 1), pltpu.roll(x, half, 1))
    return x * cos + partner * sin


def _rope(x, cos, sin, half):
    blocks = [_rope_block(x[:, j:j + LANES], cos, sin, half) for j in range(0, x.shape[1], LANES)]
    return blocks[0] if len(blocks) == 1 else jnp.concatenate(blocks, axis=1)


def _proj_kernel(latent, x_ref, mod_ref, g_ref, win_ref, gcq_ref, wuq_ref, gckv_ref, wukv_ref, *rest):
    n_tab = 4 if latent else 0
    tables, outs = rest[:n_tab], rest[n_tab:]
    split = PROJ_SPLIT if latent else 1
    rows = x_ref.shape[0] // split
    for r in range(split):
        rs = pl.ds(r * rows, rows)
        outs_r = [o.at[:, rs] if i == 1 else o.at[rs] for i, o in enumerate(outs)]
        _proj_rows(latent, x_ref.at[rs], mod_ref, g_ref, win_ref, gcq_ref, wuq_ref, gckv_ref, wukv_ref,
                   *[t.at[rs] for t in tables], *outs_r)


def _proj_rows(latent, x_ref, mod_ref, g_ref, win_ref, gcq_ref, wuq_ref, gckv_ref, wukv_ref, *rest):
    if latent:
        cosa_ref, sina_ref, cosc_ref, sinc_ref = rest[:4]
        qa_ref, ka_ref, va_ref, fb_ref, qc_ref, kc_ref, vc_ref = rest[4:]
    else:
        qa_ref, ka_ref, va_ref, fb_ref, qc_ref, kc_ref, vc_ref, ska_ref, sva_ref, sckv_ref, skr_ref = rest
    x = x_ref[...]
    y = _rms(x, g_ref[...], D_MODEL)
    sh1 = mod_ref[:, 0:D_MODEL]
    sc1 = mod_ref[:, D_MODEL:2 * D_MODEL]
    h = (y * (1.0 + sc1) + sh1).astype(BF16)

    qa = _dot(h, win_ref[:, _QA0:_KA0])
    kva = _dot(h, win_ref[:, _KA0:_FB0])
    ka, va = kva[:, :A_KV_WIDTH], kva[:, A_KV_WIDTH:]
    fb = _dot(h, win_ref[:, _FB0:_CQ0])
    cq = _dot(h, win_ref[:, _CQ0:_CKV0])
    ckr = _dot(h, win_ref[:, _CKV0:_IN_PAD])
    ckv, kr = ckr[:, :C_KV_LORA], ckr[:, C_KV_LORA:]

    if not latent:
        ska_ref[...] = ka
        sva_ref[...] = va
        skr_ref[...] = kr[:, C_STATE_LANES[0]:C_STATE_LANES[-1] + 1]
    else:
        qa = _rope(qa, cosa_ref[...], sina_ref[...], HEAD_DIM // 4)
        ka = _rope(ka, cosa_ref[...], sina_ref[...], HEAD_DIM // 4)
        kr = _rope(kr, cosc_ref[...], sinc_ref[...], LANES // 2)
    qa_ref[...] = (qa * (LOG2_E * HEAD_DIM ** -0.5)).astype(BF16)
    ka_ref[...] = ka.T.astype(BF16)
    va_ref[...] = va.astype(BF16)
    fb_ref[...] = fb.astype(fb_ref.dtype)

    cqn = _rms(cq, gcq_ref[...], C_Q_LORA).astype(BF16)
    qc = _dot(cqn, wuq_ref[...])
    if latent:
        qc = _rope(qc, cosc_ref[...], sinc_ref[...], LANES // 2)
    qc_ref[...] = (qc * (LOG2_E * (C_NOPE + C_ROPE) ** -0.5)).astype(BF16)

    ckvn = _rms(ckv, gckv_ref[...], C_KV_LORA)
    if not latent:
        sckv_ref[...] = ckvn
    _store_kv(_dot(ckvn.astype(BF16), wukv_ref[...]), kr, kc_ref, vc_ref)


def _store_kv(kv, kr, kc_ref, vc_ref):
    kc_ref[...] = (kv[:, :CW] + jnp.concatenate([kr] * C_HEADS, axis=1)).astype(BF16)
    lane = lax.broadcasted_iota(jnp.int32, (1, CW), 1)
    upper = lane % C_HEAD_PAD >= C_V
    odd = (lane // C_HEAD_PAD) % 2 == 1
    vc_ref[...] = (kv[:, CW:] + jnp.where(upper != odd, 1.0, 0.0).astype(F32)).astype(BF16)


def _projection(latent, x, mod_l, g_mix, win_p, gcq_p, wuq_p, gckv, wukv_p, tables):
    n = x.shape[0]
    T = T_PROJ
    nt = n // T
    tpb = (DEC_SEQ // T) if latent else nt
    base = 1 if latent else 0
    tok = lambda w: pl.BlockSpec((T, w), lambda i: (i, 0))
    full = lambda a: pl.BlockSpec(a.shape, lambda i: (0,) * a.ndim)
    in_specs = [
        tok(D_MODEL),
        pl.BlockSpec((None, 1, 6 * D_MODEL), lambda i: (base + i // tpb, 0, 0)),
        full(g_mix), full(win_p), full(gcq_p), full(wuq_p), full(gckv), full(wukv_p),
    ]
    args = [x, mod_l, g_mix, win_p, gcq_p, wuq_p, gckv, wukv_p]
    if latent:
        in_specs += [pl.BlockSpec((T, LANES), lambda i: (i % tpb, 0))] * 4
        args += list(tables)
    widths = [A_WIDTH, A_KV_WIDTH, A_KV_WIDTH, B_WIDTH, CW, CW, CW]
    out_specs = [tok(w) for w in widths]
    out_shape = [jax.ShapeDtypeStruct((n, w), BF16) for w in widths]
    out_specs[1] = pl.BlockSpec((A_KV_WIDTH, T), lambda i: (0, i))
    out_shape[1] = jax.ShapeDtypeStruct((A_KV_WIDTH, n), BF16)
    if latent:
        out_shape[3] = jax.ShapeDtypeStruct((n, B_WIDTH), F32)
    if not latent:
        sw = [A_KV_WIDTH, A_KV_WIDTH, C_KV_LORA, C_ROPE]
        out_specs += [tok(w) for w in sw]
        out_shape += [jax.ShapeDtypeStruct((n, w), F32) for w in sw]
    return pl.pallas_call(
        functools.partial(_proj_kernel, latent),
        grid=(nt,),
        in_specs=in_specs,
        out_specs=out_specs,
        out_shape=out_shape,
        compiler_params=_cparams(("arbitrary",)),
        name="proj_lat" if latent else "proj_ctx",
    )(*args)


def _kvcache_kernel(ckv_ref, kr_ref, wukv_ref, kc_ref, vc_ref):
    _store_kv(_dot(ckv_ref[...].astype(BF16), wukv_ref[...]), kr_ref[...], kc_ref, vc_ref)


def _kvcache(ckv, kr_pad, wukv_p):
    B, P, _ = ckv.shape
    return pl.pallas_call(
        _kvcache_kernel,
        grid=(B,),
        in_specs=[
            pl.BlockSpec((None, P, C_KV_LORA), lambda b: (b, 0, 0)),
            pl.BlockSpec((None, P, LANES), lambda b: (b, 0, 0)),
            pl.BlockSpec(wukv_p.shape, lambda b: (0, 0)),
        ],
        out_specs=[pl.BlockSpec((None, P, CW), lambda b: (b, 0, 0))] * 2,
        out_shape=[jax.ShapeDtypeStruct((B, P, CW), BF16)] * 2,
        compiler_params=_cparams(("arbitrary",)),
        name="kvcache",
    )(ckv, kr_pad, wukv_p)


def _gqa_group(g, q_blocks, keys, masks, values, sink_ref):
    tq = q_blocks[0].shape[0]
    lane = lax.broadcasted_iota(jnp.int32, (tq, LANES), 1)
    mine = (lane >= HEAD_DIM) if g else (lane < HEAD_DIM)
    qs = jnp.concatenate([jnp.where(mine, qb, jnp.zeros_like(qb)) for qb in q_blocks], axis=0)
    parts = []
    for k, mask in zip(keys, masks):
        s = _dot(qs, k)
        parts.append(s if mask is None else jnp.where(mask, s, NEG))
    s = parts[0] if len(parts) == 1 else jnp.concatenate(parts, axis=1)
    rowblk = lax.broadcasted_iota(jnp.int32, (A_GROUP * tq, 1), 0) // tq
    sk = jnp.full((A_GROUP * tq, 1), sink_ref[0, g * A_GROUP] * LOG2_E, F32)
    for j in range(1, A_GROUP):
        sk = jnp.where(rowblk == j, sink_ref[0, g * A_GROUP + j] * LOG2_E, sk)
    m = jnp.maximum(jnp.max(s, axis=-1, keepdims=True), sk)
    p = jnp.exp2(s - m)
    l = jnp.sum(p, axis=-1, keepdims=True) + jnp.exp2(sk - m)
    v = values[0] if len(values) == 1 else jnp.concatenate(values, axis=0)
    return _dot(p.astype(BF16), v) * (1.0 / l)


def _gqa_store(o_ref, rows, tq, o0, o1):
    low = lax.broadcasted_iota(jnp.int32, (tq, LANES), 1) < HEAD_DIM
    for j in range(A_GROUP):
        blk = jnp.where(low, o0[j * tq:(j + 1) * tq], o1[j * tq:(j + 1) * tq])
        o_ref[rows, j * LANES:(j + 1) * LANES] = blk.astype(BF16)


def _attn_a_ctx_kernel(sink_ref, q_ref, k_ref, v_ref, o_ref):
    n = q_ref.shape[1]
    for e in range(CTX_GROUP):
        q_blocks = [q_ref[e, :, j * LANES:(j + 1) * LANES] for j in range(A_GROUP)]
        k = k_ref[:, e * n:(e + 1) * n]
        outs = [_gqa_group(g, q_blocks, [k], [None], [v_ref[e]], sink_ref) for g in range(A_KV_HEADS)]
        _gqa_store(o_ref.at[e], slice(None), n, *outs)


def _attn_a_ctx(sink, q, k, v):
    B, n, _ = q.shape
    G = CTX_GROUP
    blk = lambda w: pl.BlockSpec((G, n, w), lambda b: (b, 0, 0))
    return pl.pallas_call(
        _attn_a_ctx_kernel,
        grid=(B // G,),
        in_specs=[pl.BlockSpec(memory_space=pltpu.SMEM), blk(A_WIDTH),
                  pl.BlockSpec((A_KV_WIDTH, G * n), lambda b: (0, b)), blk(A_KV_WIDTH)],
        out_specs=blk(A_WIDTH),
        out_shape=jax.ShapeDtypeStruct((B, n, A_WIDTH), BF16),
        compiler_params=_cparams(("arbitrary",)),
        name="attn_a_ctx",
    )(sink, q, k, v)


def _attn_a_lat_kernel(sink_ref, q_ref, k_ref, v_ref, kc_ref, vc_ref, o_ref):
    n = k_ref.shape[1]
    band = 3 * BLOCK
    kcx = kc_ref[...]
    vcx = vc_ref[...]

    def body(sb, carry):
        blk = pl.program_id(1) * A_SUB + sb
        rows = pl.ds(pl.multiple_of(sb * BLOCK, BLOCK), BLOCK)
        start = pl.multiple_of(jnp.clip((blk - 1) * BLOCK, 0, n - band), BLOCK)
        kb = k_ref[:, pl.ds(start, band)]
        vb = v_ref[pl.ds(start, band), :]
        qpos = blk * BLOCK + lax.broadcasted_iota(jnp.int32, (BLOCK, band), 0)
        kpos = start + lax.broadcasted_iota(jnp.int32, (BLOCK, band), 1)
        mask = jnp.concatenate([jnp.abs(qpos - kpos) <= WINDOW] * A_GROUP, axis=0)
        q_blocks = [q_ref[rows, j * LANES:(j + 1) * LANES] for j in range(A_GROUP)]
        outs = [_gqa_group(g, q_blocks, [kb, kcx], [mask, None], [vb, vcx], sink_ref)
                for g in range(A_KV_HEADS)]
        _gqa_store(o_ref, rows, BLOCK, *outs)
        return carry

    lax.fori_loop(0, A_SUB, body, 0, unroll=True)


def _attn_a_lat(sink, q, k, v, k_ctx, v_ctx):
    B, n, _ = q.shape
    P = v_ctx.shape[1]
    tq = A_SUB * BLOCK
    seq = lambda w, m: pl.BlockSpec((None, m, w), lambda b, i: (b, 0, 0))
    seq_t = lambda w, m: pl.BlockSpec((w, m), lambda b, i: (0, b))
    return pl.pallas_call(
        _attn_a_lat_kernel,
        grid=(B, n // tq),
        in_specs=[pl.BlockSpec(memory_space=pltpu.SMEM),
                  pl.BlockSpec((None, tq, A_WIDTH), lambda b, i: (b, i, 0)),
                  seq_t(A_KV_WIDTH, n), seq(A_KV_WIDTH, n), seq_t(A_KV_WIDTH, P), seq(A_KV_WIDTH, P)],
        out_specs=pl.BlockSpec((None, tq, A_WIDTH), lambda b, i: (b, i, 0)),
        out_shape=jax.ShapeDtypeStruct((B, n, A_WIDTH), BF16),
        compiler_params=_cparams(("arbitrary", "arbitrary")),
        name="attn_a_lat",
    )(sink, q, k, v, k_ctx, v_ctx)


def _mla_step(q_ref, k, v, m_scr, acc_scr, first):
    reps = k.shape[0] // LANES
    for h in range(C_HEADS):
        hs = slice(h * C_HEAD_PAD, (h + 1) * C_HEAD_PAD)
        s = _dot_nt(q_ref[:, hs], k[:, hs])
        m_new = jnp.broadcast_to(jnp.max(s, axis=-1, keepdims=True), (s.shape[0], LANES))
        if not first:
            m_old = m_scr[h]
            m_new = jnp.maximum(m_old, m_new)
        p = jnp.exp2(s - jnp.concatenate([m_new] * reps, axis=1))
        m_scr[h] = m_new
        pv = _dot(p.astype(BF16), v[:, hs])
        acc_scr[h] = pv if first else acc_scr[h] * jnp.exp2(m_old - m_new) + pv


def _mla_tile(n_chunks, tk, q_ref, k_ref, v_ref, kc_ref, vc_ref, o_ref, m_scr, acc_scr):
    tq = q_ref.shape[0]

    def body(c, carry):
        ks = pl.multiple_of(c * tk, tk)
        _mla_step(q_ref, k_ref[pl.ds(ks, tk), :], v_ref[pl.ds(ks, tk), :], m_scr, acc_scr, False)
        return carry

    if kc_ref is not None:
        _mla_step(q_ref, kc_ref[...], vc_ref[...], m_scr, acc_scr, True)
        lax.fori_loop(0, n_chunks, body, 0)
    else:
        _mla_step(q_ref, k_ref[0:tk, :], v_ref[0:tk, :], m_scr, acc_scr, True)
        lax.fori_loop(1, n_chunks, body, 0)

    low = lax.broadcasted_iota(jnp.int32, (tq, LANES), 1) < C_V
    for j in range(C_HEADS // 2):
        even, odd = acc_scr[2 * j], acc_scr[2 * j + 1]
        sums = pltpu.roll(jnp.where(low, odd, even), C_V, 1)
        o_ref[:, j * LANES:(j + 1) * LANES] = (jnp.where(low, even, odd) / sums).astype(BF16)


def _mla_kernel(n_chunks, tk, has_ctx, group, q_ref, k_ref, v_ref, *rest):
    if has_ctx:
        kc_ref, vc_ref, o_ref, m_scr, acc_scr = rest
    else:
        (o_ref, m_scr, acc_scr), kc_ref, vc_ref = rest, None, None
    if group is None:
        _mla_tile(n_chunks, tk, q_ref, k_ref, v_ref, kc_ref, vc_ref, o_ref, m_scr, acc_scr)
    else:
        for e in range(group):
            _mla_tile(n_chunks, tk, q_ref.at[e], k_ref.at[e], v_ref.at[e], None, None, o_ref.at[e], m_scr, acc_scr)


def _mla(q, k, v, k_ctx, v_ctx, tq, tk, group=None):
    B, n, _ = q.shape
    nk = k.shape[1]
    has_ctx = k_ctx is not None
    lead = None if group is None else group
    nb = B if group is None else B // group
    seq = lambda m: pl.BlockSpec((lead, m, CW), lambda b, i: (b, 0, 0))
    in_specs = [pl.BlockSpec((lead, tq, CW), lambda b, i: (b, i, 0)), seq(nk), seq(nk)]
    args = [q, k, v]
    if has_ctx:
        in_specs += [seq(k_ctx.shape[1])] * 2
        args += [k_ctx, v_ctx]
    return pl.pallas_call(
        functools.partial(_mla_kernel, nk // tk, tk, has_ctx, group),
        grid=(nb, n // tq),
        in_specs=in_specs,
        out_specs=pl.BlockSpec((lead, tq, C_WIDTH), lambda b, i: (b, i, 0)),
        out_shape=jax.ShapeDtypeStruct((B, n, C_WIDTH), BF16),
        scratch_shapes=[pltpu.VMEM((C_HEADS, tq, LANES), F32), pltpu.VMEM((C_HEADS, tq, LANES), F32)],
        compiler_params=_cparams(("arbitrary", "arbitrary")),
        name="mla_lat" if has_ctx else "mla_ctx",
    )(*args)


def _dft_tables(n):
    j = np.arange(n, dtype=np.int64)
    ang = ((j[:, None] * j[None, :]) % n) * (2.0 * np.pi / n)
    return np.cos(ang), np.sin(ang)


def _channel_dft():
    c, s = _dft_tables(B_GROUP_DIM)
    eye = np.eye(B_GROUPS)
    return _table_bf16(np.concatenate([np.kron(eye, c), -np.kron(eye, s)], axis=1))


def _table_bf16(t):
    return jnp.asarray(t, F32).astype(BF16)


def _fourier_ctx_kernel(scale, x_ref, fc_ref, fn_ref, o_ref):
    for e in range(CTX_GROUP):
        u = _dot(x_ref[e], fc_ref[...])
        ucat = jnp.concatenate([u[:, :B_WIDTH], u[:, B_WIDTH:]], axis=0).astype(BF16)
        o_ref[e] = (_dot(fn_ref[...], ucat) * scale).astype(BF16)


def _fourier_ctx(fb, fc):
    B, n, _ = fb.shape
    c, s = _dft_tables(n)
    fn = _table_bf16(np.concatenate([c, s], axis=1))
    scale = float((n * B_GROUP_DIM) ** -0.5)
    return pl.pallas_call(
        functools.partial(_fourier_ctx_kernel, scale),
        grid=(B // CTX_GROUP,),
        in_specs=[pl.BlockSpec((CTX_GROUP, n, B_WIDTH), lambda b: (b, 0, 0)),
                  pl.BlockSpec(fc.shape, lambda b: (0, 0)),
                  pl.BlockSpec(fn.shape, lambda b: (0, 0))],
        out_specs=pl.BlockSpec((CTX_GROUP, n, B_WIDTH), lambda b: (b, 0, 0)),
        out_shape=jax.ShapeDtypeStruct((B, n, B_WIDTH), BF16),
        compiler_params=_cparams(("arbitrary",)),
        name="fourier_ctx",
    )(fb, fc, fn)


def _fourier_lat_kernel(scale, x_ref, fc_ref, g_ref, f_ref, o_ref, zr_scr, zi_scr):
    j = pl.program_id(1)
    nc = pl.num_programs(1) // 2

    @pl.when(j < nc)
    def _():
        for cc in range(FOURIER_COLS):
            xc = x_ref[:, cc, :].astype(BF16)
            u = _dot(xc, fc_ref[...]).astype(BF16)
            p = _dot(g_ref[cc], u)
            R = p.shape[0] // 2
            zshape = (R // FOURIER_COLS, FOURIER_COLS, B_WIDTH)
            zr_scr[j * FOURIER_COLS + cc] = (p[:R, :B_WIDTH] - p[R:, B_WIDTH:]).reshape(zshape)
            zi_scr[j * FOURIER_COLS + cc] = (p[:R, B_WIDTH:] + p[R:, :B_WIDTH]).reshape(zshape)

    @pl.when(j >= nc)
    def _():
        for kk in range(FOURIER_COLS):
            z = jnp.concatenate([zr_scr[:, j - nc, kk, :], zi_scr[:, j - nc, kk, :]], axis=0).astype(BF16)
            o_ref[:, kk, :] = _dot(f_ref[...], z) * scale


def _fourier_lat(fb, fc):
    B, n, _ = fb.shape
    R = n // GRID_W
    W = GRID_W * B_WIDTH
    kr = np.arange(R, dtype=np.int64)
    pos = GRID_W * np.arange(R, dtype=np.int64)[None, None, :] + np.arange(GRID_W, dtype=np.int64)[:, None, None]
    ang = ((kr[None, :, None] * pos) % n) * (2.0 * np.pi / n)
    g = _table_bf16(np.concatenate([np.cos(ang), -np.sin(ang)], axis=1))
    c64, s64 = _dft_tables(GRID_W)
    f2 = _table_bf16(np.concatenate([c64, s64], axis=1))
    assert R == GRID_W
    nc = GRID_W // FOURIER_COLS
    scale = float((n * B_GROUP_DIM) ** -0.5)
    blk = (None, GRID_W, FOURIER_COLS, B_WIDTH)
    zscr = pltpu.VMEM((GRID_W, R // FOURIER_COLS, FOURIER_COLS, B_WIDTH), F32)
    out = pl.pallas_call(
        functools.partial(_fourier_lat_kernel, scale),
        grid=(B, 2 * nc),
        in_specs=[pl.BlockSpec(blk, lambda b, j: (b, 0, jnp.minimum(j, nc - 1), 0)),
                  pl.BlockSpec(fc.shape, lambda b, j: (0, 0)),
                  pl.BlockSpec((FOURIER_COLS, 2 * R, R), lambda b, j: (jnp.minimum(j, nc - 1), 0, 0)),
                  pl.BlockSpec(f2.shape, lambda b, j: (0, 0))],
        out_specs=pl.BlockSpec(blk, lambda b, j: (b, 0, jnp.maximum(j - nc, 0), 0)),
        out_shape=jax.ShapeDtypeStruct((B, GRID_W, R, B_WIDTH), F32),
        scratch_shapes=[zscr, zscr],
        compiler_params=_cparams(("arbitrary", "arbitrary")),
        name="fourier_lat",
    )(fb.reshape(B, R, GRID_W, B_WIDTH), fc, g, f2)
    return out.reshape(B, n, B_WIDTH)


def _ffn_kernel(T, S, final, x_ref, xp_ref, xn_ref, a_ref, ap_ref, an_ref, b_ref, bp_ref, bn_ref, c_ref, cp_ref,
                cn_ref, mod_ref, g_ref, wout_ref, wug_ref, cw_ref, cb_ref, wd_ref, gfin_ref,
                o_ref, mix_scr, hs_scr, u0_scr, u1_scr, act_scr):
    i = pl.program_id(0)
    H = BF16_ROWS
    for r0, refs in ((0, (ap_ref, bp_ref, cp_ref)), (H, (a_ref, b_ref, c_ref)), (T + H, (an_ref, bn_ref, cn_ref))):
        rows = slice(r0, r0 + refs[0].shape[0])
        mix_scr[rows, 0:A_WIDTH] = refs[0][...]
        mix_scr[rows, A_WIDTH:A_WIDTH + B_WIDTH] = refs[1][...].astype(BF16)
        mix_scr[rows, A_WIDTH + B_WIDTH:] = refs[2][...]
    mix = _dot(mix_scr[...], wout_ref[...])
    gt1 = mod_ref[:, 2 * D_MODEL:3 * D_MODEL]
    sh2 = mod_ref[:, 3 * D_MODEL:4 * D_MODEL]
    sc2 = mod_ref[:, 4 * D_MODEL:5 * D_MODEL]

    def ffn_input(x, m):
        x1 = x + gt1 * m
        return x1, (_rms(x1, g_ref[...], D_MODEL) * (1.0 + sc2) + sh2).astype(BF16)

    at_start = (i * T) % S == 0
    at_end = ((i + 1) * T) % S == 0
    hp = ffn_input(xp_ref[...], mix[0:H])[1]
    hn = ffn_input(xn_ref[...], mix[T + H:T + 2 * H])[1]
    x1, hm = ffn_input(x_ref[...], mix[H:T + H])
    o_ref[...] = x1
    hs_scr[0:H] = jnp.where(at_start, jnp.zeros_like(hp), hp)
    hs_scr[H:T + H] = hm
    hs_scr[T + H:T + 2 * H] = jnp.where(at_end, jnp.zeros_like(hn), hn)
    if S < T:
        pos = lax.broadcasted_iota(jnp.int32, (T, 1), 0) % S
        has_prev = pos != 0
        has_next = pos != S - 1

    n_chunks = D_FF // F_CHUNK
    cols = lambda j: (pl.multiple_of(j * F_CHUNK, F_CHUNK), pl.multiple_of(j * F_CHUNK + D_FF, LANES))

    def up_proj(j, u_ref):
        for b, col in enumerate(cols(j)):
            u_ref[b] = _dot(hs_scr[...], wug_ref[:, pl.ds(col, F_CHUNK)])

    def conv(u_ref, b, col):
        u = u_ref[b]
        rows = u.shape[0]
        up = pltpu.roll(u, 1, 0)[H:T + H]
        un = pltpu.roll(u, rows - 1, 0)[H:T + H]
        if S < T:
            up = jnp.where(has_prev, up, 0.0)
            un = jnp.where(has_next, un, 0.0)
        cw = cw_ref[:, pl.ds(col, F_CHUNK)]
        return (up * cw[0:1, :] + u[H:T + H] * cw[1:2, :] + un * cw[2:3, :]
                + cb_ref[:, pl.ds(col, F_CHUNK)])

    def gate(j, u_ref):
        a, g = [conv(u_ref, b, col) for b, col in enumerate(cols(j))]
        act_scr[:, pl.ds(cols(j)[0], F_CHUNK)] = (g * jax.nn.sigmoid(g) * a).astype(BF16)

    def pair(k, carry):
        up_proj(2 * k + 1, u1_scr)
        gate(2 * k, u0_scr)
        up_proj(2 * k + 2, u0_scr)
        gate(2 * k + 1, u1_scr)
        return carry

    assert n_chunks % 2 == 1
    up_proj(0, u0_scr)
    lax.fori_loop(0, n_chunks // 2, pair, 0, unroll=True)
    gate(n_chunks - 1, u0_scr)
    gt2 = mod_ref[:, 5 * D_MODEL:6 * D_MODEL]
    x2 = o_ref[...] + gt2 * _dot(act_scr[...], wd_ref[...])
    if final:
        x2 = _rms(x2, gfin_ref[...], D_MODEL)
    o_ref[...] = x2


def _ffn(latent, final, layer, x, oa, ob, oc, mod_l, g_ffn, wout, wug, conv_w, conv_b, wdown, g_final):
    n = x.shape[0]
    T = T_FFN
    S = DEC_SEQ if latent else SEQ
    nt = n // T
    tpb = (DEC_SEQ // T) if latent else nt
    base = 1 if latent else 0
    hb = T // BF16_ROWS
    nhb = n // BF16_ROWS
    once = dict(pipeline_mode=pl.Buffered(1))
    whole = lambda a: pl.BlockSpec(a.shape, lambda i: (0,) * a.ndim, **once)
    of_layer = lambda a: pl.BlockSpec((None,) + a.shape[1:], lambda i: (layer,) + (0,) * (a.ndim - 1), **once)
    with_halos = lambda w: [
        pl.BlockSpec((T, w), lambda i: (i, 0)),
        pl.BlockSpec((BF16_ROWS, w), lambda i: (jnp.maximum(i * hb - 1, 0), 0)),
        pl.BlockSpec((BF16_ROWS, w), lambda i: (jnp.minimum((i + 1) * hb, nhb - 1), 0))]
    return pl.pallas_call(
        functools.partial(_ffn_kernel, T, S, final),
        grid=(nt,),
        in_specs=[spec for a in (x, oa, ob, oc) for spec in with_halos(a.shape[1])] + [
            pl.BlockSpec((None, 1, 6 * D_MODEL), lambda i: (base + i // tpb, 0, 0)),
            whole(g_ffn), whole(wout),
            of_layer(wug), of_layer(conv_w), of_layer(conv_b), of_layer(wdown), whole(g_final),
        ],
        out_specs=pl.BlockSpec((T, D_MODEL), lambda i: (i, 0)),
        out_shape=jax.ShapeDtypeStruct((n, D_MODEL), F32),
        scratch_shapes=[pltpu.VMEM((T + 2 * BF16_ROWS, D_MODEL), BF16),
                        pltpu.VMEM((T + 2 * BF16_ROWS, D_MODEL), BF16),
                        pltpu.VMEM((2, T + 2 * BF16_ROWS, F_CHUNK), F32),
                        pltpu.VMEM((2, T + 2 * BF16_ROWS, F_CHUNK), F32),
                        pltpu.VMEM((T, D_FF), BF16)],
        compiler_params=_cparams(("arbitrary",)),
        name="ffn_lat" if latent else "ffn_ctx",
    )(x, x, x, oa, oa, oa, ob, ob, ob, oc, oc, oc, mod_l, g_ffn, wout, wug, conv_w, conv_b, wdown, g_final)


def _rope_tables(n_tok, dim, lane0, width, lanes=None):
    rows = n_tok // GRID_W
    r = np.repeat(np.arange(rows), GRID_W).astype(np.float64)
    col = np.tile(np.arange(GRID_W), rows).astype(np.float64)
    quarter = dim // 4
    inv = ROPE_BASE ** (-np.arange(quarter, dtype=np.float64) / quarter)
    ang_r = r[:, None] * inv
    ang_c = col[:, None] * inv
    cos = np.concatenate([np.cos(ang_r)] * 2 + [np.cos(ang_c)] * 2, axis=1)
    sin = np.concatenate([-np.sin(ang_r), np.sin(ang_r), -np.sin(ang_c), np.sin(ang_c)], axis=1)
    if lanes is not None:
        cos_b, sin_b = np.ones((n_tok, LANES)), np.zeros((n_tok, LANES))
        cos_b[:, lanes], sin_b[:, lanes] = cos, sin
        return jnp.asarray(cos_b, F32), jnp.asarray(sin_b, F32)
    reps = width // dim
    cos = np.concatenate([cos] * reps, axis=1)
    sin = np.concatenate([sin] * reps, axis=1)
    pad = ((0, 0), (lane0, LANES - lane0 - width))
    return jnp.asarray(np.pad(cos, pad, constant_values=1.0), F32), jnp.asarray(np.pad(sin, pad), F32)


def _place(a, lanes, width=LANES):
    src = np.full((width,), a.shape[-1], np.int32)
    src[np.asarray(lanes)] = np.arange(len(lanes))
    ext = jnp.concatenate([a, jnp.zeros(a.shape[:-1] + (1,), a.dtype)], axis=-1)
    return ext[..., src]


def _layer_weights(w_in, g_cq, w_uq, w_ukv):
    z = lambda r, c: jnp.zeros((r, c), F32)
    o = np.cumsum([0, A_WIDTH, A_KV_WIDTH, A_KV_WIDTH, B_WIDTH, C_Q_LORA, C_KV_LORA, C_ROPE])
    win_p = jnp.concatenate([
        w_in[:, o[0]:o[1]].reshape(D_MODEL, A_HEADS, HEAD_DIM)[:, A_HEAD_ORDER, :].reshape(D_MODEL, A_WIDTH),
        w_in[:, o[1]:o[4]],
        w_in[:, o[4]:o[5]], z(D_MODEL, CQ_PAD - C_Q_LORA),
        w_in[:, o[5]:o[6]],
        _place(w_in[:, o[6]:o[7]], C_ROPE_LANES) + _place(w_in[:, o[6]:o[7]], C_STATE_LANES),
    ], axis=1).astype(BF16)
    gcq_p = jnp.pad(g_cq, (0, CQ_PAD - C_Q_LORA)).reshape(1, CQ_PAD)
    hq = C_NOPE + C_ROPE
    wuq_h = w_uq.reshape(C_Q_LORA, C_HEADS, hq)
    wuq_p = _place(wuq_h, C_NOPE_LANES + C_ROPE_LANES)
    wuq_p = jnp.pad(wuq_p, ((0, CQ_PAD - C_Q_LORA), (0, 0), (0, 0)))
    wuq_p = wuq_p.reshape(CQ_PAD, C_HEADS * C_HEAD_PAD).astype(BF16)
    wukv_h = w_ukv.reshape(C_KV_LORA, C_HEADS, C_NOPE + C_V)
    wk = _place(wukv_h[:, :, :C_NOPE], C_NOPE_LANES)
    zv = jnp.zeros((C_KV_LORA, C_HEAD_PAD - C_V), F32)
    wv = [wukv_h[:, h, C_NOPE:] for h in range(C_HEADS)]
    wv = jnp.concatenate([jnp.concatenate([zv, w] if h % 2 else [w, zv], axis=1) for h, w in enumerate(wv)], axis=1)
    wukv_p = jnp.concatenate([wk.reshape(C_KV_LORA, CW), wv], axis=1).astype(BF16)
    return win_p, gcq_p, wuq_p, wukv_p


def kernel(x_prompt, x_sample, cache_win_k, cache_win_v, cache_mla_ckv, cache_mla_krope, c, c_ctx,
           w_ada, b_ada, g_mix, w_in, sink, g_cq, w_uq, g_ckv, w_ukv, w_out, g_ffn, w_ug, conv_w,
           conv_b, w_down, g_final):
    n_ctx = BATCH * SEQ
    n_lat = DEC_BATCH * DEC_SEQ
    xp = x_prompt.reshape(n_ctx, D_MODEL)
    xs = x_sample.reshape(n_lat, D_MODEL)

    cvecs = jnp.concatenate([c_ctx[None, :], c, jnp.zeros((8 - 1 - DEC_BATCH, D_MODEL), F32)], axis=0)
    mod = _modulation(cvecs, w_ada, b_ada)

    cos_a, sin_a = _rope_tables(DEC_SEQ, HEAD_DIM, 0, LANES)
    cos_c, sin_c = _rope_tables(DEC_SEQ, C_ROPE, 0, C_ROPE, lanes=C_ROPE_LANES)
    tables = (cos_a, sin_a, cos_c, sin_c)
    fc = _channel_dft()
    g_final2 = g_final.reshape(1, D_MODEL)
    wug_all = w_ug.astype(BF16)
    wdown_all = w_down.astype(BF16)
    conv_b_all = conv_b.reshape(DEPTH, 1, 2 * D_FF)

    new_k, new_v, new_ckv, new_kr = [], [], [], []
    for l in range(DEPTH):
        win_p, gcq_p, wuq_p, wukv_p = _layer_weights(w_in[l], g_cq[l], w_uq[l], w_ukv[l])
        mod_l = mod[l].reshape(8, 1, 6 * D_MODEL)
        g_mix_l = g_mix[l].reshape(1, D_MODEL)
        g_ckv_l = g_ckv[l].reshape(1, C_KV_LORA)
        g_ffn_l = g_ffn[l].reshape(1, D_MODEL)
        sink_l = sink[l].reshape(1, A_HEADS)
        wout_a = w_out[l, :A_WIDTH].reshape(A_HEADS, HEAD_DIM, D_MODEL)[jnp.array(A_HEAD_ORDER)]
        wout_l = jnp.concatenate([wout_a.reshape(A_WIDTH, D_MODEL), w_out[l, A_WIDTH:]], axis=0).astype(BF16)
        final = l == DEPTH - 1

        (qa, ka, va, fb, qc, kc, vc, ska, sva, sckv, skr) = _projection(
            False, xp, mod_l, g_mix_l, win_p, gcq_p, wuq_p, g_ckv_l, wukv_p, None)
        r3 = lambda a: a.reshape(BATCH, SEQ, a.shape[-1])
        oa = _attn_a_ctx(sink_l, r3(qa), ka, r3(va))
        ob = _fourier_ctx(r3(fb), fc)
        oc = _mla(r3(qc), r3(kc), r3(vc), None, None, SEQ, SEQ, group=CTX_GROUP)
        f2 = lambda a: a.reshape(n_ctx, a.shape[-1])
        xp = _ffn(False, final, l, xp, f2(oa), f2(ob), f2(oc), mod_l, g_ffn_l, wout_l, wug_all, conv_w, conv_b_all,
                  wdown_all, g_final2)
        new_k.append(ska.reshape(BATCH, SEQ, A_KV_HEADS, HEAD_DIM))
        new_v.append(sva.reshape(BATCH, SEQ, A_KV_HEADS, HEAD_DIM))
        new_ckv.append(sckv.reshape(BATCH, SEQ, C_KV_LORA))
        new_kr.append(skr.reshape(BATCH, SEQ, C_ROPE))

        (qa, ka, va, fb, qc, kc, vc) = _projection(
            True, xs, mod_l, g_mix_l, win_p, gcq_p, wuq_p, g_ckv_l, wukv_p, tables)
        r3 = lambda a: a.reshape(DEC_BATCH, DEC_SEQ, a.shape[-1])
        kwin = cache_win_k[:, l].reshape(DEC_BATCH * PAST_LEN, A_KV_WIDTH).T.astype(BF16)
        vwin = cache_win_v[:, l].reshape(DEC_BATCH, PAST_LEN, A_KV_WIDTH).astype(BF16)
        oa = _attn_a_lat(sink_l, r3(qa), ka, r3(va), kwin, vwin)
        ob = _fourier_lat(r3(fb), fc)
        kr_pad = _place(cache_mla_krope[:, l], C_ROPE_LANES)
        kc_ctx, vc_ctx = _kvcache(cache_mla_ckv[:, l], kr_pad, wukv_p)
        oc = _mla(r3(qc), r3(kc), r3(vc), kc_ctx, vc_ctx, TQ_MLA, TK_MLA)
        f2 = lambda a: a.reshape(n_lat, a.shape[-1])
        xs = _ffn(True, final, l, xs, f2(oa), f2(ob), f2(oc), mod_l, g_ffn_l, wout_l, wug_all, conv_w, conv_b_all,
                  wdown_all, g_final2)

    y_prompt = xp.reshape(BATCH, SEQ, D_MODEL)
    y_sample = xs.reshape(DEC_BATCH, DEC_SEQ, D_MODEL)
    return (y_prompt, y_sample, jnp.stack(new_k, axis=1), jnp.stack(new_v, axis=1),
            jnp.stack(new_ckv, axis=1), jnp.stack(new_kr, axis=1))
```
